```python
import functools
import jax
import jax.numpy as jnp
from jax import lax
import numpy as np

D_MODEL = 1024
BATCH = 4
SEQ = 4096
DEPTH = 1
DEC_BATCH = 128
DEC_SEQ = 4
PAST_LEN = 8192
PAGE_SIZE = 128

N_META = 16
HEAD_DIM = 64
H_A = D_MODEL // (2 * HEAD_DIM)
NOPE_A = HEAD_DIM
ROPE_A = HEAD_DIM // 2
DQK_A = NOPE_A + ROPE_A
V_A = HEAD_DIM
Q_LORA = D_MODEL // 2
KV_LORA = D_MODEL // 4
MLA_ROW = KV_LORA + ROPE_A
H_B = D_MODEL // (2 * HEAD_DIM)
KVH_B = max(1, H_B // 4)
GROUP_B = H_B // KVH_B
H_I = 8
D_I = 32
D_I_ROPE = D_I // 2
MAX_SEL = 256
N_BRANCH = 2
PROJ_SIZES = (Q_LORA, KV_LORA, ROPE_A, H_B * HEAD_DIM, KVH_B * HEAD_DIM, KVH_B * HEAD_DIM, H_I * D_I, D_I, H_I, N_BRANCH * D_MODEL)
N_IN = sum(PROJ_SIZES)
N_EXPERTS = 32
TOP_K = 4
D_FF = D_MODEL
SWIGLU_LIMIT = 7.0
SWIGLU_ALPHA = 1.702
ROPE_THETA = 10000.0
NORM_EPS = 1e-6
Q_BLOCK = 128
MOE_BLOCK = 128

kernel_name = 'meta_mla_dsa_gated_moe_step'


def rmsnorm(x, g):
    xf = x.astype(jnp.float32)
    y = xf * lax.rsqrt(jnp.mean(xf * xf, axis=-1, keepdims=True) + NORM_EPS)
    return (y * g.astype(jnp.float32)).astype(x.dtype)


def apply_rope(x, pos, n_rot):
    d = x.shape[-1]
    half = n_rot // 2
    inv_freq = ROPE_THETA ** (-jnp.arange(half, dtype=jnp.float32) / half)
    ang = pos.astype(jnp.float32)[:, None] * inv_freq[None, :]
    cos = jnp.cos(ang)[:, None, :]
    sin = jnp.sin(ang)[:, None, :]
    xf = x.astype(jnp.float32)
    x_pass = xf[..., :d - n_rot]
    x1 = xf[..., d - n_rot:d - half]
    x2 = xf[..., d - half:]
    out = jnp.concatenate([x_pass, x1 * cos - x2 * sin, x2 * cos + x1 * sin], axis=-1)
    return out.astype(x.dtype)


def token_projections(h, pos, w_in, g_cq, w_uq, g_qa, g_ckv, g_qb, g_kb, g_ik):
    lead = h.shape[:-1]
    z = jnp.dot(h, w_in)
    cuts = [int(c) for c in np.cumsum(PROJ_SIZES)[:-1]]
    z_cq, z_ckv, k_pe, z_qb, z_kb, z_vb, z_iq, z_ik, z_iw, z_g = jnp.split(z, cuts, axis=-1)
    q_a = jnp.einsum('btc,chd->bthd', rmsnorm(z_cq, g_cq), w_uq)
    q_a = apply_rope(rmsnorm(q_a, g_qa), pos, ROPE_A)
    c_kv = rmsnorm(z_ckv, g_ckv)
    q_b = apply_rope(rmsnorm(z_qb.reshape(lead + (H_B, HEAD_DIM)), g_qb), pos, HEAD_DIM)
    k_b = apply_rope(rmsnorm(z_kb.reshape(lead + (KVH_B, HEAD_DIM)), g_kb), pos, HEAD_DIM)
    v_b = z_vb.reshape(lead + (KVH_B, HEAD_DIM))
    i_q = apply_rope(z_iq.reshape(lead + (H_I, D_I)), pos, D_I_ROPE)
    i_k = apply_rope(rmsnorm(z_ik, g_ik)[..., None, :], pos, D_I_ROPE)[..., 0, :]
    i_w = z_iw * (H_I ** -0.5)
    gates = jax.nn.sigmoid(z_g).reshape(lead + (N_BRANCH, D_MODEL))
    return q_a, c_kv, k_pe, q_b, k_b, v_b, i_q, i_k, i_w, gates


def mla_keys(c_kv, k_pe, pos, w_uk, g_ka):
    k_nope = jnp.einsum('...lr,rhn->...lhn', c_kv, w_uk)
    k_pe_h = jnp.broadcast_to(k_pe[..., None, :], k_nope.shape[:-1] + (ROPE_A,))
    k = jnp.concatenate([k_nope, k_pe_h], axis=-1)
    return apply_rope(rmsnorm(k, g_ka), pos, ROPE_A)


def mla_probs(q, k, q_pos, k_pos):
    s = jnp.einsum('qhd,lhd->hql', q, k, preferred_element_type=jnp.float32) * (DQK_A ** -0.5)
    s = jnp.where((k_pos[None, :] <= q_pos[:, None])[None], s, -jnp.inf)
    return jax.nn.softmax(s, axis=-1)


def dsa_attend(q, k, v, i_q, i_k, i_w, q_pos, k_pos, n_sel):
    n_q = q.shape[0]
    logit = jnp.einsum('qhd,ld->qhl', i_q, i_k, preferred_element_type=jnp.float32) * (D_I ** -0.5)
    score = jnp.einsum('qh,qhl->ql', i_w.astype(jnp.float32), jax.nn.relu(logit))
    score = jnp.where(k_pos[None, :] <= q_pos[:, None], score, -jnp.inf)
    _, sel = lax.top_k(score, n_sel)
    valid = k_pos[sel] <= q_pos[:, None]
    k_sel = k[sel]
    v_sel = v[sel]
    qg = q.reshape(n_q, KVH_B, GROUP_B, HEAD_DIM)
    s = jnp.einsum('qkgd,qnkd->qkgn', qg, k_sel, preferred_element_type=jnp.float32) * (HEAD_DIM ** -0.5)
    s = jnp.where(valid[:, None, None, :], s, -jnp.inf)
    p = jax.nn.softmax(s, axis=-1).astype(v.dtype)
    o = jnp.einsum('qkgn,qnkd->qkgd', p, v_sel)
    return o.reshape(n_q, H_B, HEAD_DIM)


def prompt_attention(q_a, k_a, v_a, q_b, k_b, v_b, i_q, i_k, i_w, n_sel):
    t = q_a.shape[0]
    pos = jnp.arange(t)

    def block(args):
        qa_blk, qb_blk, iq_blk, iw_blk, q_pos = args
        p = mla_probs(qa_blk, k_a, q_pos, pos).astype(v_a.dtype)
        o_a = jnp.einsum('hql,lhv->qhv', p, v_a)
        o_b = dsa_attend(qb_blk, k_b, v_b, iq_blk, i_k, iw_blk, q_pos, pos, n_sel)
        return o_a, o_b

    q_args = (q_a, q_b, i_q, i_w, pos)
    head_a, head_b = block(tuple(a[:N_META] for a in q_args))
    n_blk = (t - N_META) // Q_BLOCK
    body_a, body_b = lax.map(block, tuple(a[N_META:].reshape((n_blk, Q_BLOCK) + a.shape[1:]) for a in q_args))
    o_a = jnp.concatenate([head_a, body_a.reshape((t - N_META,) + body_a.shape[2:])], axis=0)
    o_b = jnp.concatenate([head_b, body_b.reshape((t - N_META,) + body_b.shape[2:])], axis=0)
    return o_a, o_b


def sample_attention(args, cache_mla_l, cache_kv_l, cache_idx_l, w_uk_l, w_uv_l, g_ka_l, past, n_sel):
    pt, q_a, c_new, kpe_new, q_b, kb_new, vb_new, i_q, ik_new, i_w = args
    n_q = q_a.shape[0]
    k_pos = jnp.arange(past + n_q)
    q_pos = past + jnp.arange(n_q)
    mla_past = cache_mla_l[pt].reshape(past, MLA_ROW)
    c_kv = jnp.concatenate([mla_past[:, :KV_LORA], c_new], axis=0)
    k_pe = jnp.concatenate([mla_past[:, KV_LORA:], kpe_new], axis=0)
    k_a = mla_keys(c_kv, k_pe, k_pos, w_uk_l, g_ka_l)
    p = mla_probs(q_a, k_a, q_pos, k_pos).astype(c_kv.dtype)
    o_lat = jnp.einsum('hql,lr->qhr', p, c_kv)
    o_a = jnp.einsum('qhr,rhv->qhv', o_lat, w_uv_l)
    kv_past = cache_kv_l[pt].reshape(past, 2, KVH_B, HEAD_DIM)
    k_b = jnp.concatenate([kv_past[:, 0], kb_new], axis=0)
    v_b = jnp.concatenate([kv_past[:, 1], vb_new], axis=0)
    i_k = jnp.concatenate([cache_idx_l[pt].reshape(past, D_I), ik_new], axis=0)
    o_b = dsa_attend(q_b, k_b, v_b, i_q, i_k, i_w, q_pos, k_pos, n_sel)
    return o_a, o_b


def branch_merge(o_a, o_b, gates, w_pa, w_pb, w_o):
    lead = o_a.shape[:-2]
    y_a = jnp.dot(o_a.reshape(lead + (-1,)), w_pa)
    y_b = jnp.dot(o_b.reshape(lead + (-1,)), w_pb)
    return jnp.dot(gates[..., 0, :] * y_a + gates[..., 1, :] * y_b, w_o)


def moe_ffn(h, w_router, b_router, w_mlp1, b_mlp1, w_mlp2, b_mlp2):
    n, d = h.shape
    logits = jnp.dot(h, w_router, preferred_element_type=jnp.float32) + b_router.astype(jnp.float32)
    top_val, top_idx = lax.top_k(logits, TOP_K)
    gate = jax.nn.softmax(top_val, axis=-1).astype(h.dtype).reshape(-1)
    flat_e = top_idx.reshape(-1)
    nk = flat_e.shape[0]
    order = jnp.argsort(flat_e, stable=True)
    e_sorted = flat_e[order]
    tok_sorted = (order // TOP_K).astype(jnp.int32)
    counts = jnp.bincount(flat_e, length=N_EXPERTS)
    padded = (counts + MOE_BLOCK - 1) // MOE_BLOCK * MOE_BLOCK
    pad_end = jnp.cumsum(padded)
    pad_start = pad_end - padded
    start = jnp.cumsum(counts) - counts
    dest = pad_start[e_sorted] + jnp.arange(nk) - start[e_sorted]
    n_blocks = -(-nk // MOE_BLOCK) + N_EXPERTS
    row_tok = jnp.full((n_blocks * MOE_BLOCK,), n, jnp.int32).at[dest].set(tok_sorted)
    h_pad = jnp.concatenate([h, jnp.zeros((1, d), h.dtype)], axis=0)
    xb = h_pad[row_tok].reshape(n_blocks, MOE_BLOCK, d)
    block_e = jnp.minimum(jnp.searchsorted(pad_end, jnp.arange(n_blocks) * MOE_BLOCK, side='right'), N_EXPERTS - 1)

    def expert_block(args):
        x_blk, e = args
        u = jnp.dot(x_blk, w_mlp1[e]) + b_mlp1[e]
        g, up = u[:, :D_FF], u[:, D_FF:]
        g = jnp.minimum(g, SWIGLU_LIMIT)
        up = jnp.clip(up, -SWIGLU_LIMIT, SWIGLU_LIMIT)
        a = g * jax.nn.sigmoid(SWIGLU_ALPHA * g) * (up + 1.0)
        return jnp.dot(a, w_mlp2[e]) + b_mlp2[e]

    yb = lax.map(expert_block, (xb, block_e)).reshape(-1, d)
    y = yb[dest] * gate[order][:, None]
    return jax.ops.segment_sum(y, tok_sorted, num_segments=n)


def setup_inputs(seed: int = 0) -> dict:
    key = jax.random.key(seed)
    ks = iter(jax.random.split(key, 40))

    def nrm(shape, scale=1.0):
        return jax.random.normal(next(ks), shape, jnp.float32) * scale

    def gain(shape):
        return 1.0 + 0.02 * nrm(shape)

    n_pages = PAST_LEN // PAGE_SIZE
    used = DEC_BATCH * n_pages
    n_pool = used + max(1, used // 4)
    page_table = jax.random.permutation(next(ks), n_pool)[:used].reshape(DEC_BATCH, n_pages).astype(jnp.int32)
    L = DEPTH
    return {
        'x_prompt': nrm((BATCH, SEQ, D_MODEL)),
        'x_sample': nrm((DEC_BATCH, DEC_SEQ, D_MODEL)),
        'cache_mla': nrm((L, n_pool, PAGE_SIZE, MLA_ROW)),
        'cache_dsa_kv': nrm((L, n_pool, PAGE_SIZE, 2, KVH_B, HEAD_DIM)),
        'cache_dsa_idx': nrm((L, n_pool, PAGE_SIZE, D_I)),
        'page_table': page_table,
        'meta_tokens': nrm((N_META, D_MODEL)),
        'g_attn_norm': gain((L, D_MODEL)),
        'w_in': nrm((L, D_MODEL, N_IN), D_MODEL ** -0.5),
        'g_cq': gain((L, Q_LORA)),
        'w_uq': nrm((L, Q_LORA, H_A, DQK_A), Q_LORA ** -0.5),
        'g_qa': gain((L, DQK_A)),
        'g_ckv': gain((L, KV_LORA)),
        'w_uk': nrm((L, KV_LORA, H_A, NOPE_A), KV_LORA ** -0.5),
        'w_uv': nrm((L, KV_LORA, H_A, V_A), KV_LORA ** -0.5),
        'g_ka': gain((L, DQK_A)),
        'g_qb': gain((L, HEAD_DIM)),
        'g_kb': gain((L, HEAD_DIM)),
        'g_ik': gain((L, D_I)),
        'w_pa': nrm((L, H_A * V_A, D_MODEL), (H_A * V_A) ** -0.5),
        'w_pb': nrm((L, H_B * HEAD_DIM, D_MODEL), (H_B * HEAD_DIM) ** -0.5),
        'w_o': nrm((L, D_MODEL, D_MODEL), D_MODEL ** -0.5),
        'g_ffn_norm': gain((L, D_MODEL)),
        'w_router': nrm((L, D_MODEL, N_EXPERTS), D_MODEL ** -0.5),
        'b_router': nrm((L, N_EXPERTS), 0.01),
        'w_mlp1': nrm((L, N_EXPERTS, D_MODEL, 2 * D_FF), D_MODEL ** -0.5),
        'b_mlp1': nrm((L, N_EXPERTS, 2 * D_FF), 0.02),
        'w_mlp2': nrm((L, N_EXPERTS, D_FF, D_MODEL), D_FF ** -0.5),
        'b_mlp2': nrm((L, N_EXPERTS, D_MODEL), 0.02),
    }


def reference(x_prompt, x_sample, cache_mla, cache_dsa_kv, cache_dsa_idx, page_table, meta_tokens,
              g_attn_norm, w_in, g_cq, w_uq, g_qa, g_ckv, w_uk, w_uv, g_ka, g_qb, g_kb, g_ik,
              w_pa, w_pb, w_o, g_ffn_norm, w_router, b_router, w_mlp1, b_mlp1, w_mlp2, b_mlp2):
    n_b, n_s, d = x_prompt.shape
    n_t = n_s + N_META
    n_ds = x_sample.shape[1]
    past = page_table.shape[1] * PAGE_SIZE
    pos_p = jnp.arange(n_t)
    pos_s = past + jnp.arange(n_ds)
    n_sel_p = min(MAX_SEL, n_s // 4)
    n_sel_s = min(MAX_SEL, (past + n_ds) // 4)
    meta = jnp.broadcast_to(meta_tokens[None].astype(x_prompt.dtype), (n_b, N_META, d))
    xp = jnp.concatenate([meta, x_prompt], axis=1)
    xs = x_sample
    mla_p, kv_p, idx_p, mla_s, kv_s, idx_s = [], [], [], [], [], []
    for l in range(DEPTH):
        proj = functools.partial(token_projections, w_in=w_in[l], g_cq=g_cq[l], w_uq=w_uq[l], g_qa=g_qa[l],
                                 g_ckv=g_ckv[l], g_qb=g_qb[l], g_kb=g_kb[l], g_ik=g_ik[l])
        moe = functools.partial(moe_ffn, w_router=w_router[l], b_router=b_router[l], w_mlp1=w_mlp1[l],
                                b_mlp1=b_mlp1[l], w_mlp2=w_mlp2[l], b_mlp2=b_mlp2[l])
        q_a, c_kv, k_pe, q_b, k_b, v_b, i_q, i_k, i_w, gates = proj(rmsnorm(xp, g_attn_norm[l]), pos_p)
        k_a = mla_keys(c_kv, k_pe, pos_p, w_uk[l], g_ka[l])
        v_a = jnp.einsum('btr,rhv->bthv', c_kv, w_uv[l])
        o_a, o_b = jax.vmap(functools.partial(prompt_attention, n_sel=n_sel_p))(q_a, k_a, v_a, q_b, k_b, v_b, i_q, i_k, i_w)
        xp = xp + branch_merge(o_a, o_b, gates, w_pa[l], w_pb[l], w_o[l])
        xp = xp + moe(rmsnorm(xp, g_ffn_norm[l]).reshape(-1, d)).reshape(xp.shape)
        mla_p.append(jnp.concatenate([c_kv, k_pe], axis=-1))
        kv_p.append(jnp.stack([k_b, v_b], axis=2))
        idx_p.append(i_k)
        q_a, c_kv, k_pe, q_b, k_b, v_b, i_q, i_k, i_w, gates = proj(rmsnorm(xs, g_attn_norm[l]), pos_s)
        step = functools.partial(sample_attention, cache_mla_l=cache_mla[l], cache_kv_l=cache_dsa_kv[l],
                                 cache_idx_l=cache_dsa_idx[l], w_uk_l=w_uk[l], w_uv_l=w_uv[l], g_ka_l=g_ka[l],
                                 past=past, n_sel=n_sel_s)
        o_a, o_b = lax.map(step, (page_table, q_a, c_kv, k_pe, q_b, k_b, v_b, i_q, i_k, i_w))
        xs = xs + branch_merge(o_a, o_b, gates, w_pa[l], w_pb[l], w_o[l])
        xs = xs + moe(rmsnorm(xs, g_ffn_norm[l]).reshape(-1, d)).reshape(xs.shape)
        mla_s.append(jnp.concatenate([c_kv, k_pe], axis=-1))
        kv_s.append(jnp.stack([k_b, v_b], axis=2))
        idx_s.append(i_k)
    y_prompt = xp[:, N_META:]
    y_sample = xs
    new_mla_p = jnp.stack(mla_p, axis=0)
    new_dsa_kv_p = jnp.stack(kv_p, axis=0)
    new_dsa_idx_p = jnp.stack(idx_p, axis=0)
    new_mla_s = jnp.stack(mla_s, axis=0)
    new_dsa_kv_s = jnp.stack(kv_s, axis=0)
    new_dsa_idx_s = jnp.stack(idx_s, axis=0)
    return (y_prompt, y_sample, new_mla_p, new_dsa_kv_p, new_dsa_idx_p, new_mla_s, new_dsa_kv_s, new_dsa_idx_s)
```

```python
import functools

import jax
import jax.numpy as jnp
import numpy as np
from jax import lax
from jax.experimental import pallas as pl
from jax.experimental.pallas import tpu as pltpu

N_META = 16
HEAD_DIM = 64
H_A = 8
NOPE_A = 64
ROPE_A = 32
DQK_A = NOPE_A + ROPE_A
V_A = 64
Q_LORA = 512
KV_LORA = 256
H_B = 8
KVH_B = 2
GROUP_B = H_B // KVH_B
H_I = 8
D_I = 32
D_I_ROPE = 16
MAX_SEL = 256
N_EXPERTS = 32
TOP_K = 4
SWIGLU_LIMIT = 7.0
SWIGLU_ALPHA = 1.702
ROPE_THETA = 10000.0
NORM_EPS = 1e-6
PAGE_SIZE = 128

LANE = 128
MASK_VALUE = -1e30
INT_MIN = -(2 ** 31)
VMEM_LIMIT = 56 * 1024 * 1024
MAX_PAGES_PER_STEP = 16

COL_CQ = 0
COL_CKV = 512
COL_QB = 768
COL_KB = 1280
COL_VB = 1408
COL_IQ = 1536
COL_MISC = 1792
COL_GATE = 2048
N_PROJ = 4096

F32 = jnp.float32
BF16 = jnp.bfloat16
NT_DIMS = (((1,), (1,)), ((), ()))


def _cdiv(a, b):
    return (a + b - 1) // b


def _round_up(a, b):
    return _cdiv(a, b) * b


def _params(*sem):
    return pltpu.CompilerParams(dimension_semantics=sem, vmem_limit_bytes=VMEM_LIMIT)


def _mm_kernel(x_ref, w_ref, o_ref):
    o_ref[...] = jnp.dot(x_ref[...].astype(BF16), w_ref[...], preferred_element_type=F32).astype(o_ref.dtype)


def _mm_norm_kernel(x_ref, g_ref, w_ref, o_ref, xn_ref):
    @pl.when(pl.program_id(1) == 0)
    def _():
        x = x_ref[...].astype(F32)
        ms = jnp.mean(x * x, axis=-1, keepdims=True)
        xn_ref[...] = (x * lax.rsqrt(ms + NORM_EPS) * g_ref[...]).astype(BF16)

    o_ref[...] = jnp.dot(xn_ref[...], w_ref[...], preferred_element_type=F32).astype(o_ref.dtype)


def _matmul(x, w, *, gain=None, col_block=0, k=None, tm=512, tn=512, out_dtype=F32, name="mm"):
    m = x.shape[0]
    k = x.shape[1] if k is None else k
    n = w.shape[1]
    tn = min(tn, n)
    assert m % tm == 0 and n % tn == 0 and w.shape[0] == k
    grid = (m // tm, n // tn)
    x_spec = pl.BlockSpec((tm, k), lambda i, j: (i, col_block))
    w_spec = pl.BlockSpec((k, tn), lambda i, j: (0, j))
    o_spec = pl.BlockSpec((tm, tn), lambda i, j: (i, j))
    out_shape = jax.ShapeDtypeStruct((m, n), out_dtype)
    if gain is None:
        return pl.pallas_call(_mm_kernel, out_shape=out_shape, grid=grid, in_specs=[x_spec, w_spec],
                              out_specs=o_spec, compiler_params=_params("parallel", "arbitrary"), name=name)(x, w)
    g_spec = pl.BlockSpec((1, k), lambda i, j: (0, 0))
    return pl.pallas_call(_mm_norm_kernel, out_shape=out_shape, grid=grid, in_specs=[x_spec, g_spec, w_spec],
                          out_specs=o_spec, scratch_shapes=[pltpu.VMEM((tm, k), BF16)],
                          compiler_params=_params("parallel", "arbitrary"), name=name)(
                              x, gain.reshape(1, k).astype(F32), w)


def _softmax_step(s, m_prev, l_prev, acc_prev, v_blk):
    m_new = jnp.maximum(m_prev, jnp.max(s, axis=1, keepdims=True))
    alpha = jnp.exp(m_prev - m_new)
    p = jnp.exp(s - m_new)
    l_new = alpha * l_prev + jnp.sum(p, axis=1, keepdims=True)
    acc_new = alpha * acc_prev + jnp.dot(p.astype(BF16), v_blk, preferred_element_type=F32)
    return m_new, l_new, acc_new


def _sortable_key(x):
    b = lax.bitcast_convert_type(x + 0.0, jnp.int32)
    return jnp.where(b < 0, b ^ jnp.int32(0x7FFFFFFF), b)


def _kth_largest_key(count_ge, n_sel, rows):
    def step(s, t):
        bit = lax.shift_left(jnp.int32(1), jnp.int32(31) - s)
        cand = t + bit
        return jnp.where(count_ge(cand) >= n_sel, cand, t)

    return lax.fori_loop(0, 32, step, jnp.full((rows, 1), INT_MIN, jnp.int32))


def _mla_prompt_kernel(q_ref, k_ref, v_ref, o_ref, *, tq, tk):
    i = pl.program_id(2)
    n_c = ((i + 1) * tq + tk - 1) // tk
    q = q_ref[...]
    q_pos = i * tq + lax.broadcasted_iota(jnp.int32, (tq, tk), 0)

    def body(c, carry):
        start = pl.multiple_of(c * tk, tk)
        k_blk = k_ref[pl.ds(start, tk), :]
        v_blk = v_ref[pl.ds(start, tk), :]
        mask = (start + lax.broadcasted_iota(jnp.int32, (tq, tk), 1)) <= q_pos
        out = []
        for h in range(2):
            m, l, acc = carry[3 * h:3 * h + 3]
            s = lax.dot_general(q[:, h * LANE:(h + 1) * LANE], k_blk[:, h * LANE:(h + 1) * LANE], NT_DIMS,
                                preferred_element_type=F32)
            s = jnp.where(mask, s, MASK_VALUE)
            out.extend(_softmax_step(s, m, l, acc, v_blk))
        return tuple(out)

    m0 = jnp.full((tq, 1), MASK_VALUE, F32)
    l0 = jnp.zeros((tq, 1), F32)
    a0 = jnp.zeros((tq, LANE), F32)
    res = lax.fori_loop(0, n_c, body, (m0, l0, a0, m0, l0, a0))
    o_h0 = res[2] / res[1]
    o_h1 = res[5] / res[4]
    lane = lax.broadcasted_iota(jnp.int32, (tq, LANE), 1)
    o_ref[...] = jnp.where(lane < V_A, o_h0, o_h1).astype(o_ref.dtype)


def _mla_prompt(q, k, v, *, tq, tk):
    b, t, _ = q.shape
    grid = (b, H_A // 2, t // tq)
    return pl.pallas_call(
        functools.partial(_mla_prompt_kernel, tq=tq, tk=tk),
        out_shape=jax.ShapeDtypeStruct((b, t, H_A * V_A), BF16),
        grid=grid,
        in_specs=[pl.BlockSpec((None, tq, 2 * LANE), lambda bi, g, i: (bi, i, g)),
                  pl.BlockSpec((None, t, 2 * LANE), lambda bi, g, i: (bi, 0, g)),
                  pl.BlockSpec((None, t, 2 * V_A), lambda bi, g, i: (bi, 0, g))],
        out_specs=pl.BlockSpec((None, tq, 2 * V_A), lambda bi, g, i: (bi, i, g)),
        compiler_params=_params("parallel", "parallel", "arbitrary"),
        name="mla_prompt")(q, k, v)


def _select_bias(key, thr, need, carry, valid, tri_ref):
    w = min(tri_ref.shape[0], key.shape[1])
    tri = tri_ref[0:w, 0:w]
    parts = []
    for t in range(key.shape[1] // w):
        k_t = key[:, t * w:(t + 1) * w]
        eq = k_t == thr
        rank = jnp.dot(jnp.where(eq, 1.0, 0.0).astype(BF16), tri, preferred_element_type=F32) + carry
        carry = jnp.max(rank, axis=1, keepdims=True)
        tie = jnp.where(rank <= need, 0.0, MASK_VALUE)
        parts.append(jnp.where(k_t > thr, 0.0, jnp.where(eq, tie, MASK_VALUE)))
    bias = parts[0] if len(parts) == 1 else jnp.concatenate(parts, axis=1)
    if valid is not None:
        bias = jnp.where(valid, bias, MASK_VALUE)
    return bias, carry


def _dsa_prompt_kernel(iq_ref, w_ref, q_ref, ik_ref, k_ref, v_ref, tri_ref, o_ref, key_ref, bias_ref, wb_ref,
                       *, tq, tk, n_sel):
    i = pl.program_id(1)
    n_c = ((i + 1) * tq + tk - 1) // tk
    q_pos = i * tq + lax.broadcasted_iota(jnp.int32, (tq, tk), 0)
    lane_pos = lax.broadcasted_iota(jnp.int32, (tq, tk), 1)

    w = w_ref[...]
    for h in range(H_I):
        wb_ref[h] = jnp.broadcast_to(w[:, h:h + 1], (tq, LANE))
    iq = iq_ref[...].reshape(H_I * tq, D_I)

    def score_chunk(c, _):
        start = pl.multiple_of(c * tk, tk)
        logit = lax.dot_general(iq, ik_ref[pl.ds(start, tk), :], NT_DIMS, preferred_element_type=F32)
        cols = []
        for t in range(tk // LANE):
            acc = jnp.zeros((tq, LANE), F32)
            for h in range(H_I):
                acc = acc + jnp.maximum(logit[h * tq:(h + 1) * tq, t * LANE:(t + 1) * LANE], 0.0) * wb_ref[h]
            cols.append(acc)
        score = jnp.concatenate(cols, axis=1)
        score = jnp.where(start + lane_pos <= q_pos, score, -jnp.inf)
        key_ref[c] = _sortable_key(score)
        return 0

    lax.fori_loop(0, n_c, score_chunk, 0)

    def count_ge(cand):
        cand_b = jnp.broadcast_to(cand, (tq, LANE))

        def body(c, cnt):
            key = key_ref[c]
            for t in range(tk // LANE):
                cnt = cnt + jnp.where(key[:, t * LANE:(t + 1) * LANE] >= cand_b, 1, 0)
            return cnt

        cnt = lax.fori_loop(0, n_c, body, jnp.zeros((tq, LANE), jnp.int32))
        return jnp.sum(cnt, axis=1, keepdims=True)

    thr = _kth_largest_key(count_ge, n_sel, tq)
    need = (n_sel - count_ge(thr + 1)).astype(F32)

    def bias_chunk(c, carry):
        start = pl.multiple_of(c * tk, tk)
        bias, carry = _select_bias(key_ref[c], thr, need, carry, start + lane_pos <= q_pos, tri_ref)
        bias_ref[c] = bias
        return carry

    lax.fori_loop(0, n_c, bias_chunk, jnp.zeros((tq, 1), F32))

    rows = GROUP_B * tq
    for g in range(KVH_B):
        qg = q_ref[g * GROUP_B:(g + 1) * GROUP_B].reshape(rows, LANE)

        def attend(c, carry):
            start = pl.multiple_of(c * tk, tk)
            s = lax.dot_general(qg, k_ref[pl.ds(start, tk), :], NT_DIMS, preferred_element_type=F32)
            s = (s.reshape(GROUP_B, tq, tk) + bias_ref[c][None]).reshape(rows, tk)
            return _softmax_step(s, *carry, v_ref[pl.ds(start, tk), :])

        m, l, acc = lax.fori_loop(0, n_c, attend, (jnp.full((rows, 1), MASK_VALUE, F32),
                                                   jnp.zeros((rows, 1), F32), jnp.zeros((rows, LANE), F32)))
        o_ref[g * GROUP_B:(g + 1) * GROUP_B] = (acc / l).reshape(GROUP_B, tq, LANE).astype(o_ref.dtype)


def _dsa_prompt(iq, iw, q, ik, k, v, *, tq, tk, n_sel):
    b, nq = iq.shape[:2]
    t = ik.shape[1]
    n_chunks = t // tk
    tri_w = 2 * LANE if tk % (2 * LANE) == 0 else LANE
    tri = jnp.asarray(np.arange(tri_w)[:, None] <= np.arange(tri_w)[None, :], BF16)
    return pl.pallas_call(
        functools.partial(_dsa_prompt_kernel, tq=tq, tk=tk, n_sel=n_sel),
        out_shape=jax.ShapeDtypeStruct((b, nq, H_B, tq, LANE), BF16),
        grid=(b, nq),
        in_specs=[pl.BlockSpec((None, None, H_I, tq, D_I), lambda bi, i: (bi, i, 0, 0, 0)),
                  pl.BlockSpec((None, tq, H_I), lambda bi, i: (bi, i, 0)),
                  pl.BlockSpec((None, None, H_B, tq, LANE), lambda bi, i: (bi, i, 0, 0, 0)),
                  pl.BlockSpec((None, t, D_I), lambda bi, i: (bi, 0, 0)),
                  pl.BlockSpec((None, t, LANE), lambda bi, i: (bi, 0, 0)),
                  pl.BlockSpec((None, t, LANE), lambda bi, i: (bi, 0, 0)),
                  pl.BlockSpec((tri_w, tri_w), lambda bi, i: (0, 0))],
        out_specs=pl.BlockSpec((None, None, H_B, tq, LANE), lambda bi, i: (bi, i, 0, 0, 0)),
        scratch_shapes=[pltpu.VMEM((n_chunks, tq, tk), jnp.int32),
                        pltpu.VMEM((n_chunks, tq, tk), F32),
                        pltpu.VMEM((H_I, tq, LANE), F32)],
        compiler_params=_params("parallel", "arbitrary"),
        name="dsa_prompt")(iq, iw, q, ik, k, v, tri)


def _page_specs(layer, pages, width):
    def spec(p):
        return pl.BlockSpec((None, None, PAGE_SIZE, width), lambda b, j, pt: (layer, pt[b, j * pages + p], 0, 0))
    return [spec(p) for p in range(pages)]


def _mla_sample_kernel(pt_ref, *refs, pages, n_new):
    page_refs = refs[:pages]
    (wt_ref, qbd_ref, qr_ref, qs_ref, cg_ref, sg_ref, c_new_ref, pe_new_ref, cg_new_ref, sg_new_ref,
     o_ref, lhs_ref, c_buf, pe_buf, m_ref, l_ref, acc_ref) = refs[pages:]
    j = pl.program_id(1)
    n_heads = H_A
    rows = o_ref.shape[0]

    @pl.when(j == 0)
    def _():
        wt = wt_ref[...]
        lhs_ref[0:n_heads * NOPE_A, :] = wt
        lhs_ref[n_heads * NOPE_A:, :] = jnp.dot(qbd_ref[...], wt, preferred_element_type=F32).astype(BF16)
        m_ref[...] = jnp.full(m_ref.shape, MASK_VALUE, F32)
        l_ref[...] = jnp.zeros(l_ref.shape, F32)
        acc_ref[...] = jnp.zeros(acc_ref.shape, F32)

    ones = jnp.ones((8, ROPE_A), BF16)

    def attend(c_blk, pe, cg, sg, mask):
        n = c_blk.shape[0]
        big = lax.dot_general(lhs_ref[...], c_blk, NT_DIMS, preferred_element_type=F32)
        s_nope = big[n_heads * NOPE_A:]
        ssq = jnp.zeros((8, n), F32)
        row_id = lax.broadcasted_iota(jnp.int32, (8, n), 0)
        for h in range(n_heads):
            kh = big[h * NOPE_A:(h + 1) * NOPE_A]
            ssq = jnp.where(row_id == h, jnp.sum(kh * kh, axis=0, keepdims=True), ssq)
        pe2 = pe * pe
        pe2_hi = pe2.astype(BF16)
        pe2_lo = (pe2 - pe2_hi.astype(F32)).astype(BF16)
        ssq = ssq + (lax.dot_general(ones, pe2_hi, NT_DIMS, preferred_element_type=F32)
                     + lax.dot_general(ones, pe2_lo, NT_DIMS, preferred_element_type=F32))
        inv_rms = lax.rsqrt(ssq * (1.0 / DQK_A) + NORM_EPS)
        s_rope = (lax.dot_general(qr_ref[...], (pe * cg).astype(BF16), NT_DIMS, preferred_element_type=F32)
                  + lax.dot_general(qs_ref[...], (pe * sg).astype(BF16), NT_DIMS, preferred_element_type=F32))
        s = ((s_nope + s_rope).reshape(rows // n_heads, n_heads, n) * inv_rms[None]).reshape(rows, n)
        if mask is not None:
            s = jnp.where(mask, s, MASK_VALUE)
        m, l, acc = _softmax_step(s, m_ref[...], l_ref[...], acc_ref[...], c_blk)
        m_ref[...] = m
        l_ref[...] = l
        acc_ref[...] = acc

    for p in range(pages):
        pg = page_refs[p][...]
        c_buf[p * PAGE_SIZE:(p + 1) * PAGE_SIZE, :] = pg[:, :KV_LORA].astype(BF16)
        pe_buf[p * PAGE_SIZE:(p + 1) * PAGE_SIZE, :] = pg[:, KV_LORA:]
    attend(c_buf[...], pe_buf[...], cg_ref[...], sg_ref[...], None)

    @pl.when(j == pl.num_programs(1) - 1)
    def _():
        n = c_new_ref.shape[0]
        key_i = lax.broadcasted_iota(jnp.int32, (rows, n), 1)
        q_i = lax.broadcasted_iota(jnp.int32, (rows, n), 0) // n_heads
        attend(c_new_ref[...].astype(BF16), pe_new_ref[...], cg_new_ref[...], sg_new_ref[...],
               (key_i <= q_i) & (key_i < n_new))
        o_ref[...] = acc_ref[...] / l_ref[...]


def _mla_sample(layer, page_table, cache, wt, qbd, qr, qs, cg, sg, c_new, pe_new, cg_new, sg_new, *, pages, n_new):
    n_seq, n_pages = page_table.shape
    rows = qbd.shape[1]
    n_main = pages * PAGE_SIZE
    n_tail = c_new.shape[1]
    width = KV_LORA + ROPE_A
    const = lambda shape: pl.BlockSpec(shape, lambda b, j, pt: tuple(0 for _ in shape))
    per_seq = lambda shape: pl.BlockSpec((None,) + shape, lambda b, j, pt: (b,) + tuple(0 for _ in shape))
    in_specs = _page_specs(layer, pages, width) + [
        const((H_A * NOPE_A, KV_LORA)),
        per_seq((rows, H_A * NOPE_A)), per_seq((rows, ROPE_A)), per_seq((rows, ROPE_A)),
        pl.BlockSpec((n_main, ROPE_A), lambda b, j, pt: (j, 0)),
        pl.BlockSpec((n_main, ROPE_A), lambda b, j, pt: (j, 0)),
        per_seq((n_tail, KV_LORA)), per_seq((n_tail, ROPE_A)),
        const((n_tail, ROPE_A)), const((n_tail, ROPE_A))]
    return pl.pallas_call(
        functools.partial(_mla_sample_kernel, pages=pages, n_new=n_new),
        out_shape=jax.ShapeDtypeStruct((n_seq, rows, KV_LORA), F32),
        grid_spec=pltpu.PrefetchScalarGridSpec(
            num_scalar_prefetch=1, grid=(n_seq, n_pages // pages), in_specs=in_specs,
            out_specs=pl.BlockSpec((None, rows, KV_LORA), lambda b, j, pt: (b, 0, 0)),
            scratch_shapes=[pltpu.VMEM((H_A * NOPE_A + rows, KV_LORA), BF16),
                            pltpu.VMEM((n_main, KV_LORA), BF16),
                            pltpu.VMEM((n_main, ROPE_A), F32),
                            pltpu.VMEM((rows, 1), F32), pltpu.VMEM((rows, 1), F32),
                            pltpu.VMEM((rows, KV_LORA), F32)]),
        compiler_params=_params("parallel", "arbitrary"),
        name="mla_sample")(page_table, *([cache] * pages), wt, qbd, qr, qs, cg, sg, c_new, pe_new, cg_new, sg_new)


def _dsa_select_kernel(pt_ref, *refs, pages, n_new, n_sel):
    page_refs = refs[:pages]
    (iq_ref, w_ref, ik_new_ref, tri_ref, bias_ref, bias_new_ref, ik_buf, key_ref, key_new_ref) = refs[pages:]
    j = pl.program_id(1)
    n_steps = pl.num_programs(1)
    n_main = ik_buf.shape[0]
    n_tail = ik_new_ref.shape[0]
    nq = 8

    def scores(ik_blk):
        logit = lax.dot_general(iq_ref[...], ik_blk, NT_DIMS, preferred_element_type=F32)
        weighted = jnp.maximum(logit, 0.0) * w_ref[...]
        return jnp.sum(weighted.reshape(H_I, nq, ik_blk.shape[0]), axis=0)

    for p in range(pages):
        ik_buf[p * PAGE_SIZE:(p + 1) * PAGE_SIZE, :] = page_refs[p][...].astype(BF16)
    key_ref[j] = _sortable_key(scores(ik_buf[...]))

    @pl.when(j == n_steps - 1)
    def _():
        key_i = lax.broadcasted_iota(jnp.int32, (nq, n_tail), 1)
        q_i = lax.broadcasted_iota(jnp.int32, (nq, n_tail), 0) % n_new
        valid_new = (key_i <= q_i) & (key_i < n_new)
        score_new = jnp.where(valid_new, scores(ik_new_ref[...]), -jnp.inf)
        key_new_ref[...] = _sortable_key(score_new)

        def count_ge(cand):
            def body(c, cnt):
                key = key_ref[c]
                for t in range(n_main // LANE):
                    cnt = cnt + jnp.where(key[:, t * LANE:(t + 1) * LANE] >= cand, 1, 0)
                return cnt

            cnt = lax.fori_loop(0, n_steps, body, jnp.zeros((nq, LANE), jnp.int32))
            key_new = key_new_ref[...]
            for t in range(n_tail // LANE):
                cnt = cnt + jnp.where(key_new[:, t * LANE:(t + 1) * LANE] >= cand, 1, 0)
            return jnp.sum(cnt, axis=1, keepdims=True)

        thr = _kth_largest_key(count_ge, n_sel, nq)
        need = (n_sel - count_ge(thr + 1)).astype(F32)

        def bias_chunk(c, carry):
            bias, carry = _select_bias(key_ref[c], thr, need, carry, None, tri_ref)
            bias_ref[c] = bias
            return carry

        carry = lax.fori_loop(0, n_steps, bias_chunk, jnp.zeros((nq, 1), F32))
        bias_new, _ = _select_bias(key_new_ref[...], thr, need, carry, valid_new, tri_ref)
        bias_new_ref[...] = bias_new


def _dsa_select(layer, page_table, cache_idx, iq, iw, ik_new, *, pages, n_new, n_sel):
    n_seq, n_pages = page_table.shape
    n_steps = n_pages // pages
    n_main = pages * PAGE_SIZE
    n_tail = ik_new.shape[1]
    rows = iq.shape[1]
    tri = jnp.asarray(np.arange(2 * LANE)[:, None] <= np.arange(2 * LANE)[None, :], BF16)
    per_seq = lambda shape: pl.BlockSpec((None,) + shape, lambda b, j, pt: (b,) + tuple(0 for _ in shape))
    in_specs = _page_specs(layer, pages, D_I) + [
        per_seq((rows, D_I)), per_seq((rows, 1)), per_seq((n_tail, D_I)),
        pl.BlockSpec((2 * LANE, 2 * LANE), lambda b, j, pt: (0, 0))]
    return pl.pallas_call(
        functools.partial(_dsa_select_kernel, pages=pages, n_new=n_new, n_sel=n_sel),
        out_shape=(jax.ShapeDtypeStruct((n_seq, n_steps, 8, n_main), F32),
                   jax.ShapeDtypeStruct((n_seq, 8, n_tail), F32)),
        grid_spec=pltpu.PrefetchScalarGridSpec(
            num_scalar_prefetch=1, grid=(n_seq, n_steps), in_specs=in_specs,
            out_specs=(pl.BlockSpec((None, n_steps, 8, n_main), lambda b, j, pt: (b, 0, 0, 0)),
                       pl.BlockSpec((None, 8, n_tail), lambda b, j, pt: (b, 0, 0))),
            scratch_shapes=[pltpu.VMEM((n_main, D_I), BF16),
                            pltpu.VMEM((n_steps, 8, n_main), jnp.int32),
                            pltpu.VMEM((8, n_tail), jnp.int32)]),
        compiler_params=_params("parallel", "arbitrary"),
        name="dsa_select")(page_table, *([cache_idx] * pages), iq, iw, ik_new, tri)


def _dsa_sample_kernel(pt_ref, *refs, pages):
    page_refs = refs[:pages]
    (q_ref, bias_ref, kv_new_ref, bias_new_ref, o_ref, k_buf, v_buf, m_ref, l_ref, acc_ref) = refs[pages:]
    j = pl.program_id(1)
    rows = q_ref.shape[0]
    kw = KVH_B * HEAD_DIM

    @pl.when(j == 0)
    def _():
        m_ref[...] = jnp.full(m_ref.shape, MASK_VALUE, F32)
        l_ref[...] = jnp.zeros(l_ref.shape, F32)
        acc_ref[...] = jnp.zeros(acc_ref.shape, F32)

    def attend(k_blk, v_blk, bias):
        n = k_blk.shape[0]
        s = lax.dot_general(q_ref[...], k_blk, NT_DIMS, preferred_element_type=F32)
        s = (s.reshape(rows // 8, 8, n) + bias[None]).reshape(rows, n)
        m, l, acc = _softmax_step(s, m_ref[...], l_ref[...], acc_ref[...], v_blk)
        m_ref[...] = m
        l_ref[...] = l
        acc_ref[...] = acc

    for p in range(pages):
        pg = page_refs[p][...]
        k_buf[p * PAGE_SIZE:(p + 1) * PAGE_SIZE, :] = pg[:, :kw].astype(BF16)
        v_buf[p * PAGE_SIZE:(p + 1) * PAGE_SIZE, :] = pg[:, kw:].astype(BF16)
    attend(k_buf[...], v_buf[...], bias_ref[...])

    @pl.when(j == pl.num_programs(1) - 1)
    def _():
        kv_new = kv_new_ref[...]
        attend(kv_new[:, :kw], kv_new[:, kw:], bias_new_ref[...])
        o_ref[...] = acc_ref[...] / l_ref[...]


def _dsa_sample(layer, page_table, cache_kv, q, bias, kv_new, bias_new, *, pages):
    n_seq, n_pages = page_table.shape
    n_steps = n_pages // pages
    n_main = pages * PAGE_SIZE
    rows = q.shape[1]
    n_tail = kv_new.shape[1]
    kw = KVH_B * HEAD_DIM
    per_seq = lambda shape: pl.BlockSpec((None,) + shape, lambda b, j, pt: (b,) + tuple(0 for _ in shape))
    in_specs = _page_specs(layer, pages, 2 * kw) + [
        per_seq((rows, kw)),
        pl.BlockSpec((None, None, 8, n_main), lambda b, j, pt: (b, j, 0, 0)),
        per_seq((n_tail, 2 * kw)), per_seq((8, n_tail))]
    return pl.pallas_call(
        functools.partial(_dsa_sample_kernel, pages=pages),
        out_shape=jax.ShapeDtypeStruct((n_seq, rows, kw), F32),
        grid_spec=pltpu.PrefetchScalarGridSpec(
            num_scalar_prefetch=1, grid=(n_seq, n_steps), in_specs=in_specs,
            out_specs=pl.BlockSpec((None, rows, kw), lambda b, j, pt: (b, 0, 0)),
            scratch_shapes=[pltpu.VMEM((n_main, kw), BF16), pltpu.VMEM((n_main, kw), BF16),
                            pltpu.VMEM((rows, 1), F32), pltpu.VMEM((rows, 1), F32),
                            pltpu.VMEM((rows, kw), F32)]),
        compiler_params=_params("parallel", "arbitrary"),
        name="dsa_sample")(page_table, *([cache_kv] * pages), q, bias, kv_new, bias_new)


def _router_kernel(x_ref, g_ref, w_ref, b_ref, h_ref, logit_ref):
    x = x_ref[...]
    ms = jnp.mean(x * x, axis=-1, keepdims=True)
    h = x * lax.rsqrt(ms + NORM_EPS) * g_ref[...]
    h_ref[...] = h.astype(h_ref.dtype)
    logit_ref[...] = jnp.dot(h, w_ref[...], preferred_element_type=F32, precision=lax.Precision.HIGHEST) + b_ref[...]


def _router(x, gain, w_router, b_router, *, tm=512):
    n, d = x.shape
    e = w_router.shape[1]
    return pl.pallas_call(
        _router_kernel,
        out_shape=(jax.ShapeDtypeStruct((n, d), BF16), jax.ShapeDtypeStruct((n, e), F32)),
        grid=(n // tm,),
        in_specs=[pl.BlockSpec((tm, d), lambda i: (i, 0)), pl.BlockSpec((1, d), lambda i: (0, 0)),
                  pl.BlockSpec((d, e), lambda i: (0, 0)), pl.BlockSpec((1, e), lambda i: (0, 0))],
        out_specs=(pl.BlockSpec((tm, d), lambda i: (i, 0)), pl.BlockSpec((tm, e), lambda i: (i, 0))),
        compiler_params=_params("parallel"),
        name="router")(x, gain.reshape(1, d), w_router, b_router.reshape(1, e))


def _expert_kernel(be_ref, bv_ref, x_ref, w1_ref, b1_ref, w2_ref, b2_ref, o_ref):
    i = pl.program_id(0)

    @pl.when(bv_ref[i] != 0)
    def _():
        d_ff = w2_ref.shape[0]
        u = jnp.dot(x_ref[...], w1_ref[...].astype(BF16), preferred_element_type=F32) + b1_ref[...]
        g = jnp.minimum(u[:, :d_ff], SWIGLU_LIMIT)
        up = jnp.clip(u[:, d_ff:], -SWIGLU_LIMIT, SWIGLU_LIMIT)
        a = g * jax.nn.sigmoid(SWIGLU_ALPHA * g) * (up + 1.0)
        o_ref[...] = jnp.dot(a.astype(BF16), w2_ref[...].astype(BF16), preferred_element_type=F32) + b2_ref[...]

    @pl.when(bv_ref[i] == 0)
    def _():
        o_ref[...] = jnp.zeros(o_ref.shape, o_ref.dtype)


def _experts(block_e, block_valid, xb, w1, b1, w2, b2, *, tm):
    n_rows, d = xb.shape
    e, _, two_ff = w1.shape
    d_ff = w2.shape[1]
    return pl.pallas_call(
        _expert_kernel,
        out_shape=jax.ShapeDtypeStruct((n_rows, d), F32),
        grid_spec=pltpu.PrefetchScalarGridSpec(
            num_scalar_prefetch=2, grid=(n_rows // tm,),
            in_specs=[pl.BlockSpec((tm, d), lambda i, be, bv: (i, 0)),
                      pl.BlockSpec((None, d, two_ff), lambda i, be, bv: (be[i], 0, 0)),
                      pl.BlockSpec((None, 1, two_ff), lambda i, be, bv: (be[i], 0, 0)),
                      pl.BlockSpec((None, d_ff, d), lambda i, be, bv: (be[i], 0, 0)),
                      pl.BlockSpec((None, 1, d), lambda i, be, bv: (be[i], 0, 0))],
            out_specs=pl.BlockSpec((tm, d), lambda i, be, bv: (i, 0))),
        compiler_params=_params("arbitrary"),
        name="experts")(block_e, block_valid, xb, w1, b1.reshape(e, 1, two_ff), w2, b2.reshape(e, 1, d))


def _moe(x, gain, w_router, b_router, w1, b1, w2, b2, *, tm_e=256):
    n, d = x.shape
    h, logits = _router(x, gain, w_router, b_router)
    top_val, top_idx = lax.top_k(logits, TOP_K)
    gate = jax.nn.softmax(top_val, axis=-1)
    flat_e = top_idx.reshape(-1)
    nk = flat_e.shape[0]
    onehot = (flat_e[:, None] == jnp.arange(N_EXPERTS, dtype=flat_e.dtype)[None, :]).astype(jnp.int32)
    rank = jnp.take_along_axis(jnp.cumsum(onehot, axis=0), flat_e[:, None], axis=1)[:, 0] - 1
    counts = jnp.sum(onehot, axis=0)
    padded = _cdiv(counts, tm_e) * tm_e
    pad_end = jnp.cumsum(padded)
    dest = (pad_end - padded)[flat_e] + rank
    n_blocks = _cdiv(nk, tm_e) + N_EXPERTS
    row_tok = jnp.zeros((n_blocks * tm_e,), jnp.int32).at[dest].set(jnp.arange(nk, dtype=jnp.int32) // TOP_K)
    block_start = jnp.arange(n_blocks, dtype=jnp.int32) * tm_e
    block_e = jnp.minimum(jnp.searchsorted(pad_end, block_start, side="right"), N_EXPERTS - 1).astype(jnp.int32)
    block_valid = (block_start < pad_end[-1]).astype(jnp.int32)
    yb = _experts(block_e, block_valid, h[row_tok], w1, b1, w2, b2, tm=tm_e)
    y = jnp.sum(yb[dest].reshape(n, TOP_K, d) * gate[:, :, None], axis=1)
    return x + y


def _rms(x, g, n=None):
    n = x.shape[-1] if n is None else n
    y = x * lax.rsqrt(jnp.sum(x * x, axis=-1, keepdims=True) / n + NORM_EPS)
    return y * g


def _rope_tables(pos, n_rot):
    half = n_rot // 2
    inv_freq = ROPE_THETA ** (-jnp.arange(half, dtype=F32) / half)
    ang = pos.astype(F32)[:, None] * inv_freq[None, :]
    return jnp.cos(ang), jnp.sin(ang)


def _rope(x, cos, sin, n_rot):
    d = x.shape[-1]
    half = n_rot // 2
    cos = cos[:, None, :]
    sin = sin[:, None, :]
    x1 = x[..., d - n_rot:d - half]
    x2 = x[..., d - half:]
    return jnp.concatenate([x[..., :d - n_rot], x1 * cos - x2 * sin, x2 * cos + x1 * sin], axis=-1)


def _regroup_w_in(w_in):
    cuts = np.cumsum([0, Q_LORA, KV_LORA, ROPE_A, H_B * HEAD_DIM, KVH_B * HEAD_DIM, KVH_B * HEAD_DIM,
                      H_I * D_I, D_I, H_I, 2 * w_in.shape[0]])
    w_cq, w_ckv, w_kpe, w_qb, w_kb, w_vb, w_iq, w_ik, w_iw, w_g = [w_in[:, a:b] for a, b in zip(cuts[:-1], cuts[1:])]
    misc_pad = jnp.zeros((w_in.shape[0], COL_GATE - COL_MISC - ROPE_A - D_I - H_I), w_in.dtype)
    return jnp.concatenate([w_cq, w_ckv, w_qb, w_kb, w_vb, w_iq, w_kpe, w_ik, w_iw, misc_pad, w_g], axis=1).astype(BF16)


def _layer(l, xp, xs, page_table, caches, wts, *, t_real):
    (cache_mla, cache_dsa_kv, cache_dsa_idx) = caches
    n_b, t_pad, d = xp.shape
    n_seq, n_ds, _ = xs.shape
    past = page_table.shape[1] * PAGE_SIZE
    n_p = n_b * t_pad
    n_s = n_seq * n_ds
    n_sel_p = min(MAX_SEL, (t_real - N_META) // 4)
    n_sel_s = min(MAX_SEL, (past + n_ds) // 4)

    x_all = jnp.concatenate([xp.reshape(n_p, d), xs.reshape(n_s, d)], axis=0)
    n_all = _round_up(n_p + n_s, 512)
    x_all = jnp.pad(x_all, ((0, n_all - n_p - n_s), (0, 0)))
    pos = jnp.concatenate([jnp.tile(jnp.arange(t_pad), n_b), jnp.tile(past + jnp.arange(n_ds), n_seq),
                           jnp.zeros((n_all - n_p - n_s,), jnp.int32)])
    z = _matmul(x_all, _regroup_w_in(wts["w_in"][l]), gain=wts["g_attn_norm"][l], name="proj_in")

    cos_a, sin_a = _rope_tables(pos, ROPE_A)
    cos_b, sin_b = _rope_tables(pos, HEAD_DIM)
    cos_i, sin_i = _rope_tables(pos, D_I_ROPE)

    w_uq = jnp.pad(wts["w_uq"][l], ((0, 0), (0, 0), (0, LANE - DQK_A))).reshape(Q_LORA, H_A * LANE).astype(BF16)
    q_a = _matmul(z, w_uq, gain=wts["g_cq"][l], col_block=COL_CQ // Q_LORA, k=Q_LORA, name="q_up")
    q_a = q_a.reshape(n_all, H_A, LANE)[..., :DQK_A]
    q_a = _rope(_rms(q_a, wts["g_qa"][l]), cos_a, sin_a, ROPE_A) * (DQK_A ** -0.5)

    c_kv = _rms(z[:, COL_CKV:COL_CKV + KV_LORA], wts["g_ckv"][l])
    k_pe = z[:, COL_MISC:COL_MISC + ROPE_A]
    q_b = _rope(_rms(z[:, COL_QB:COL_QB + H_B * HEAD_DIM].reshape(n_all, H_B, HEAD_DIM), wts["g_qb"][l]),
                cos_b, sin_b, HEAD_DIM) * (HEAD_DIM ** -0.5)
    k_b = _rope(_rms(z[:, COL_KB:COL_KB + KVH_B * HEAD_DIM].reshape(n_all, KVH_B, HEAD_DIM), wts["g_kb"][l]),
                cos_b, sin_b, HEAD_DIM)
    v_b = z[:, COL_VB:COL_VB + KVH_B * HEAD_DIM].reshape(n_all, KVH_B, HEAD_DIM)
    i_q = _rope(z[:, COL_IQ:COL_IQ + H_I * D_I].reshape(n_all, H_I, D_I), cos_i, sin_i, D_I_ROPE)
    i_k = _rope(_rms(z[:, COL_MISC + ROPE_A:COL_MISC + ROPE_A + D_I], wts["g_ik"][l])[:, None, :],
                cos_i, sin_i, D_I_ROPE)[:, 0, :]
    i_w = z[:, COL_MISC + ROPE_A + D_I:COL_MISC + ROPE_A + D_I + H_I] * ((H_I ** -0.5) * (D_I ** -0.5))
    gates = jax.nn.sigmoid(z[:, COL_GATE:COL_GATE + 2 * d])

    mla_rows = jnp.concatenate([c_kv, k_pe], axis=-1)
    kv_rows = jnp.stack([k_b, v_b], axis=1)
    half_sel = (jnp.arange(H_B)[:, None] // GROUP_B == jnp.arange(KVH_B)[None, :]).astype(F32)
    q_b_pad = (q_b[:, :, None, :] * half_sel[None, :, :, None]).reshape(n_all, H_B, KVH_B * HEAD_DIM)

    tq = 128
    nq = t_pad // tq
    n_kc = max(dd for dd in range(1, 5) if nq % dd == 0)
    tk = n_kc * tq
    w_kv = jnp.concatenate([wts["w_uk"][l].reshape(KV_LORA, H_A * NOPE_A), wts["w_uv"][l].reshape(KV_LORA, H_A * V_A)],
                           axis=1).astype(BF16)
    kv_a = _matmul(z, w_kv, gain=wts["g_ckv"][l], col_block=COL_CKV // KV_LORA, k=KV_LORA, name="kv_up")
    k_nope = kv_a[:n_p, :H_A * NOPE_A].reshape(n_p, H_A, NOPE_A)
    v_a = kv_a[:n_p, H_A * NOPE_A:]
    k_a = jnp.concatenate([k_nope, jnp.broadcast_to(k_pe[:n_p, None, :], (n_p, H_A, ROPE_A))], axis=-1)
    k_a = _rope(_rms(k_a, wts["g_ka"][l]), cos_a[:n_p], sin_a[:n_p], ROPE_A)
    pad_h = lambda a: jnp.pad(a, ((0, 0), (0, 0), (0, LANE - DQK_A))).astype(BF16)
    o_a_p = _mla_prompt(pad_h(q_a[:n_p]).reshape(n_b, t_pad, H_A * LANE), pad_h(k_a).reshape(n_b, t_pad, H_A * LANE),
                        v_a.astype(BF16).reshape(n_b, t_pad, H_A * V_A), tq=tq, tk=tk)

    blocked = lambda a: a.reshape((n_b, nq, tq) + a.shape[1:]).swapaxes(2, 3)
    o_b_p = _dsa_prompt(blocked(i_q[:n_p].astype(BF16)), i_w[:n_p].reshape(n_b, t_pad, H_I),
                        blocked(q_b_pad[:n_p].astype(BF16)), i_k[:n_p].astype(BF16).reshape(n_b, t_pad, D_I),
                        k_b[:n_p].astype(BF16).reshape(n_b, t_pad, KVH_B * HEAD_DIM),
                        v_b[:n_p].astype(BF16).reshape(n_b, t_pad, KVH_B * HEAD_DIM), tq=tq, tk=tk, n_sel=n_sel_p)
    o_b_p = o_b_p.reshape(n_b, nq, KVH_B, GROUP_B, tq, KVH_B, HEAD_DIM)
    o_b_p = jnp.stack([o_b_p[:, :, g, :, :, g, :] for g in range(KVH_B)], axis=2)
    o_b_p = o_b_p.reshape(n_b, nq, H_B, tq, HEAD_DIM).swapaxes(2, 3).reshape(n_p, H_B * HEAD_DIM)

    sl = slice(n_p, n_p + n_s)
    pages = max(dd for dd in range(1, MAX_PAGES_PER_STEP + 1) if page_table.shape[1] % dd == 0)
    assert 8 % n_ds == 0
    n_main = pages * PAGE_SIZE
    n_tail = LANE
    pad_tail = lambda a: jnp.pad(a.reshape((n_seq, n_ds) + a.shape[1:]),
                                 ((0, 0), (0, n_tail - n_ds)) + ((0, 0),) * (a.ndim - 1))
    g_ka = wts["g_ka"][l]
    pos_k = jnp.arange(past + n_tail)
    cos_k, sin_k = _rope_tables(pos_k, ROPE_A)
    g1, g2 = g_ka[NOPE_A:NOPE_A + ROPE_A // 2], g_ka[NOPE_A + ROPE_A // 2:]
    cg = jnp.concatenate([cos_k * g1, cos_k * g2], axis=1)
    sg = jnp.concatenate([sin_k * g1, -sin_k * g2], axis=1)
    qa_s = q_a[sl]
    rows_a = n_ds * H_A
    eye_h = jnp.eye(H_A, dtype=F32)
    qn_s = qa_s[..., :NOPE_A] * g_ka[:NOPE_A]
    qbd = (qn_s[:, :, None, :] * eye_h[None, :, :, None]).reshape(n_seq, rows_a, H_A * NOPE_A)
    q_r = qa_s[:, :, NOPE_A:].reshape(n_seq, rows_a, ROPE_A)
    q_r_swap = jnp.concatenate([q_r[..., ROPE_A // 2:], q_r[..., :ROPE_A // 2]], axis=-1)
    wt = wts["w_uk"][l].reshape(KV_LORA, H_A * NOPE_A).T
    o_lat = _mla_sample(l, page_table, cache_mla, wt.astype(BF16), qbd.astype(BF16), q_r.astype(BF16),
                        q_r_swap.astype(BF16), cg[:past], sg[:past], pad_tail(c_kv[sl]), pad_tail(k_pe[sl]),
                        cg[past:], sg[past:], pages=pages, n_new=n_ds)
    o_a_s = jnp.einsum("bqhr,rhv->bqhv", o_lat.reshape(n_seq, n_ds, H_A, KV_LORA), wts["w_uv"][l])
    o_a_s = o_a_s.reshape(n_s, H_A * V_A)

    slot = jnp.arange(8) % n_ds
    iq_s = i_q[sl].reshape(n_seq, n_ds, H_I, D_I)[:, slot].swapaxes(1, 2).reshape(n_seq, H_I * 8, D_I)
    iw_s = i_w[sl].reshape(n_seq, n_ds, H_I)[:, slot].swapaxes(1, 2).reshape(n_seq, H_I * 8, 1)
    bias, bias_new = _dsa_select(l, page_table, cache_dsa_idx, iq_s.astype(BF16), iw_s,
                                 pad_tail(i_k[sl]).astype(BF16), pages=pages, n_new=n_ds, n_sel=n_sel_s)
    qb_s = q_b_pad[sl].reshape(n_seq, n_ds, H_B, KVH_B * HEAD_DIM).swapaxes(1, 2).reshape(n_seq, H_B * n_ds, -1)
    kv_new = pad_tail(kv_rows[sl].reshape(n_s, 2 * KVH_B * HEAD_DIM)).astype(BF16)
    o_b_s = _dsa_sample(l, page_table, cache_dsa_kv.reshape(cache_dsa_kv.shape[:3] + (-1,)), qb_s.astype(BF16),
                        bias, kv_new, bias_new, pages=pages)
    o_b_s = o_b_s.reshape(n_seq, KVH_B, GROUP_B, n_ds, KVH_B, HEAD_DIM)
    o_b_s = jnp.stack([o_b_s[:, g, :, :, g, :] for g in range(KVH_B)], axis=1)
    o_b_s = o_b_s.reshape(n_seq, H_B, n_ds, HEAD_DIM).swapaxes(1, 2).reshape(n_s, H_B * HEAD_DIM)

    zeros_tail = jnp.zeros((n_all - n_p - n_s, H_A * V_A), BF16)
    o_a = jnp.concatenate([o_a_p.reshape(n_p, H_A * V_A), o_a_s.astype(BF16), zeros_tail], axis=0)
    o_b = jnp.concatenate([o_b_p, o_b_s.astype(BF16), zeros_tail], axis=0)
    y_a = _matmul(o_a, wts["w_pa"][l].astype(BF16), name="proj_a")
    y_b = _matmul(o_b, wts["w_pb"][l].astype(BF16), name="proj_b")
    merged = gates[:, :d] * y_a + gates[:, d:] * y_b
    x_all = x_all + _matmul(merged, wts["w_o"][l].astype(BF16), name="proj_o")
    x_all = _moe(x_all, wts["g_ffn_norm"][l], wts["w_router"][l], wts["b_router"][l], wts["w_mlp1"][l],
                 wts["b_mlp1"][l], wts["w_mlp2"][l], wts["b_mlp2"][l])

    new_p = (mla_rows[:n_p].reshape(n_b, t_pad, -1)[:, :t_real],
             kv_rows[:n_p].reshape(n_b, t_pad, 2, KVH_B, HEAD_DIM)[:, :t_real],
             i_k[:n_p].reshape(n_b, t_pad, D_I)[:, :t_real])
    new_s = (mla_rows[sl].reshape(n_seq, n_ds, -1), kv_rows[sl].reshape(n_seq, n_ds, 2, KVH_B, HEAD_DIM),
             i_k[sl].reshape(n_seq, n_ds, D_I))
    return x_all[:n_p].reshape(n_b, t_pad, d), x_all[sl].reshape(n_seq, n_ds, d), new_p, new_s


def kernel(x_prompt, x_sample, cache_mla, cache_dsa_kv, cache_dsa_idx, page_table, meta_tokens, g_attn_norm, w_in,
           g_cq, w_uq, g_qa, g_ckv, w_uk, w_uv, g_ka, g_qb, g_kb, g_ik, w_pa, w_pb, w_o, g_ffn_norm, w_router,
           b_router, w_mlp1, b_mlp1, w_mlp2, b_mlp2):
    n_b, n_s, d = x_prompt.shape
    t_real = n_s + N_META
    t_pad = _round_up(t_real, LANE)
    wts = dict(g_attn_norm=g_attn_norm, w_in=w_in, g_cq=g_cq, w_uq=w_uq, g_qa=g_qa, g_ckv=g_ckv, w_uk=w_uk, w_uv=w_uv,
               g_ka=g_ka, g_qb=g_qb, g_kb=g_kb, g_ik=g_ik, w_pa=w_pa, w_pb=w_pb, w_o=w_o, g_ffn_norm=g_ffn_norm,
               w_router=w_router, b_router=b_router, w_mlp1=w_mlp1, b_mlp1=b_mlp1, w_mlp2=w_mlp2, b_mlp2=b_mlp2)
    meta = jnp.broadcast_to(meta_tokens[None].astype(x_prompt.dtype), (n_b, N_META, d))
    xp = jnp.concatenate([meta, x_prompt, jnp.zeros((n_b, t_pad - t_real, d), x_prompt.dtype)], axis=1)
    xs = x_sample
    outs_p, outs_s = [], []
    for l in range(w_in.shape[0]):
        xp, xs, new_p, new_s = _layer(l, xp, xs, page_table, (cache_mla, cache_dsa_kv, cache_dsa_idx), wts,
                                      t_real=t_real)
        outs_p.append(new_p)
        outs_s.append(new_s)
    stack = lambda outs, i: jnp.stack([o[i] for o in outs], axis=0)
    return (xp[:, N_META:t_real], xs, stack(outs_p, 0), stack(outs_p, 1), stack(outs_p, 2),
            stack(outs_s, 0), stack(outs_s, 1), stack(outs_s, 2))
```

```python
import functools

import jax
import jax.numpy as jnp
import numpy as np
from jax import lax
from jax.experimental import pallas as pl
from jax.experimental.pallas import tpu as pltpu

N_META = 16
HEAD_DIM = 64
H_A = 8
NOPE_A = 64
ROPE_A = 32
DQK_A = NOPE_A + ROPE_A
V_A = 64
Q_LORA = 512
KV_LORA = 256
H_B = 8
KVH_B = 2
GROUP_B = H_B // KVH_B
H_I = 8
D_I = 32
D_I_ROPE = 16
MAX_SEL = 256
N_EXPERTS = 32
TOP_K = 4
SWIGLU_LIMIT = 7.0
SWIGLU_ALPHA = 1.702
ROPE_THETA = 10000.0
NORM_EPS = 1e-6
PAGE_SIZE = 128

LANE = 128
MASK_VALUE = -1e30
INT_MIN = -(2 ** 31)
VMEM_LIMIT = 56 * 1024 * 1024
MAX_PAGES_PER_STEP = 16

COL_CQ = 0
COL_CKV = 512
COL_QB = 768
COL_KB = 1280
COL_VB = 1408
COL_IQ = 1536
COL_MISC = 1792
COL_GATE = 2048
N_PROJ = 4096

F32 = jnp.float32
BF16 = jnp.bfloat16
NT_DIMS = (((1,), (1,)), ((), ()))


def _cdiv(a, b):
    return (a + b - 1) // b


def _round_up(a, b):
    return _cdiv(a, b) * b


def _params(*sem):
    return pltpu.CompilerParams(dimension_semantics=sem, vmem_limit_bytes=VMEM_LIMIT)


def _mm_kernel(x_ref, w_ref, o_ref):
    o_ref[...] = jnp.dot(x_ref[...].astype(BF16), w_ref[...], preferred_element_type=F32).astype(o_ref.dtype)


def _mm_norm_kernel(x_ref, g_ref, w_ref, o_ref, xn_ref):
    @pl.when(pl.program_id(1) == 0)
    def _():
        x = x_ref[...].astype(F32)
        ms = jnp.mean(x * x, axis=-1, keepdims=True)
        xn_ref[...] = (x * lax.rsqrt(ms + NORM_EPS) * g_ref[...]).astype(BF16)

    o_ref[...] = jnp.dot(xn_ref[...], w_ref[...], preferred_element_type=F32).astype(o_ref.dtype)


def _matmul(x, w, *, gain=None, col_block=0, k=None, tm=512, tn=512, out_dtype=F32, name="mm"):
    m = x.shape[0]
    k = x.shape[1] if k is None else k
    n = w.shape[1]
    tn = min(tn, n)
    assert m % tm == 0 and n % tn == 0 and w.shape[0] == k
    grid = (m // tm, n // tn)
    x_spec = pl.BlockSpec((tm, k), lambda i, j: (i, col_block))
    w_spec = pl.BlockSpec((k, tn), lambda i, j: (0, j))
    o_spec = pl.BlockSpec((tm, tn), lambda i, j: (i, j))
    out_shape = jax.ShapeDtypeStruct((m, n), out_dtype)
    if gain is None:
        return pl.pallas_call(_mm_kernel, out_shape=out_shape, grid=grid, in_specs=[x_spec, w_spec],
                              out_specs=o_spec, compiler_params=_params("parallel", "arbitrary"), name=name)(x, w)
    g_spec = pl.BlockSpec((1, k), lambda i, j: (0, 0))
    return pl.pallas_call(_mm_norm_kernel, out_shape=out_shape, grid=grid, in_specs=[x_spec, g_spec, w_spec],
                          out_specs=o_spec, scratch_shapes=[pltpu.VMEM((tm, k), BF16)],
                          compiler_params=_params("parallel", "arbitrary"), name=name)(
                              x, gain.reshape(1, k).astype(F32), w)


def _softmax_step(s, m_prev, l_prev, acc_prev, v_blk):
    m_new = jnp.maximum(m_prev, jnp.max(s, axis=1, keepdims=True))
    alpha = jnp.exp(m_prev - m_new)
    p = jnp.exp(s - m_new)
    l_new = alpha * l_prev + jnp.sum(p, axis=1, keepdims=True)
    acc_new = alpha * acc_prev + jnp.dot(p.astype(BF16), v_blk, preferred_element_type=F32)
    return m_new, l_new, acc_new


def _sortable_key(x):
    b = lax.bitcast_convert_type(x + 0.0, jnp.int32)
    return jnp.where(b < 0, b ^ jnp.int32(0x7FFFFFFF), b)


KEY_NEG_INF = INT_MIN + 0x7FFFFF


def _has_excess_ties(n_ge, thr, n_sel):
    return jnp.max(jnp.where(n_ge > n_sel, jnp.where(thr > KEY_NEG_INF, 1, 0), 0)) > 0


def _kth_largest_key(count_ge, n_sel, rows):
    def step(s, t):
        bit = lax.shift_left(jnp.int32(1), jnp.int32(31) - s)
        cand = t + bit
        return jnp.where(count_ge(cand) >= n_sel, cand, t)

    return lax.fori_loop(0, 32, step, jnp.full((rows, 1), INT_MIN, jnp.int32))


MLA_HEADS_PER_STEP = 4


def _mla_prompt_kernel(q_ref, k_ref, v_ref, o_ref, *, tq):
    i = pl.program_id(2)
    nh = MLA_HEADS_PER_STEP
    q = q_ref[...]
    causal = lax.broadcasted_iota(jnp.int32, (tq, tq), 1) <= lax.broadcasted_iota(jnp.int32, (tq, tq), 0)

    def chunk(c, carry, diagonal):
        m, l, acc = carry
        start = pl.multiple_of(c * tq, tq)
        k_blk = k_ref[pl.ds(start, tq), :]
        v_blk = v_ref[pl.ds(start, tq), :]
        s = jnp.concatenate([lax.dot_general(q[:, h * LANE:(h + 1) * LANE], k_blk[:, h * LANE:(h + 1) * LANE],
                                             NT_DIMS, preferred_element_type=F32) for h in range(nh)], axis=0)
        if diagonal:
            s = jnp.where(causal[None], s.reshape(nh, tq, tq), MASK_VALUE).reshape(nh * tq, tq)
        m_new = jnp.maximum(m, jnp.max(s, axis=1, keepdims=True))
        alpha = jnp.exp(m - m_new)
        p = jnp.exp(s - m_new)
        l_new = alpha * l + jnp.sum(p, axis=1, keepdims=True)
        p = p.astype(BF16)
        pv = jnp.concatenate([jnp.dot(p[h * tq:(h + 1) * tq], v_blk[:, (h // 2) * LANE:(h // 2 + 1) * LANE],
                                      preferred_element_type=F32) for h in range(nh)], axis=0)
        return m_new, l_new, alpha * acc + pv

    init = (jnp.full((nh * tq, 1), MASK_VALUE, F32), jnp.zeros((nh * tq, 1), F32), jnp.zeros((nh * tq, LANE), F32))
    carry = lax.fori_loop(0, i, functools.partial(chunk, diagonal=False), init)
    _, l, acc = chunk(i, carry, True)
    o = acc / l
    lane = lax.broadcasted_iota(jnp.int32, (tq, LANE), 1)
    pairs = [jnp.where(lane < V_A, o[(2 * g) * tq:(2 * g + 1) * tq], o[(2 * g + 1) * tq:(2 * g + 2) * tq])
             for g in range(nh // 2)]
    o_ref[...] = jnp.concatenate(pairs, axis=1).astype(o_ref.dtype)


def _mla_prompt(q, k, v, *, tq):
    b, t, _ = q.shape
    nh = MLA_HEADS_PER_STEP
    grid = (b, H_A // nh, t // tq)
    return pl.pallas_call(
        functools.partial(_mla_prompt_kernel, tq=tq),
        out_shape=jax.ShapeDtypeStruct((b, t, H_A * V_A), BF16),
        grid=grid,
        in_specs=[pl.BlockSpec((None, tq, nh * LANE), lambda bi, g, i: (bi, i, g)),
                  pl.BlockSpec((None, t, nh * LANE), lambda bi, g, i: (bi, 0, g)),
                  pl.BlockSpec((None, t, nh * V_A), lambda bi, g, i: (bi, 0, g))],
        out_specs=pl.BlockSpec((None, tq, nh * V_A), lambda bi, g, i: (bi, i, g)),
        compiler_params=_params("parallel", "parallel", "arbitrary"),
        name="mla_prompt")(q, k, v)


def _select_bias(key, thr, need, carry, valid, tri_ref):
    w = min(tri_ref.shape[0], key.shape[1])
    tri = tri_ref[0:w, 0:w]
    parts = []
    for t in range(key.shape[1] // w):
        k_t = key[:, t * w:(t + 1) * w]
        eq = k_t == thr
        rank = jnp.dot(jnp.where(eq, 1.0, 0.0).astype(BF16), tri, preferred_element_type=F32) + carry
        carry = jnp.max(rank, axis=1, keepdims=True)
        tie = jnp.where(rank <= need, 0.0, MASK_VALUE)
        parts.append(jnp.where(k_t > thr, 0.0, jnp.where(eq, tie, MASK_VALUE)))
    bias = parts[0] if len(parts) == 1 else jnp.concatenate(parts, axis=1)
    if valid is not None:
        bias = jnp.where(valid, bias, MASK_VALUE)
    return bias, carry


def _dsa_prompt_kernel(iq_ref, w_ref, q_ref, ik_ref, k_ref, v_ref, tri_ref, o_ref, key_ref, bias_ref, wb_ref,
                       *, tq, tk, n_sel):
    i = pl.program_id(1)
    n_c = ((i + 1) * tq + tk - 1) // tk
    q_pos = i * tq + lax.broadcasted_iota(jnp.int32, (tq, tk), 0)
    lane_pos = lax.broadcasted_iota(jnp.int32, (tq, tk), 1)

    w = w_ref[...]
    for h in range(H_I):
        wb_ref[h] = jnp.broadcast_to(w[:, h:h + 1], (tq, LANE))
    iq = iq_ref[...].reshape(H_I * tq, D_I)

    def score_chunk(c, _):
        start = pl.multiple_of(c * tk, tk)
        logit = lax.dot_general(iq, ik_ref[pl.ds(start, tk), :], NT_DIMS, preferred_element_type=F32)
        cols = []
        for t in range(tk // LANE):
            acc = jnp.zeros((tq, LANE), F32)
            for h in range(H_I):
                acc = acc + jnp.maximum(logit[h * tq:(h + 1) * tq, t * LANE:(t + 1) * LANE], 0.0) * wb_ref[h]
            cols.append(acc)
        score = jnp.concatenate(cols, axis=1)
        score = jnp.where(start + lane_pos <= q_pos, score, -jnp.inf)
        key_ref[c] = _sortable_key(score)
        return 0

    lax.fori_loop(0, n_c, score_chunk, 0)

    def count_ge(cand):
        cand_b = jnp.broadcast_to(cand, (tq, LANE))

        def body(c, cnt):
            key = key_ref[c]
            for t in range(tk // LANE):
                cnt = cnt + jnp.where(key[:, t * LANE:(t + 1) * LANE] >= cand_b, 1, 0)
            return cnt

        cnt = lax.fori_loop(0, n_c, body, jnp.zeros((tq, LANE), jnp.int32))
        return jnp.sum(cnt, axis=1, keepdims=True)

    thr = _kth_largest_key(count_ge, n_sel, tq)

    def bias_by_threshold():
        thr_eff = jnp.maximum(thr, KEY_NEG_INF + 1)

        def body(c, _):
            bias_ref[c] = jnp.where(key_ref[c] >= thr_eff, 0.0, MASK_VALUE)
            return 0

        lax.fori_loop(0, n_c, body, 0)

    def bias_with_ties():
        need = (n_sel - count_ge(thr + 1)).astype(F32)

        def body(c, carry):
            start = pl.multiple_of(c * tk, tk)
            bias, carry = _select_bias(key_ref[c], thr, need, carry, start + lane_pos <= q_pos, tri_ref)
            bias_ref[c] = bias
            return carry

        lax.fori_loop(0, n_c, body, jnp.zeros((tq, 1), F32))

    lax.cond(_has_excess_ties(count_ge(thr), thr, n_sel), bias_with_ties, bias_by_threshold)

    rows = H_B * tq
    qa = q_ref[...].reshape(rows, LANE)

    def attend(c, carry):
        start = pl.multiple_of(c * tk, tk)
        s = lax.dot_general(qa, k_ref[pl.ds(start, tk), :], NT_DIMS, preferred_element_type=F32)
        s = (s.reshape(H_B, tq, tk) + bias_ref[c][None]).reshape(rows, tk)
        return _softmax_step(s, *carry, v_ref[pl.ds(start, tk), :])

    m, l, acc = lax.fori_loop(0, n_c, attend, (jnp.full((rows, 1), MASK_VALUE, F32),
                                               jnp.zeros((rows, 1), F32), jnp.zeros((rows, LANE), F32)))
    o_ref[...] = (acc / l).reshape(H_B, tq, LANE).astype(o_ref.dtype)


def _dsa_prompt(iq, iw, q, ik, k, v, *, tq, tk, n_sel):
    b, nq = iq.shape[:2]
    t = ik.shape[1]
    n_chunks = t // tk
    tri_w = 2 * LANE if tk % (2 * LANE) == 0 else LANE
    tri = jnp.asarray(np.arange(tri_w)[:, None] <= np.arange(tri_w)[None, :], BF16)
    return pl.pallas_call(
        functools.partial(_dsa_prompt_kernel, tq=tq, tk=tk, n_sel=n_sel),
        out_shape=jax.ShapeDtypeStruct((b, nq, H_B, tq, LANE), BF16),
        grid=(b, nq),
        in_specs=[pl.BlockSpec((None, None, H_I, tq, D_I), lambda bi, i: (bi, i, 0, 0, 0)),
                  pl.BlockSpec((None, tq, H_I), lambda bi, i: (bi, i, 0)),
                  pl.BlockSpec((None, None, H_B, tq, LANE), lambda bi, i: (bi, i, 0, 0, 0)),
                  pl.BlockSpec((None, t, D_I), lambda bi, i: (bi, 0, 0)),
                  pl.BlockSpec((None, t, LANE), lambda bi, i: (bi, 0, 0)),
                  pl.BlockSpec((None, t, LANE), lambda bi, i: (bi, 0, 0)),
                  pl.BlockSpec((tri_w, tri_w), lambda bi, i: (0, 0))],
        out_specs=pl.BlockSpec((None, None, H_B, tq, LANE), lambda bi, i: (bi, i, 0, 0, 0)),
        scratch_shapes=[pltpu.VMEM((n_chunks, tq, tk), jnp.int32),
                        pltpu.VMEM((n_chunks, tq, tk), F32),
                        pltpu.VMEM((H_I, tq, LANE), F32)],
        compiler_params=_params("parallel", "arbitrary"),
        name="dsa_prompt")(iq, iw, q, ik, k, v, tri)


def _page_specs(layer, pages, width):
    def spec(p):
        return pl.BlockSpec((None, None, width, PAGE_SIZE), lambda b, j, pt: (layer, pt[b, j * pages + p], 0, 0))
    return [spec(p) for p in range(pages)]


def _seq_spec(shape):
    return pl.BlockSpec((None,) + shape, lambda b, j, pt: (b,) + tuple(0 for _ in shape))


def _const_spec(shape):
    return pl.BlockSpec(shape, lambda b, j, pt: tuple(0 for _ in shape))


def _softmax_step_t(s, m_ref, l_ref, acc_ref, v_t):
    m_prev = m_ref[...]
    m_new = jnp.maximum(m_prev, jnp.max(s, axis=1, keepdims=True))
    alpha = jnp.exp(m_prev - m_new)
    p = jnp.exp(s - m_new)
    l_ref[...] = alpha * l_ref[...] + jnp.sum(p, axis=1, keepdims=True)
    acc_ref[...] = alpha * acc_ref[...] + lax.dot_general(p.astype(BF16), v_t, NT_DIMS, preferred_element_type=F32)
    m_ref[...] = m_new


def _init_softmax(m_ref, l_ref, acc_ref):
    m_ref[...] = jnp.full(m_ref.shape, MASK_VALUE, F32)
    l_ref[...] = jnp.zeros(l_ref.shape, F32)
    acc_ref[...] = jnp.zeros(acc_ref.shape, F32)


def _mla_sample_kernel(pt_ref, *refs, pages, n_new):
    page_refs = refs[:pages]
    (wt_ref, qbd_ref, qr_ref, qs_ref, cg_ref, sg_ref, new_ref, cg_new_ref, sg_new_ref,
     o_ref, lhs_ref, c_buf, pe_buf, m_ref, l_ref, acc_ref) = refs[pages:]
    j = pl.program_id(1)
    rows = o_ref.shape[0]
    n_k = H_A * NOPE_A

    @pl.when(j == 0)
    def _():
        wt = wt_ref[...]
        lhs_ref[0:n_k, :] = wt
        lhs_ref[n_k:, :] = jnp.dot(qbd_ref[...], wt, preferred_element_type=F32).astype(BF16)
        _init_softmax(m_ref, l_ref, acc_ref)

    def attend(c_t, pe_t, cg, sg, mask):
        n = c_t.shape[1]
        big = jnp.dot(lhs_ref[...], c_t, preferred_element_type=F32)
        ssq = jnp.zeros((H_A, n), F32)
        row_id = lax.broadcasted_iota(jnp.int32, (H_A, n), 0)
        for h in range(H_A):
            kh = big[h * NOPE_A:(h + 1) * NOPE_A]
            ssq = jnp.where(row_id == h, jnp.sum(kh * kh, axis=0, keepdims=True), ssq)
        ssq = ssq + jnp.sum(pe_t * pe_t, axis=0, keepdims=True)
        inv_rms = lax.rsqrt(ssq * (1.0 / DQK_A) + NORM_EPS)
        s_rope = (jnp.dot(qr_ref[...], (pe_t * cg).astype(BF16), preferred_element_type=F32)
                  + jnp.dot(qs_ref[...], (pe_t * sg).astype(BF16), preferred_element_type=F32))
        s = ((big[n_k:] + s_rope).reshape(rows // H_A, H_A, n) * inv_rms[None]).reshape(rows, n)
        if mask is not None:
            s = jnp.where(mask, s, MASK_VALUE)
        _softmax_step_t(s, m_ref, l_ref, acc_ref, c_t)

    for p in range(pages):
        pg = page_refs[p][...]
        c_buf[:, p * PAGE_SIZE:(p + 1) * PAGE_SIZE] = pg[:KV_LORA].astype(BF16)
        pe_buf[:, p * PAGE_SIZE:(p + 1) * PAGE_SIZE] = pg[KV_LORA:]
    attend(c_buf[...], pe_buf[...], cg_ref[...], sg_ref[...], None)

    @pl.when(j == pl.num_programs(1) - 1)
    def _():
        new = new_ref[...]
        n = new.shape[1]
        key_i = lax.broadcasted_iota(jnp.int32, (rows, n), 1)
        q_i = lax.broadcasted_iota(jnp.int32, (rows, n), 0) // H_A
        attend(new[:KV_LORA].astype(BF16), new[KV_LORA:], cg_new_ref[...], sg_new_ref[...],
               (key_i <= q_i) & (key_i < n_new))
        o_ref[...] = acc_ref[...] / l_ref[...]


def _mla_sample(layer, page_table, cache_t, wt, qbd, qr, qs, cg_t, sg_t, new_t, cg_new_t, sg_new_t, *, pages, n_new):
    n_seq, n_pages = page_table.shape
    rows = qbd.shape[1]
    n_main = pages * PAGE_SIZE
    n_tail = new_t.shape[2]
    width = KV_LORA + ROPE_A
    in_specs = _page_specs(layer, pages, width) + [
        _const_spec((H_A * NOPE_A, KV_LORA)),
        _seq_spec((rows, H_A * NOPE_A)), _seq_spec((rows, ROPE_A)), _seq_spec((rows, ROPE_A)),
        pl.BlockSpec((ROPE_A, n_main), lambda b, j, pt: (0, j)),
        pl.BlockSpec((ROPE_A, n_main), lambda b, j, pt: (0, j)),
        _seq_spec((width, n_tail)), _const_spec((ROPE_A, n_tail)), _const_spec((ROPE_A, n_tail))]
    return pl.pallas_call(
        functools.partial(_mla_sample_kernel, pages=pages, n_new=n_new),
        out_shape=jax.ShapeDtypeStruct((n_seq, rows, KV_LORA), F32),
        grid_spec=pltpu.PrefetchScalarGridSpec(
            num_scalar_prefetch=1, grid=(n_seq, n_pages // pages), in_specs=in_specs,
            out_specs=pl.BlockSpec((None, rows, KV_LORA), lambda b, j, pt: (b, 0, 0)),
            scratch_shapes=[pltpu.VMEM((H_A * NOPE_A + rows, KV_LORA), BF16),
                            pltpu.VMEM((KV_LORA, n_main), BF16),
                            pltpu.VMEM((ROPE_A, n_main), F32),
                            pltpu.VMEM((rows, 1), F32), pltpu.VMEM((rows, 1), F32),
                            pltpu.VMEM((rows, KV_LORA), F32)]),
        compiler_params=_params("parallel", "arbitrary"),
        name="mla_sample")(page_table, *([cache_t] * pages), wt, qbd, qr, qs, cg_t, sg_t, new_t, cg_new_t, sg_new_t)


def _dsa_select_kernel(pt_ref, *refs, pages, n_new, n_sel):
    page_refs = refs[:pages]
    (iq_ref, w_ref, ik_new_ref, tri_ref, bias_ref, bias_new_ref, ik_buf, key_ref, key_new_ref) = refs[pages:]
    j = pl.program_id(1)
    n_steps = pl.num_programs(1)
    n_main = ik_buf.shape[1]
    n_tail = ik_new_ref.shape[1]
    nq = 8

    def scores(ik_t):
        logit = jnp.dot(iq_ref[...], ik_t, preferred_element_type=F32)
        weighted = jnp.maximum(logit, 0.0) * w_ref[...]
        return jnp.sum(weighted.reshape(H_I, nq, ik_t.shape[1]), axis=0)

    for p in range(pages):
        ik_buf[:, p * PAGE_SIZE:(p + 1) * PAGE_SIZE] = page_refs[p][...].astype(BF16)
    key_ref[j] = _sortable_key(scores(ik_buf[...]))

    @pl.when(j == n_steps - 1)
    def _():
        key_i = lax.broadcasted_iota(jnp.int32, (nq, n_tail), 1)
        q_i = lax.broadcasted_iota(jnp.int32, (nq, n_tail), 0) % n_new
        valid_new = (key_i <= q_i) & (key_i < n_new)
        key_new_ref[...] = _sortable_key(jnp.where(valid_new, scores(ik_new_ref[...]), -jnp.inf))

        def count_ge(cand):
            def body(c, cnt):
                key = key_ref[c]
                for t in range(n_main // LANE):
                    cnt = cnt + jnp.where(key[:, t * LANE:(t + 1) * LANE] >= cand, 1, 0)
                return cnt

            cnt = lax.fori_loop(0, n_steps, body, jnp.zeros((nq, LANE), jnp.int32))
            key_new = key_new_ref[...]
            for t in range(n_tail // LANE):
                cnt = cnt + jnp.where(key_new[:, t * LANE:(t + 1) * LANE] >= cand, 1, 0)
            return jnp.sum(cnt, axis=1, keepdims=True)

        thr = _kth_largest_key(count_ge, n_sel, nq)

        def bias_by_threshold():
            thr_eff = jnp.maximum(thr, KEY_NEG_INF + 1)

            def body(c, _):
                bias_ref[c] = jnp.where(key_ref[c] >= thr_eff, 0.0, MASK_VALUE)
                return 0

            lax.fori_loop(0, n_steps, body, 0)
            bias_new_ref[...] = jnp.where(key_new_ref[...] >= thr_eff, 0.0, MASK_VALUE)

        def bias_with_ties():
            need = (n_sel - count_ge(thr + 1)).astype(F32)

            def body(c, carry):
                bias, carry = _select_bias(key_ref[c], thr, need, carry, None, tri_ref)
                bias_ref[c] = bias
                return carry

            carry = lax.fori_loop(0, n_steps, body, jnp.zeros((nq, 1), F32))
            bias_new, _ = _select_bias(key_new_ref[...], thr, need, carry, valid_new, tri_ref)
            bias_new_ref[...] = bias_new

        lax.cond(_has_excess_ties(count_ge(thr), thr, n_sel), bias_with_ties, bias_by_threshold)


def _dsa_select(layer, page_table, cache_idx_t, iq, iw, ik_new_t, *, pages, n_new, n_sel):
    n_seq, n_pages = page_table.shape
    n_steps = n_pages // pages
    n_main = pages * PAGE_SIZE
    n_tail = ik_new_t.shape[2]
    rows = iq.shape[1]
    tri = jnp.asarray(np.arange(2 * LANE)[:, None] <= np.arange(2 * LANE)[None, :], BF16)
    in_specs = _page_specs(layer, pages, D_I) + [
        _seq_spec((rows, D_I)), _seq_spec((rows, 1)), _seq_spec((D_I, n_tail)), _const_spec((2 * LANE, 2 * LANE))]
    return pl.pallas_call(
        functools.partial(_dsa_select_kernel, pages=pages, n_new=n_new, n_sel=n_sel),
        out_shape=(jax.ShapeDtypeStruct((n_seq, n_steps, 8, n_main), F32),
                   jax.ShapeDtypeStruct((n_seq, 8, n_tail), F32)),
        grid_spec=pltpu.PrefetchScalarGridSpec(
            num_scalar_prefetch=1, grid=(n_seq, n_steps), in_specs=in_specs,
            out_specs=(pl.BlockSpec((None, n_steps, 8, n_main), lambda b, j, pt: (b, 0, 0, 0)),
                       pl.BlockSpec((None, 8, n_tail), lambda b, j, pt: (b, 0, 0))),
            scratch_shapes=[pltpu.VMEM((D_I, n_main), BF16),
                            pltpu.VMEM((n_steps, 8, n_main), jnp.int32),
                            pltpu.VMEM((8, n_tail), jnp.int32)]),
        compiler_params=_params("parallel", "arbitrary"),
        name="dsa_select")(page_table, *([cache_idx_t] * pages), iq, iw, ik_new_t, tri)


def _dsa_sample_kernel(pt_ref, *refs, pages):
    page_refs = refs[:pages]
    (q_ref, bias_ref, kv_new_ref, bias_new_ref, o_ref, k_buf, v_buf, m_ref, l_ref, acc_ref) = refs[pages:]
    j = pl.program_id(1)
    rows = q_ref.shape[0]
    kw = KVH_B * HEAD_DIM

    @pl.when(j == 0)
    def _():
        _init_softmax(m_ref, l_ref, acc_ref)

    def attend(k_t, v_t, bias):
        n = k_t.shape[1]
        s = jnp.dot(q_ref[...], k_t, preferred_element_type=F32)
        s = (s.reshape(rows // 8, 8, n) + bias[None]).reshape(rows, n)
        _softmax_step_t(s, m_ref, l_ref, acc_ref, v_t)

    for p in range(pages):
        pg = page_refs[p][...]
        k_buf[:, p * PAGE_SIZE:(p + 1) * PAGE_SIZE] = pg[:kw].astype(BF16)
        v_buf[:, p * PAGE_SIZE:(p + 1) * PAGE_SIZE] = pg[kw:].astype(BF16)
    attend(k_buf[...], v_buf[...], bias_ref[...])

    @pl.when(j == pl.num_programs(1) - 1)
    def _():
        kv_new = kv_new_ref[...]
        attend(kv_new[:kw], kv_new[kw:], bias_new_ref[...])
        o_ref[...] = acc_ref[...] / l_ref[...]


def _dsa_sample(layer, page_table, cache_kv_t, q, bias, kv_new_t, bias_new, *, pages):
    n_seq, n_pages = page_table.shape
    n_steps = n_pages // pages
    n_main = pages * PAGE_SIZE
    rows = q.shape[1]
    n_tail = kv_new_t.shape[2]
    kw = KVH_B * HEAD_DIM
    in_specs = _page_specs(layer, pages, 2 * kw) + [
        _seq_spec((rows, kw)),
        pl.BlockSpec((None, None, 8, n_main), lambda b, j, pt: (b, j, 0, 0)),
        _seq_spec((2 * kw, n_tail)), _seq_spec((8, n_tail))]
    return pl.pallas_call(
        functools.partial(_dsa_sample_kernel, pages=pages),
        out_shape=jax.ShapeDtypeStruct((n_seq, rows, kw), F32),
        grid_spec=pltpu.PrefetchScalarGridSpec(
            num_scalar_prefetch=1, grid=(n_seq, n_steps), in_specs=in_specs,
            out_specs=pl.BlockSpec((None, rows, kw), lambda b, j, pt: (b, 0, 0)),
            scratch_shapes=[pltpu.VMEM((kw, n_main), BF16), pltpu.VMEM((kw, n_main), BF16),
                            pltpu.VMEM((rows, 1), F32), pltpu.VMEM((rows, 1), F32),
                            pltpu.VMEM((rows, kw), F32)]),
        compiler_params=_params("parallel", "arbitrary"),
        name="dsa_sample")(page_table, *([cache_kv_t] * pages), q, bias, kv_new_t, bias_new)


def _router_kernel(x_ref, g_ref, w_ref, b_ref, h_ref, logit_ref):
    x = x_ref[...]
    ms = jnp.mean(x * x, axis=-1, keepdims=True)
    h = x * lax.rsqrt(ms + NORM_EPS) * g_ref[...]
    h_ref[...] = h.astype(h_ref.dtype)
    logit_ref[...] = jnp.dot(h, w_ref[...], preferred_element_type=F32, precision=lax.Precision.HIGHEST) + b_ref[...]


def _router(x, gain, w_router, b_router, *, tm=512):
    n, d = x.shape
    e = w_router.shape[1]
    return pl.pallas_call(
        _router_kernel,
        out_shape=(jax.ShapeDtypeStruct((n, d), BF16), jax.ShapeDtypeStruct((n, e), F32)),
        grid=(n // tm,),
        in_specs=[pl.BlockSpec((tm, d), lambda i: (i, 0)), pl.BlockSpec((1, d), lambda i: (0, 0)),
                  pl.BlockSpec((d, e), lambda i: (0, 0)), pl.BlockSpec((1, e), lambda i: (0, 0))],
        out_specs=(pl.BlockSpec((tm, d), lambda i: (i, 0)), pl.BlockSpec((tm, e), lambda i: (i, 0))),
        compiler_params=_params("parallel"),
        name="router")(x, gain.reshape(1, d), w_router, b_router.reshape(1, e))


def _expert_kernel(be_ref, bv_ref, x_ref, w1_ref, b1_ref, w2_ref, b2_ref, o_ref):
    i = pl.program_id(0)

    @pl.when(bv_ref[i] != 0)
    def _():
        d_ff = w2_ref.shape[0]
        u = jnp.dot(x_ref[...], w1_ref[...].astype(BF16), preferred_element_type=F32) + b1_ref[...]
        g = jnp.minimum(u[:, :d_ff], SWIGLU_LIMIT)
        up = jnp.clip(u[:, d_ff:], -SWIGLU_LIMIT, SWIGLU_LIMIT)
        a = g * jax.nn.sigmoid(SWIGLU_ALPHA * g) * (up + 1.0)
        o_ref[...] = jnp.dot(a.astype(BF16), w2_ref[...].astype(BF16), preferred_element_type=F32) + b2_ref[...]

    @pl.when(bv_ref[i] == 0)
    def _():
        o_ref[...] = jnp.zeros(o_ref.shape, o_ref.dtype)


def _experts(block_e, block_valid, xb, w1, b1, w2, b2, *, tm):
    n_rows, d = xb.shape
    e, _, two_ff = w1.shape
    d_ff = w2.shape[1]
    return pl.pallas_call(
        _expert_kernel,
        out_shape=jax.ShapeDtypeStruct((n_rows, d), F32),
        grid_spec=pltpu.PrefetchScalarGridSpec(
            num_scalar_prefetch=2, grid=(n_rows // tm,),
            in_specs=[pl.BlockSpec((tm, d), lambda i, be, bv: (i, 0)),
                      pl.BlockSpec((None, d, two_ff), lambda i, be, bv: (be[i], 0, 0)),
                      pl.BlockSpec((None, 1, two_ff), lambda i, be, bv: (be[i], 0, 0)),
                      pl.BlockSpec((None, d_ff, d), lambda i, be, bv: (be[i], 0, 0)),
                      pl.BlockSpec((None, 1, d), lambda i, be, bv: (be[i], 0, 0))],
            out_specs=pl.BlockSpec((tm, d), lambda i, be, bv: (i, 0))),
        compiler_params=_params("arbitrary"),
        name="experts")(block_e, block_valid, xb, w1, b1.reshape(e, 1, two_ff), w2, b2.reshape(e, 1, d))


def _moe(x, gain, w_router, b_router, w1, b1, w2, b2, *, tm_e=256):
    n, d = x.shape
    h, logits = _router(x, gain, w_router, b_router)
    top_val, top_idx = lax.top_k(logits, TOP_K)
    gate = jax.nn.softmax(top_val, axis=-1)
    flat_e = top_idx.reshape(-1).astype(jnp.int32)
    nk = flat_e.shape[0]
    order = jnp.argsort(flat_e, stable=True).astype(jnp.int32)
    e_sorted = flat_e[order]
    experts = jnp.arange(N_EXPERTS, dtype=jnp.int32)
    counts = jnp.sum((flat_e[None, :] == experts[:, None]).astype(jnp.int32), axis=1)
    start = jnp.cumsum(counts) - counts
    padded = _cdiv(counts, tm_e) * tm_e
    pad_end = jnp.cumsum(padded)
    pad_start = pad_end - padded
    n_blocks = _cdiv(nk, tm_e) + N_EXPERTS
    block_start = jnp.arange(n_blocks, dtype=jnp.int32) * tm_e
    block_e = jnp.minimum(jnp.sum((block_start[:, None] >= pad_end[None, :]).astype(jnp.int32), axis=1), N_EXPERTS - 1)
    block_valid = (block_start < pad_end[-1]).astype(jnp.int32)
    row = jnp.arange(n_blocks * tm_e, dtype=jnp.int32)
    row_e = jnp.repeat(block_e, tm_e)
    offset = row - pad_start[row_e]
    src = jnp.clip(start[row_e] + offset, 0, nk - 1)
    row_tok = jnp.where(offset < counts[row_e], order[src] // TOP_K, 0)
    dest_sorted = pad_start[e_sorted] + jnp.arange(nk, dtype=jnp.int32) - start[e_sorted]
    dest = dest_sorted[jnp.argsort(order)]
    yb = _experts(block_e, block_valid, h[row_tok], w1, b1, w2, b2, tm=tm_e)
    dest_k = dest.reshape(n, TOP_K)
    y = x
    for k in range(TOP_K):
        y = y + yb[dest_k[:, k]] * gate[:, k:k + 1]
    return y


def _rms(x, g, n=None):
    n = x.shape[-1] if n is None else n
    y = x * lax.rsqrt(jnp.sum(x * x, axis=-1, keepdims=True) / n + NORM_EPS)
    return y * g


def _rope_tables(pos, n_rot):
    half = n_rot // 2
    inv_freq = ROPE_THETA ** (-jnp.arange(half, dtype=F32) / half)
    ang = pos.astype(F32)[:, None] * inv_freq[None, :]
    return jnp.cos(ang), jnp.sin(ang)


def _rope(x, cos, sin, n_rot):
    d = x.shape[-1]
    half = n_rot // 2
    cos = cos[:, None, :]
    sin = sin[:, None, :]
    x1 = x[..., d - n_rot:d - half]
    x2 = x[..., d - half:]
    return jnp.concatenate([x[..., :d - n_rot], x1 * cos - x2 * sin, x2 * cos + x1 * sin], axis=-1)


def _regroup_w_in(w_in):
    cuts = np.cumsum([0, Q_LORA, KV_LORA, ROPE_A, H_B * HEAD_DIM, KVH_B * HEAD_DIM, KVH_B * HEAD_DIM,
                      H_I * D_I, D_I, H_I, 2 * w_in.shape[0]])
    w_cq, w_ckv, w_kpe, w_qb, w_kb, w_vb, w_iq, w_ik, w_iw, w_g = [w_in[:, a:b] for a, b in zip(cuts[:-1], cuts[1:])]
    misc_pad = jnp.zeros((w_in.shape[0], COL_GATE - COL_MISC - ROPE_A - D_I - H_I), w_in.dtype)
    return jnp.concatenate([w_cq, w_ckv, w_qb, w_kb, w_vb, w_iq, w_kpe, w_ik, w_iw, misc_pad, w_g], axis=1).astype(BF16)


def _layer(l, xp, xs, page_table, caches, wts, *, t_real):
    (cache_mla, cache_dsa_kv, cache_dsa_idx) = caches
    n_b, t_pad, d = xp.shape
    n_seq, n_ds, _ = xs.shape
    past = page_table.shape[1] * PAGE_SIZE
    n_p = n_b * t_pad
    n_s = n_seq * n_ds
    n_sel_p = min(MAX_SEL, (t_real - N_META) // 4)
    n_sel_s = min(MAX_SEL, (past + n_ds) // 4)

    x_all = jnp.concatenate([xp.reshape(n_p, d), xs.reshape(n_s, d)], axis=0)
    n_all = _round_up(n_p + n_s, 512)
    x_all = jnp.pad(x_all, ((0, n_all - n_p - n_s), (0, 0)))
    pos = jnp.concatenate([jnp.tile(jnp.arange(t_pad), n_b), jnp.tile(past + jnp.arange(n_ds), n_seq),
                           jnp.zeros((n_all - n_p - n_s,), jnp.int32)])
    z = _matmul(x_all, _regroup_w_in(wts["w_in"][l]), gain=wts["g_attn_norm"][l], name="proj_in")

    cos_a, sin_a = _rope_tables(pos, ROPE_A)
    cos_b, sin_b = _rope_tables(pos, HEAD_DIM)
    cos_i, sin_i = _rope_tables(pos, D_I_ROPE)

    w_uq = jnp.pad(wts["w_uq"][l], ((0, 0), (0, 0), (0, LANE - DQK_A))).reshape(Q_LORA, H_A * LANE).astype(BF16)
    q_a = _matmul(z, w_uq, gain=wts["g_cq"][l], col_block=COL_CQ // Q_LORA, k=Q_LORA, name="q_up")
    q_a = q_a.reshape(n_all, H_A, LANE)[..., :DQK_A]
    q_a = _rope(_rms(q_a, wts["g_qa"][l]), cos_a, sin_a, ROPE_A) * (DQK_A ** -0.5)

    c_kv = _rms(z[:, COL_CKV:COL_CKV + KV_LORA], wts["g_ckv"][l])
    k_pe = z[:, COL_MISC:COL_MISC + ROPE_A]
    q_b = _rope(_rms(z[:, COL_QB:COL_QB + H_B * HEAD_DIM].reshape(n_all, H_B, HEAD_DIM), wts["g_qb"][l]),
                cos_b, sin_b, HEAD_DIM) * (HEAD_DIM ** -0.5)
    k_b = _rope(_rms(z[:, COL_KB:COL_KB + KVH_B * HEAD_DIM].reshape(n_all, KVH_B, HEAD_DIM), wts["g_kb"][l]),
                cos_b, sin_b, HEAD_DIM)
    v_b = z[:, COL_VB:COL_VB + KVH_B * HEAD_DIM].reshape(n_all, KVH_B, HEAD_DIM)
    i_q = _rope(z[:, COL_IQ:COL_IQ + H_I * D_I].reshape(n_all, H_I, D_I), cos_i, sin_i, D_I_ROPE)
    i_k = _rope(_rms(z[:, COL_MISC + ROPE_A:COL_MISC + ROPE_A + D_I], wts["g_ik"][l])[:, None, :],
                cos_i, sin_i, D_I_ROPE)[:, 0, :]
    i_w = z[:, COL_MISC + ROPE_A + D_I:COL_MISC + ROPE_A + D_I + H_I] * ((H_I ** -0.5) * (D_I ** -0.5))
    gates = jax.nn.sigmoid(z[:, COL_GATE:COL_GATE + 2 * d])

    mla_rows = jnp.concatenate([c_kv, k_pe], axis=-1)
    kv_rows = jnp.stack([k_b, v_b], axis=1)
    half_sel = (jnp.arange(H_B)[:, None] // GROUP_B == jnp.arange(KVH_B)[None, :]).astype(F32)
    q_b_pad = (q_b[:, :, None, :] * half_sel[None, :, :, None]).reshape(n_all, H_B, KVH_B * HEAD_DIM)

    tq = 128
    nq = t_pad // tq
    n_kc = max(dd for dd in range(1, 5) if nq % dd == 0)
    tk = n_kc * tq
    w_kv = jnp.concatenate([wts["w_uk"][l].reshape(KV_LORA, H_A * NOPE_A), wts["w_uv"][l].reshape(KV_LORA, H_A * V_A)],
                           axis=1).astype(BF16)
    kv_a = _matmul(z, w_kv, gain=wts["g_ckv"][l], col_block=COL_CKV // KV_LORA, k=KV_LORA, name="kv_up")
    k_nope = kv_a[:n_p, :H_A * NOPE_A].reshape(n_p, H_A, NOPE_A)
    v_a = kv_a[:n_p, H_A * NOPE_A:]
    k_a = jnp.concatenate([k_nope, jnp.broadcast_to(k_pe[:n_p, None, :], (n_p, H_A, ROPE_A))], axis=-1)
    k_a = _rope(_rms(k_a, wts["g_ka"][l]), cos_a[:n_p], sin_a[:n_p], ROPE_A)
    pad_h = lambda a: jnp.pad(a, ((0, 0), (0, 0), (0, LANE - DQK_A))).astype(BF16)
    o_a_p = _mla_prompt(pad_h(q_a[:n_p]).reshape(n_b, t_pad, H_A * LANE), pad_h(k_a).reshape(n_b, t_pad, H_A * LANE),
                        v_a.astype(BF16).reshape(n_b, t_pad, H_A * V_A), tq=tk)

    blocked = lambda a: a.reshape((n_b, nq, tq) + a.shape[1:]).swapaxes(2, 3)
    o_b_p = _dsa_prompt(blocked(i_q[:n_p].astype(BF16)), i_w[:n_p].reshape(n_b, t_pad, H_I),
                        blocked(q_b_pad[:n_p].astype(BF16)), i_k[:n_p].astype(BF16).reshape(n_b, t_pad, D_I),
                        k_b[:n_p].astype(BF16).reshape(n_b, t_pad, KVH_B * HEAD_DIM),
                        v_b[:n_p].astype(BF16).reshape(n_b, t_pad, KVH_B * HEAD_DIM), tq=tq, tk=tk, n_sel=n_sel_p)
    o_b_p = o_b_p.reshape(n_b, nq, KVH_B, GROUP_B, tq, KVH_B, HEAD_DIM)
    o_b_p = jnp.stack([o_b_p[:, :, g, :, :, g, :] for g in range(KVH_B)], axis=2)
    o_b_p = o_b_p.reshape(n_b, nq, H_B, tq, HEAD_DIM).swapaxes(2, 3).reshape(n_p, H_B * HEAD_DIM)

    sl = slice(n_p, n_p + n_s)
    pages = max(dd for dd in range(1, MAX_PAGES_PER_STEP + 1) if page_table.shape[1] % dd == 0)
    assert 8 % n_ds == 0
    n_tail = LANE
    tail_t = lambda a: jnp.pad(a.reshape(n_seq, n_ds, a.shape[-1]), ((0, 0), (0, n_tail - n_ds), (0, 0))).swapaxes(1, 2)
    cache_mla_t = jnp.swapaxes(cache_mla, 2, 3)
    cache_idx_t = jnp.swapaxes(cache_dsa_idx, 2, 3)
    cache_kv_t = jnp.moveaxis(cache_dsa_kv, 2, -1).reshape(cache_dsa_kv.shape[:2] + (2 * KVH_B * HEAD_DIM, PAGE_SIZE))
    g_ka = wts["g_ka"][l]
    pos_k = jnp.arange(past + n_tail)
    cos_k, sin_k = _rope_tables(pos_k, ROPE_A)
    g1, g2 = g_ka[NOPE_A:NOPE_A + ROPE_A // 2], g_ka[NOPE_A + ROPE_A // 2:]
    cg_t = jnp.concatenate([cos_k * g1, cos_k * g2], axis=1).T
    sg_t = jnp.concatenate([sin_k * g1, -sin_k * g2], axis=1).T
    qa_s = q_a[sl]
    rows_a = n_ds * H_A
    eye_h = jnp.eye(H_A, dtype=F32)
    qn_s = qa_s[..., :NOPE_A] * g_ka[:NOPE_A]
    qbd = (qn_s[:, :, None, :] * eye_h[None, :, :, None]).reshape(n_seq, rows_a, H_A * NOPE_A)
    q_r = qa_s[:, :, NOPE_A:].reshape(n_seq, rows_a, ROPE_A)
    q_r_swap = jnp.concatenate([q_r[..., ROPE_A // 2:], q_r[..., :ROPE_A // 2]], axis=-1)
    wt = wts["w_uk"][l].reshape(KV_LORA, H_A * NOPE_A).T
    o_lat = _mla_sample(l, page_table, cache_mla_t, wt.astype(BF16), qbd.astype(BF16), q_r.astype(BF16),
                        q_r_swap.astype(BF16), cg_t[:, :past], sg_t[:, :past], tail_t(mla_rows[sl]),
                        cg_t[:, past:], sg_t[:, past:], pages=pages, n_new=n_ds)
    o_a_s = jnp.einsum("bqhr,rhv->bqhv", o_lat.reshape(n_seq, n_ds, H_A, KV_LORA), wts["w_uv"][l])
    o_a_s = o_a_s.reshape(n_s, H_A * V_A)

    slot = jnp.arange(8) % n_ds
    iq_s = i_q[sl].reshape(n_seq, n_ds, H_I, D_I)[:, slot].swapaxes(1, 2).reshape(n_seq, H_I * 8, D_I)
    iw_s = i_w[sl].reshape(n_seq, n_ds, H_I)[:, slot].swapaxes(1, 2).reshape(n_seq, H_I * 8, 1)
    bias, bias_new = _dsa_select(l, page_table, cache_idx_t, iq_s.astype(BF16), iw_s,
                                 tail_t(i_k[sl]).astype(BF16), pages=pages, n_new=n_ds, n_sel=n_sel_s)
    qb_s = q_b_pad[sl].reshape(n_seq, n_ds, H_B, KVH_B * HEAD_DIM).swapaxes(1, 2).reshape(n_seq, H_B * n_ds, -1)
    kv_new_t = tail_t(kv_rows[sl].reshape(n_s, 2 * KVH_B * HEAD_DIM)).astype(BF16)
    o_b_s = _dsa_sample(l, page_table, cache_kv_t, qb_s.astype(BF16), bias, kv_new_t, bias_new, pages=pages)
    o_b_s = o_b_s.reshape(n_seq, KVH_B, GROUP_B, n_ds, KVH_B, HEAD_DIM)
    o_b_s = jnp.stack([o_b_s[:, g, :, :, g, :] for g in range(KVH_B)], axis=1)
    o_b_s = o_b_s.reshape(n_seq, H_B, n_ds, HEAD_DIM).swapaxes(1, 2).reshape(n_s, H_B * HEAD_DIM)

    zeros_tail = jnp.zeros((n_all - n_p - n_s, H_A * V_A), BF16)
    o_a = jnp.concatenate([o_a_p.reshape(n_p, H_A * V_A), o_a_s.astype(BF16), zeros_tail], axis=0)
    o_b = jnp.concatenate([o_b_p, o_b_s.astype(BF16), zeros_tail], axis=0)
    y_a = _matmul(o_a, wts["w_pa"][l].astype(BF16), name="proj_a")
    y_b = _matmul(o_b, wts["w_pb"][l].astype(BF16), name="proj_b")
    merged = gates[:, :d] * y_a + gates[:, d:] * y_b
    x_all = x_all + _matmul(merged, wts["w_o"][l].astype(BF16), name="proj_o")
    x_all = _moe(x_all, wts["g_ffn_norm"][l], wts["w_router"][l], wts["b_router"][l], wts["w_mlp1"][l],
                 wts["b_mlp1"][l], wts["w_mlp2"][l], wts["b_mlp2"][l])

    new_p = (mla_rows[:n_p].reshape(n_b, t_pad, -1)[:, :t_real],
             kv_rows[:n_p].reshape(n_b, t_pad, 2, KVH_B, HEAD_DIM)[:, :t_real],
             i_k[:n_p].reshape(n_b, t_pad, D_I)[:, :t_real])
    new_s = (mla_rows[sl].reshape(n_seq, n_ds, -1), kv_rows[sl].reshape(n_seq, n_ds, 2, KVH_B, HEAD_DIM),
             i_k[sl].reshape(n_seq, n_ds, D_I))
    return x_all[:n_p].reshape(n_b, t_pad, d), x_all[sl].reshape(n_seq, n_ds, d), new_p, new_s


def kernel(x_prompt, x_sample, cache_mla, cache_dsa_kv, cache_dsa_idx, page_table, meta_tokens, g_attn_norm, w_in,
           g_cq, w_uq, g_qa, g_ckv, w_uk, w_uv, g_ka, g_qb, g_kb, g_ik, w_pa, w_pb, w_o, g_ffn_norm, w_router,
           b_router, w_mlp1, b_mlp1, w_mlp2, b_mlp2):
    n_b, n_s, d = x_prompt.shape
    t_real = n_s + N_META
    t_pad = _round_up(t_real, LANE)
    wts = dict(g_attn_norm=g_attn_norm, w_in=w_in, g_cq=g_cq, w_uq=w_uq, g_qa=g_qa, g_ckv=g_ckv, w_uk=w_uk, w_uv=w_uv,
               g_ka=g_ka, g_qb=g_qb, g_kb=g_kb, g_ik=g_ik, w_pa=w_pa, w_pb=w_pb, w_o=w_o, g_ffn_norm=g_ffn_norm,
               w_router=w_router, b_router=b_router, w_mlp1=w_mlp1, b_mlp1=b_mlp1, w_mlp2=w_mlp2, b_mlp2=b_mlp2)
    meta = jnp.broadcast_to(meta_tokens[None].astype(x_prompt.dtype), (n_b, N_META, d))
    xp = jnp.concatenate([meta, x_prompt, jnp.zeros((n_b, t_pad - t_real, d), x_prompt.dtype)], axis=1)
    xs = x_sample
    outs_p, outs_s = [], []
    for l in range(w_in.shape[0]):
        xp, xs, new_p, new_s = _layer(l, xp, xs, page_table, (cache_mla, cache_dsa_kv, cache_dsa_idx), wts,
                                      t_real=t_real)
        outs_p.append(new_p)
        outs_s.append(new_s)
    stack = lambda outs, i: jnp.stack([o[i] for o in outs], axis=0)
    return (xp[:, N_META:t_real], xs, stack(outs_p, 0), stack(outs_p, 1), stack(outs_p, 2),
            stack(outs_s, 0), stack(outs_s, 1), stack(outs_s, 2))
```

```python
import functools

import jax
import jax.numpy as jnp
import numpy as np
from jax import lax
from jax.experimental import pallas as pl
from jax.experimental.pallas import tpu as pltpu

N_META = 16
HEAD_DIM = 64
H_A = 8
NOPE_A = 64
ROPE_A = 32
DQK_A = NOPE_A + ROPE_A
V_A = 64
Q_LORA = 512
KV_LORA = 256
H_B = 8
KVH_B = 2
GROUP_B = H_B // KVH_B
H_I = 8
D_I = 32
D_I_ROPE = 16
MAX_SEL = 256
N_EXPERTS = 32
TOP_K = 4
SWIGLU_LIMIT = 7.0
SWIGLU_ALPHA = 1.702
ROPE_THETA = 10000.0
NORM_EPS = 1e-6
PAGE_SIZE = 128

LANE = 128
MASK_VALUE = -1e30
INT_MIN = -(2 ** 31)
VMEM_LIMIT = 56 * 1024 * 1024
MAX_PAGES_PER_STEP = 16

COL_CQ = 0
COL_CKV = 512
COL_QB = 768
COL_KB = 1280
COL_VB = 1408
COL_IQ = 1536
COL_MISC = 1792
COL_GATE = 2048
N_PROJ = 4096

F32 = jnp.float32
BF16 = jnp.bfloat16
NT_DIMS = (((1,), (1,)), ((), ()))


def _cdiv(a, b):
    return (a + b - 1) // b


def _round_up(a, b):
    return _cdiv(a, b) * b


def _params(*sem):
    return pltpu.CompilerParams(dimension_semantics=sem, vmem_limit_bytes=VMEM_LIMIT)


def _mm_kernel(x_ref, w_ref, o_ref):
    o_ref[...] = jnp.dot(x_ref[...].astype(BF16), w_ref[...], preferred_element_type=F32).astype(o_ref.dtype)


def _mm_norm_kernel(x_ref, g_ref, w_ref, o_ref, xn_ref):
    @pl.when(pl.program_id(1) == 0)
    def _():
        x = x_ref[...].astype(F32)
        ms = jnp.mean(x * x, axis=-1, keepdims=True)
        xn_ref[...] = (x * lax.rsqrt(ms + NORM_EPS) * g_ref[...]).astype(BF16)

    o_ref[...] = jnp.dot(xn_ref[...], w_ref[...], preferred_element_type=F32).astype(o_ref.dtype)


def _matmul(x, w, *, gain=None, col_block=0, k=None, tm=512, tn=512, out_dtype=F32, name="mm"):
    m = x.shape[0]
    k = x.shape[1] if k is None else k
    n = w.shape[1]
    tn = min(tn, n)
    assert m % tm == 0 and n % tn == 0 and w.shape[0] == k
    grid = (m // tm, n // tn)
    x_spec = pl.BlockSpec((tm, k), lambda i, j: (i, col_block))
    w_spec = pl.BlockSpec((k, tn), lambda i, j: (0, j))
    o_spec = pl.BlockSpec((tm, tn), lambda i, j: (i, j))
    out_shape = jax.ShapeDtypeStruct((m, n), out_dtype)
    if gain is None:
        return pl.pallas_call(_mm_kernel, out_shape=out_shape, grid=grid, in_specs=[x_spec, w_spec],
                              out_specs=o_spec, compiler_params=_params("parallel", "arbitrary"), name=name)(x, w)
    g_spec = pl.BlockSpec((1, k), lambda i, j: (0, 0))
    return pl.pallas_call(_mm_norm_kernel, out_shape=out_shape, grid=grid, in_specs=[x_spec, g_spec, w_spec],
                          out_specs=o_spec, scratch_shapes=[pltpu.VMEM((tm, k), BF16)],
                          compiler_params=_params("parallel", "arbitrary"), name=name)(
                              x, gain.reshape(1, k).astype(F32), w)


def _merge_kernel(x_ref, oa_ref, ob_ref, zg_ref, wpa_ref, wpb_ref, wo_ref, o_ref):
    d = x_ref.shape[1]
    y_a = jnp.dot(oa_ref[...], wpa_ref[...], preferred_element_type=F32)
    y_b = jnp.dot(ob_ref[...], wpb_ref[...], preferred_element_type=F32)
    zg = zg_ref[...]
    merged = jax.nn.sigmoid(zg[:, :d]) * y_a + jax.nn.sigmoid(zg[:, d:]) * y_b
    o_ref[...] = x_ref[...] + jnp.dot(merged.astype(BF16), wo_ref[...], preferred_element_type=F32)


def _merge(x, o_a, o_b, z, w_pa, w_pb, w_o, *, tm=512):
    n, d = x.shape
    ka, kb = o_a.shape[1], o_b.shape[1]
    const = lambda shape: pl.BlockSpec(shape, lambda i: (0, 0))
    return pl.pallas_call(
        _merge_kernel,
        out_shape=jax.ShapeDtypeStruct((n, d), F32),
        grid=(n // tm,),
        in_specs=[pl.BlockSpec((tm, d), lambda i: (i, 0)), pl.BlockSpec((tm, ka), lambda i: (i, 0)),
                  pl.BlockSpec((tm, kb), lambda i: (i, 0)), pl.BlockSpec((tm, 2 * d), lambda i: (i, COL_GATE // (2 * d))),
                  const((ka, d)), const((kb, d)), const((d, d))],
        out_specs=pl.BlockSpec((tm, d), lambda i: (i, 0)),
        compiler_params=_params("parallel"),
        name="merge")(x, o_a, o_b, z, w_pa, w_pb, w_o)


def _softmax_step(s, m_prev, l_prev, acc_prev, v_blk):
    m_new = jnp.maximum(m_prev, jnp.max(s, axis=1, keepdims=True))
    alpha = jnp.exp(m_prev - m_new)
    p = jnp.exp(s - m_new)
    l_new = alpha * l_prev + jnp.sum(p, axis=1, keepdims=True)
    acc_new = alpha * acc_prev + jnp.dot(p.astype(BF16), v_blk, preferred_element_type=F32)
    return m_new, l_new, acc_new


def _sortable_key(x):
    b = lax.bitcast_convert_type(x + 0.0, jnp.int32)
    return jnp.where(b < 0, b ^ jnp.int32(0x7FFFFFFF), b)


KEY_NEG_INF = INT_MIN + 0x7FFFFF


def _has_excess_ties(n_ge, thr, n_sel):
    return jnp.max(jnp.where(n_ge > n_sel, jnp.where(thr > KEY_NEG_INF, 1, 0), 0)) > 0


def _kth_largest_key(count_ge, n_sel, rows):
    def step(s, t):
        bit = lax.shift_left(jnp.int32(1), jnp.int32(31) - s)
        cand = t + bit
        return jnp.where(count_ge(cand) >= n_sel, cand, t)

    return lax.fori_loop(0, 32, step, jnp.full((rows, 1), INT_MIN, jnp.int32))


MLA_HEADS_PER_STEP = 4


def _mla_prompt_kernel(q_ref, k_ref, v_ref, o_ref, *, tq):
    i = pl.program_id(2)
    nh = MLA_HEADS_PER_STEP
    q = q_ref[...]
    causal = lax.broadcasted_iota(jnp.int32, (tq, tq), 1) <= lax.broadcasted_iota(jnp.int32, (tq, tq), 0)

    def chunk(c, carry, diagonal):
        m, l, acc = carry
        start = pl.multiple_of(c * tq, tq)
        k_blk = k_ref[pl.ds(start, tq), :]
        v_blk = v_ref[pl.ds(start, tq), :]
        s = jnp.concatenate([lax.dot_general(q[:, h * LANE:(h + 1) * LANE], k_blk[:, h * LANE:(h + 1) * LANE],
                                             NT_DIMS, preferred_element_type=F32) for h in range(nh)], axis=0)
        if diagonal:
            s = jnp.where(causal[None], s.reshape(nh, tq, tq), MASK_VALUE).reshape(nh * tq, tq)
        m_new = jnp.maximum(m, jnp.max(s, axis=1, keepdims=True))
        alpha = jnp.exp(m - m_new)
        p = jnp.exp(s - m_new)
        l_new = alpha * l + jnp.sum(p, axis=1, keepdims=True)
        p = p.astype(BF16)
        pv = jnp.concatenate([jnp.dot(p[h * tq:(h + 1) * tq], v_blk[:, (h // 2) * LANE:(h // 2 + 1) * LANE],
                                      preferred_element_type=F32) for h in range(nh)], axis=0)
        return m_new, l_new, alpha * acc + pv

    init = (jnp.full((nh * tq, 1), MASK_VALUE, F32), jnp.zeros((nh * tq, 1), F32), jnp.zeros((nh * tq, LANE), F32))
    carry = lax.fori_loop(0, i, functools.partial(chunk, diagonal=False), init)
    _, l, acc = chunk(i, carry, True)
    o = acc / l
    lane = lax.broadcasted_iota(jnp.int32, (tq, LANE), 1)
    pairs = [jnp.where(lane < V_A, o[(2 * g) * tq:(2 * g + 1) * tq], o[(2 * g + 1) * tq:(2 * g + 2) * tq])
             for g in range(nh // 2)]
    o_ref[...] = jnp.concatenate(pairs, axis=1).astype(o_ref.dtype)


def _mla_prompt(q, k, v, *, tq):
    b, t, _ = q.shape
    nh = MLA_HEADS_PER_STEP
    grid = (b, H_A // nh, t // tq)
    return pl.pallas_call(
        functools.partial(_mla_prompt_kernel, tq=tq),
        out_shape=jax.ShapeDtypeStruct((b, t, H_A * V_A), BF16),
        grid=grid,
        in_specs=[pl.BlockSpec((None, tq, nh * LANE), lambda bi, g, i: (bi, i, g)),
                  pl.BlockSpec((None, t, nh * LANE), lambda bi, g, i: (bi, 0, g)),
                  pl.BlockSpec((None, t, nh * V_A), lambda bi, g, i: (bi, 0, g))],
        out_specs=pl.BlockSpec((None, tq, nh * V_A), lambda bi, g, i: (bi, i, g)),
        compiler_params=_params("parallel", "parallel", "arbitrary"),
        name="mla_prompt")(q, k, v)


def _select_bias(key, thr, need, carry, valid, tri_ref):
    w = min(tri_ref.shape[0], key.shape[1])
    tri = tri_ref[0:w, 0:w]
    parts = []
    for t in range(key.shape[1] // w):
        k_t = key[:, t * w:(t + 1) * w]
        eq = k_t == thr
        rank = jnp.dot(jnp.where(eq, 1.0, 0.0).astype(BF16), tri, preferred_element_type=F32) + carry
        carry = jnp.max(rank, axis=1, keepdims=True)
        tie = jnp.where(rank <= need, 0.0, MASK_VALUE)
        parts.append(jnp.where(k_t > thr, 0.0, jnp.where(eq, tie, MASK_VALUE)))
    bias = parts[0] if len(parts) == 1 else jnp.concatenate(parts, axis=1)
    if valid is not None:
        bias = jnp.where(valid, bias, MASK_VALUE)
    return bias, carry


def _dsa_prompt_kernel(iq_ref, w_ref, q_ref, ik_ref, k_ref, v_ref, tri_ref, o_ref, key_ref, bias_ref, wb_ref,
                       *, tq, tk, n_sel):
    i = pl.program_id(1)
    n_c = ((i + 1) * tq + tk - 1) // tk
    q_pos = i * tq + lax.broadcasted_iota(jnp.int32, (tq, tk), 0)
    lane_pos = lax.broadcasted_iota(jnp.int32, (tq, tk), 1)

    w = w_ref[...]
    for h in range(H_I):
        wb_ref[h] = jnp.broadcast_to(w[:, h:h + 1], (tq, LANE))
    iq = iq_ref[...].reshape(H_I * tq, D_I)

    def score_chunk(c, _):
        start = pl.multiple_of(c * tk, tk)
        logit = lax.dot_general(iq, ik_ref[pl.ds(start, tk), :], NT_DIMS, preferred_element_type=F32)
        cols = []
        for t in range(tk // LANE):
            acc = jnp.zeros((tq, LANE), F32)
            for h in range(H_I):
                acc = acc + jnp.maximum(logit[h * tq:(h + 1) * tq, t * LANE:(t + 1) * LANE], 0.0) * wb_ref[h]
            cols.append(acc)
        score = jnp.concatenate(cols, axis=1)
        score = jnp.where(start + lane_pos <= q_pos, score, -jnp.inf)
        key_ref[c] = _sortable_key(score)
        return 0

    lax.fori_loop(0, n_c, score_chunk, 0)

    def count_ge(cand):
        cand_b = jnp.broadcast_to(cand, (tq, LANE))

        def body(c, cnt):
            key = key_ref[c]
            for t in range(tk // LANE):
                cnt = cnt + jnp.where(key[:, t * LANE:(t + 1) * LANE] >= cand_b, 1, 0)
            return cnt

        cnt = lax.fori_loop(0, n_c, body, jnp.zeros((tq, LANE), jnp.int32))
        return jnp.sum(cnt, axis=1, keepdims=True)

    thr = _kth_largest_key(count_ge, n_sel, tq)

    def bias_by_threshold():
        thr_eff = jnp.maximum(thr, KEY_NEG_INF + 1)

        def body(c, _):
            bias_ref[c] = jnp.where(key_ref[c] >= thr_eff, 0.0, MASK_VALUE)
            return 0

        lax.fori_loop(0, n_c, body, 0)

    def bias_with_ties():
        need = (n_sel - count_ge(thr + 1)).astype(F32)

        def body(c, carry):
            start = pl.multiple_of(c * tk, tk)
            bias, carry = _select_bias(key_ref[c], thr, need, carry, start + lane_pos <= q_pos, tri_ref)
            bias_ref[c] = bias
            return carry

        lax.fori_loop(0, n_c, body, jnp.zeros((tq, 1), F32))

    lax.cond(_has_excess_ties(count_ge(thr), thr, n_sel), bias_with_ties, bias_by_threshold)

    rows = H_B * tq
    qa = q_ref[...].reshape(rows, LANE)

    def attend(c, carry):
        start = pl.multiple_of(c * tk, tk)
        s = lax.dot_general(qa, k_ref[pl.ds(start, tk), :], NT_DIMS, preferred_element_type=F32)
        s = (s.reshape(H_B, tq, tk) + bias_ref[c][None]).reshape(rows, tk)
        return _softmax_step(s, *carry, v_ref[pl.ds(start, tk), :])

    m, l, acc = lax.fori_loop(0, n_c, attend, (jnp.full((rows, 1), MASK_VALUE, F32),
                                               jnp.zeros((rows, 1), F32), jnp.zeros((rows, LANE), F32)))
    o_ref[...] = (acc / l).reshape(H_B, tq, LANE).astype(o_ref.dtype)


def _dsa_prompt(iq, iw, q, ik, k, v, *, tq, tk, n_sel):
    b, nq = iq.shape[:2]
    t = ik.shape[1]
    n_chunks = t // tk
    tri_w = 2 * LANE if tk % (2 * LANE) == 0 else LANE
    tri = jnp.asarray(np.arange(tri_w)[:, None] <= np.arange(tri_w)[None, :], BF16)
    return pl.pallas_call(
        functools.partial(_dsa_prompt_kernel, tq=tq, tk=tk, n_sel=n_sel),
        out_shape=jax.ShapeDtypeStruct((b, nq, H_B, tq, LANE), BF16),
        grid=(b, nq),
        in_specs=[pl.BlockSpec((None, None, H_I, tq, D_I), lambda bi, i: (bi, i, 0, 0, 0)),
                  pl.BlockSpec((None, tq, H_I), lambda bi, i: (bi, i, 0)),
                  pl.BlockSpec((None, None, H_B, tq, LANE), lambda bi, i: (bi, i, 0, 0, 0)),
                  pl.BlockSpec((None, t, D_I), lambda bi, i: (bi, 0, 0)),
                  pl.BlockSpec((None, t, LANE), lambda bi, i: (bi, 0, 0)),
                  pl.BlockSpec((None, t, LANE), lambda bi, i: (bi, 0, 0)),
                  pl.BlockSpec((tri_w, tri_w), lambda bi, i: (0, 0))],
        out_specs=pl.BlockSpec((None, None, H_B, tq, LANE), lambda bi, i: (bi, i, 0, 0, 0)),
        scratch_shapes=[pltpu.VMEM((n_chunks, tq, tk), jnp.int32),
                        pltpu.VMEM((n_chunks, tq, tk), F32),
                        pltpu.VMEM((H_I, tq, LANE), F32)],
        compiler_params=_params("parallel", "arbitrary"),
        name="dsa_prompt")(iq, iw, q, ik, k, v, tri)


def _page_specs(layer, pages, width):
    def spec(p):
        return pl.BlockSpec((None, None, width, PAGE_SIZE), lambda b, j, pt: (layer, pt[b, j * pages + p], 0, 0))
    return [spec(p) for p in range(pages)]


def _seq_spec(shape):
    return pl.BlockSpec((None,) + shape, lambda b, j, pt: (b,) + tuple(0 for _ in shape))


def _const_spec(shape):
    return pl.BlockSpec(shape, lambda b, j, pt: tuple(0 for _ in shape))


def _softmax_step_t(s, m_ref, l_ref, acc_ref, v_t):
    m_prev = m_ref[...]
    m_new = jnp.maximum(m_prev, jnp.max(s, axis=1, keepdims=True))
    alpha = jnp.exp(m_prev - m_new)
    p = jnp.exp(s - m_new)
    l_ref[...] = alpha * l_ref[...] + jnp.sum(p, axis=1, keepdims=True)
    acc_ref[...] = alpha * acc_ref[...] + lax.dot_general(p.astype(BF16), v_t, NT_DIMS, preferred_element_type=F32)
    m_ref[...] = m_new


def _init_softmax(m_ref, l_ref, acc_ref):
    m_ref[...] = jnp.full(m_ref.shape, MASK_VALUE, F32)
    l_ref[...] = jnp.zeros(l_ref.shape, F32)
    acc_ref[...] = jnp.zeros(acc_ref.shape, F32)


def _mla_sample_kernel(pt_ref, *refs, pages, n_new):
    page_refs = refs[:pages]
    (wt_ref, qbd_ref, qr_ref, qs_ref, cg_ref, sg_ref, new_ref, cg_new_ref, sg_new_ref,
     o_ref, lhs_ref, c_buf, pe_buf, m_ref, l_ref, acc_ref) = refs[pages:]
    j = pl.program_id(1)
    rows = o_ref.shape[0]
    n_k = H_A * NOPE_A

    @pl.when(j == 0)
    def _():
        wt = wt_ref[...]
        lhs_ref[0:n_k, :] = wt
        lhs_ref[n_k:, :] = jnp.dot(qbd_ref[...], wt, preferred_element_type=F32).astype(BF16)
        _init_softmax(m_ref, l_ref, acc_ref)

    def attend(c_t, pe_t, cg, sg, mask):
        n = c_t.shape[1]
        big = jnp.dot(lhs_ref[...], c_t, preferred_element_type=F32)
        ssq = jnp.zeros((H_A, n), F32)
        row_id = lax.broadcasted_iota(jnp.int32, (H_A, n), 0)
        for h in range(H_A):
            kh = big[h * NOPE_A:(h + 1) * NOPE_A]
            ssq = jnp.where(row_id == h, jnp.sum(kh * kh, axis=0, keepdims=True), ssq)
        ssq = ssq + jnp.sum(pe_t * pe_t, axis=0, keepdims=True)
        inv_rms = lax.rsqrt(ssq * (1.0 / DQK_A) + NORM_EPS)
        s_rope = (jnp.dot(qr_ref[...], (pe_t * cg).astype(BF16), preferred_element_type=F32)
                  + jnp.dot(qs_ref[...], (pe_t * sg).astype(BF16), preferred_element_type=F32))
        s = ((big[n_k:] + s_rope).reshape(rows // H_A, H_A, n) * inv_rms[None]).reshape(rows, n)
        if mask is not None:
            s = jnp.where(mask, s, MASK_VALUE)
        _softmax_step_t(s, m_ref, l_ref, acc_ref, c_t)

    for p in range(pages):
        pg = page_refs[p][...]
        c_buf[:, p * PAGE_SIZE:(p + 1) * PAGE_SIZE] = pg[:KV_LORA].astype(BF16)
        pe_buf[:, p * PAGE_SIZE:(p + 1) * PAGE_SIZE] = pg[KV_LORA:]
    attend(c_buf[...], pe_buf[...], cg_ref[...], sg_ref[...], None)

    @pl.when(j == pl.num_programs(1) - 1)
    def _():
        new = new_ref[...]
        n = new.shape[1]
        key_i = lax.broadcasted_iota(jnp.int32, (rows, n), 1)
        q_i = lax.broadcasted_iota(jnp.int32, (rows, n), 0) // H_A
        attend(new[:KV_LORA].astype(BF16), new[KV_LORA:], cg_new_ref[...], sg_new_ref[...],
               (key_i <= q_i) & (key_i < n_new))
        o_ref[...] = acc_ref[...] / l_ref[...]


def _mla_sample(layer, page_table, cache_t, wt, qbd, qr, qs, cg_t, sg_t, new_t, cg_new_t, sg_new_t, *, pages, n_new):
    n_seq, n_pages = page_table.shape
    rows = qbd.shape[1]
    n_main = pages * PAGE_SIZE
    n_tail = new_t.shape[2]
    width = KV_LORA + ROPE_A
    in_specs = _page_specs(layer, pages, width) + [
        _const_spec((H_A * NOPE_A, KV_LORA)),
        _seq_spec((rows, H_A * NOPE_A)), _seq_spec((rows, ROPE_A)), _seq_spec((rows, ROPE_A)),
        pl.BlockSpec((ROPE_A, n_main), lambda b, j, pt: (0, j)),
        pl.BlockSpec((ROPE_A, n_main), lambda b, j, pt: (0, j)),
        _seq_spec((width, n_tail)), _const_spec((ROPE_A, n_tail)), _const_spec((ROPE_A, n_tail))]
    return pl.pallas_call(
        functools.partial(_mla_sample_kernel, pages=pages, n_new=n_new),
        out_shape=jax.ShapeDtypeStruct((n_seq, rows, KV_LORA), F32),
        grid_spec=pltpu.PrefetchScalarGridSpec(
            num_scalar_prefetch=1, grid=(n_seq, n_pages // pages), in_specs=in_specs,
            out_specs=pl.BlockSpec((None, rows, KV_LORA), lambda b, j, pt: (b, 0, 0)),
            scratch_shapes=[pltpu.VMEM((H_A * NOPE_A + rows, KV_LORA), BF16),
                            pltpu.VMEM((KV_LORA, n_main), BF16),
                            pltpu.VMEM((ROPE_A, n_main), F32),
                            pltpu.VMEM((rows, 1), F32), pltpu.VMEM((rows, 1), F32),
                            pltpu.VMEM((rows, KV_LORA), F32)]),
        compiler_params=_params("parallel", "arbitrary"),
        name="mla_sample")(page_table, *([cache_t] * pages), wt, qbd, qr, qs, cg_t, sg_t, new_t, cg_new_t, sg_new_t)


def _dsa_select_kernel(pt_ref, *refs, pages, n_new, n_sel):
    page_refs = refs[:pages]
    (iq_ref, w_ref, ik_new_ref, tri_ref, bias_ref, bias_new_ref, ik_buf, key_ref, key_new_ref) = refs[pages:]
    j = pl.program_id(1)
    n_steps = key_ref.shape[0]
    n_main = ik_buf.shape[1]
    n_tail = ik_new_ref.shape[1]
    nq = 8

    def scores(ik_t):
        logit = jnp.dot(iq_ref[...], ik_t, preferred_element_type=F32)
        weighted = jnp.maximum(logit, 0.0) * w_ref[...]
        return jnp.sum(weighted.reshape(H_I, nq, ik_t.shape[1]), axis=0)

    for p in range(pages):
        ik_buf[:, p * PAGE_SIZE:(p + 1) * PAGE_SIZE] = page_refs[p][...].astype(BF16)
    key_ref[j] = _sortable_key(scores(ik_buf[...]))

    @pl.when(j == n_steps - 1)
    def _():
        key_i = lax.broadcasted_iota(jnp.int32, (nq, n_tail), 1)
        q_i = lax.broadcasted_iota(jnp.int32, (nq, n_tail), 0) % n_new
        valid_new = (key_i <= q_i) & (key_i < n_new)
        key_new_ref[...] = _sortable_key(jnp.where(valid_new, scores(ik_new_ref[...]), -jnp.inf))

        def count_ge(cand):
            cand_b = jnp.broadcast_to(cand, (nq, LANE))
            cnts = [jnp.zeros((nq, LANE), jnp.int32) for _ in range(4)]
            tiles = [key_ref[c, :, t * LANE:(t + 1) * LANE] for c in range(n_steps) for t in range(n_main // LANE)]
            tiles += [key_new_ref[:, t * LANE:(t + 1) * LANE] for t in range(n_tail // LANE)]
            for idx, tile in enumerate(tiles):
                cnts[idx % 4] = cnts[idx % 4] + jnp.where(tile >= cand_b, 1, 0)
            return jnp.sum((cnts[0] + cnts[1]) + (cnts[2] + cnts[3]), axis=1, keepdims=True)

        thr = _kth_largest_key(count_ge, n_sel, nq)

        def bias_by_threshold():
            thr_eff = jnp.maximum(thr, KEY_NEG_INF + 1)

            def body(c, _):
                bias_ref[c] = jnp.where(key_ref[c] >= thr_eff, 0.0, MASK_VALUE)
                return 0

            lax.fori_loop(0, n_steps, body, 0)
            bias_new_ref[...] = jnp.where(key_new_ref[...] >= thr_eff, 0.0, MASK_VALUE)

        def bias_with_ties():
            need = (n_sel - count_ge(thr + 1)).astype(F32)

            def body(c, carry):
                bias, carry = _select_bias(key_ref[c], thr, need, carry, None, tri_ref)
                bias_ref[c] = bias
                return carry

            carry = lax.fori_loop(0, n_steps, body, jnp.zeros((nq, 1), F32))
            bias_new, _ = _select_bias(key_new_ref[...], thr, need, carry, valid_new, tri_ref)
            bias_new_ref[...] = bias_new

        lax.cond(_has_excess_ties(count_ge(thr), thr, n_sel), bias_with_ties, bias_by_threshold)


def _dsa_select(layer, page_table, cache_idx_t, iq, iw, ik_new_t, *, pages, n_new, n_sel):
    n_seq, n_pages = page_table.shape
    n_steps = n_pages // pages
    n_main = pages * PAGE_SIZE
    n_tail = ik_new_t.shape[2]
    rows = iq.shape[1]
    tri = jnp.asarray(np.arange(2 * LANE)[:, None] <= np.arange(2 * LANE)[None, :], BF16)
    in_specs = _page_specs(layer, pages, D_I) + [
        _seq_spec((rows, D_I)), _seq_spec((rows, 1)), _seq_spec((D_I, n_tail)), _const_spec((2 * LANE, 2 * LANE))]
    return pl.pallas_call(
        functools.partial(_dsa_select_kernel, pages=pages, n_new=n_new, n_sel=n_sel),
        out_shape=(jax.ShapeDtypeStruct((n_seq, n_steps, 8, n_main), F32),
                   jax.ShapeDtypeStruct((n_seq, 8, n_tail), F32)),
        grid_spec=pltpu.PrefetchScalarGridSpec(
            num_scalar_prefetch=1, grid=(n_seq, n_steps), in_specs=in_specs,
            out_specs=(pl.BlockSpec((None, n_steps, 8, n_main), lambda b, j, pt: (b, 0, 0, 0)),
                       pl.BlockSpec((None, 8, n_tail), lambda b, j, pt: (b, 0, 0))),
            scratch_shapes=[pltpu.VMEM((D_I, n_main), BF16),
                            pltpu.VMEM((n_steps, 8, n_main), jnp.int32),
                            pltpu.VMEM((8, n_tail), jnp.int32)]),
        compiler_params=_params("parallel", "arbitrary"),
        name="dsa_select")(page_table, *([cache_idx_t] * pages), iq, iw, ik_new_t, tri)


def _dsa_sample_kernel(pt_ref, *refs, pages):
    page_refs = refs[:pages]
    (q_ref, bias_ref, kv_new_ref, bias_new_ref, o_ref, k_buf, v_buf, m_ref, l_ref, acc_ref) = refs[pages:]
    j = pl.program_id(1)
    rows = q_ref.shape[0]
    kw = KVH_B * HEAD_DIM

    @pl.when(j == 0)
    def _():
        _init_softmax(m_ref, l_ref, acc_ref)

    def attend(k_t, v_t, bias):
        n = k_t.shape[1]
        s = jnp.dot(q_ref[...], k_t, preferred_element_type=F32)
        s = (s.reshape(rows // 8, 8, n) + bias[None]).reshape(rows, n)
        _softmax_step_t(s, m_ref, l_ref, acc_ref, v_t)

    for p in range(pages):
        pg = page_refs[p][...]
        k_buf[:, p * PAGE_SIZE:(p + 1) * PAGE_SIZE] = pg[:kw].astype(BF16)
        v_buf[:, p * PAGE_SIZE:(p + 1) * PAGE_SIZE] = pg[kw:].astype(BF16)
    attend(k_buf[...], v_buf[...], bias_ref[...])

    @pl.when(j == pl.num_programs(1) - 1)
    def _():
        kv_new = kv_new_ref[...]
        attend(kv_new[:kw], kv_new[kw:], bias_new_ref[...])
        o_ref[...] = acc_ref[...] / l_ref[...]


def _dsa_sample(layer, page_table, cache_kv_t, q, bias, kv_new_t, bias_new, *, pages):
    n_seq, n_pages = page_table.shape
    n_steps = n_pages // pages
    n_main = pages * PAGE_SIZE
    rows = q.shape[1]
    n_tail = kv_new_t.shape[2]
    kw = KVH_B * HEAD_DIM
    in_specs = _page_specs(layer, pages, 2 * kw) + [
        _seq_spec((rows, kw)),
        pl.BlockSpec((None, None, 8, n_main), lambda b, j, pt: (b, j, 0, 0)),
        _seq_spec((2 * kw, n_tail)), _seq_spec((8, n_tail))]
    return pl.pallas_call(
        functools.partial(_dsa_sample_kernel, pages=pages),
        out_shape=jax.ShapeDtypeStruct((n_seq, rows, kw), F32),
        grid_spec=pltpu.PrefetchScalarGridSpec(
            num_scalar_prefetch=1, grid=(n_seq, n_steps), in_specs=in_specs,
            out_specs=pl.BlockSpec((None, rows, kw), lambda b, j, pt: (b, 0, 0)),
            scratch_shapes=[pltpu.VMEM((kw, n_main), BF16), pltpu.VMEM((kw, n_main), BF16),
                            pltpu.VMEM((rows, 1), F32), pltpu.VMEM((rows, 1), F32),
                            pltpu.VMEM((rows, kw), F32)]),
        compiler_params=_params("parallel", "arbitrary"),
        name="dsa_sample")(page_table, *([cache_kv_t] * pages), q, bias, kv_new_t, bias_new)


def _router_kernel(x_ref, g_ref, w_ref, b_ref, logit_ref):
    x = x_ref[...]
    ms = jnp.mean(x * x, axis=-1, keepdims=True)
    h = x * lax.rsqrt(ms + NORM_EPS) * g_ref[...]
    logit_ref[...] = jnp.dot(h, w_ref[...], preferred_element_type=F32, precision=lax.Precision.HIGHEST) + b_ref[...]


def _router(x, gain, w_router, b_router, *, tm=512):
    n, d = x.shape
    e = w_router.shape[1]
    return pl.pallas_call(
        _router_kernel,
        out_shape=jax.ShapeDtypeStruct((n, e), F32),
        grid=(n // tm,),
        in_specs=[pl.BlockSpec((tm, d), lambda i: (i, 0)), pl.BlockSpec((1, d), lambda i: (0, 0)),
                  pl.BlockSpec((d, e), lambda i: (0, 0)), pl.BlockSpec((1, e), lambda i: (0, 0))],
        out_specs=pl.BlockSpec((tm, e), lambda i: (i, 0)),
        compiler_params=_params("parallel"),
        name="router")(x, gain.reshape(1, d), w_router, b_router.reshape(1, e))


def _expert_kernel(be_ref, bv_ref, x_ref, g_ref, w1_ref, b1_ref, w2_ref, b2_ref, o_ref):
    i = pl.program_id(0)

    @pl.when(bv_ref[i] != 0)
    def _():
        d_ff = w2_ref.shape[0]
        x = x_ref[...]
        ms = jnp.mean(x * x, axis=-1, keepdims=True)
        h = (x * lax.rsqrt(ms + NORM_EPS) * g_ref[...]).astype(BF16)
        u = jnp.dot(h, w1_ref[...].astype(BF16), preferred_element_type=F32) + b1_ref[...]
        g = jnp.minimum(u[:, :d_ff], SWIGLU_LIMIT)
        up = jnp.clip(u[:, d_ff:], -SWIGLU_LIMIT, SWIGLU_LIMIT)
        a = g * jax.nn.sigmoid(SWIGLU_ALPHA * g) * (up + 1.0)
        o_ref[...] = jnp.dot(a.astype(BF16), w2_ref[...].astype(BF16), preferred_element_type=F32) + b2_ref[...]

    @pl.when(bv_ref[i] == 0)
    def _():
        o_ref[...] = jnp.zeros(o_ref.shape, o_ref.dtype)


def _experts(block_e, block_valid, xb, gain, w1, b1, w2, b2, *, tm):
    n_rows, d = xb.shape
    e, _, two_ff = w1.shape
    d_ff = w2.shape[1]
    return pl.pallas_call(
        _expert_kernel,
        out_shape=jax.ShapeDtypeStruct((n_rows, d), F32),
        grid_spec=pltpu.PrefetchScalarGridSpec(
            num_scalar_prefetch=2, grid=(n_rows // tm,),
            in_specs=[pl.BlockSpec((tm, d), lambda i, be, bv: (i, 0)),
                      pl.BlockSpec((1, d), lambda i, be, bv: (0, 0)),
                      pl.BlockSpec((None, d, two_ff), lambda i, be, bv: (be[i], 0, 0)),
                      pl.BlockSpec((None, 1, two_ff), lambda i, be, bv: (be[i], 0, 0)),
                      pl.BlockSpec((None, d_ff, d), lambda i, be, bv: (be[i], 0, 0)),
                      pl.BlockSpec((None, 1, d), lambda i, be, bv: (be[i], 0, 0))],
            out_specs=pl.BlockSpec((tm, d), lambda i, be, bv: (i, 0))),
        compiler_params=_params("arbitrary"),
        name="experts")(block_e, block_valid, xb, gain.reshape(1, d), w1, b1.reshape(e, 1, two_ff), w2,
                        b2.reshape(e, 1, d))


def _moe(x, gain, w_router, b_router, w1, b1, w2, b2, *, tm_e=256):
    n, d = x.shape
    logits = _router(x, gain, w_router, b_router)
    top_val, top_idx = lax.top_k(logits, TOP_K)
    gate = jax.nn.softmax(top_val, axis=-1)
    flat_e = top_idx.reshape(-1).astype(jnp.int32)
    nk = flat_e.shape[0]
    order = jnp.argsort(flat_e, stable=True).astype(jnp.int32)
    e_sorted = flat_e[order]
    experts = jnp.arange(N_EXPERTS, dtype=jnp.int32)
    counts = jnp.sum((flat_e[None, :] == experts[:, None]).astype(jnp.int32), axis=1)
    start = jnp.cumsum(counts) - counts
    padded = _cdiv(counts, tm_e) * tm_e
    pad_end = jnp.cumsum(padded)
    pad_start = pad_end - padded
    n_blocks = _cdiv(nk, tm_e) + N_EXPERTS
    block_start = jnp.arange(n_blocks, dtype=jnp.int32) * tm_e
    block_e = jnp.minimum(jnp.sum((block_start[:, None] >= pad_end[None, :]).astype(jnp.int32), axis=1), N_EXPERTS - 1)
    block_valid = (block_start < pad_end[-1]).astype(jnp.int32)
    row = jnp.arange(n_blocks * tm_e, dtype=jnp.int32)
    row_e = jnp.repeat(block_e, tm_e)
    offset = row - pad_start[row_e]
    src = jnp.clip(start[row_e] + offset, 0, nk - 1)
    row_tok = jnp.where(offset < counts[row_e], order[src] // TOP_K, 0)
    dest_sorted = pad_start[e_sorted] + jnp.arange(nk, dtype=jnp.int32) - start[e_sorted]
    dest = dest_sorted[jnp.argsort(order)]
    yb = _experts(block_e, block_valid, x[row_tok], gain, w1, b1, w2, b2, tm=tm_e)
    dest_k = dest.reshape(n, TOP_K)
    y = x
    for k in range(TOP_K):
        y = y + yb[dest_k[:, k]] * gate[:, k:k + 1]
    return y


def _rms(x, g, n=None):
    n = x.shape[-1] if n is None else n
    y = x * lax.rsqrt(jnp.sum(x * x, axis=-1, keepdims=True) / n + NORM_EPS)
    return y * g


def _rope_tables(pos, n_rot):
    half = n_rot // 2
    inv_freq = ROPE_THETA ** (-jnp.arange(half, dtype=F32) / half)
    ang = pos.astype(F32)[:, None] * inv_freq[None, :]
    return jnp.cos(ang), jnp.sin(ang)


def _rope(x, cos, sin, n_rot):
    d = x.shape[-1]
    half = n_rot // 2
    cos = cos[:, None, :]
    sin = sin[:, None, :]
    x1 = x[..., d - n_rot:d - half]
    x2 = x[..., d - half:]
    return jnp.concatenate([x[..., :d - n_rot], x1 * cos - x2 * sin, x2 * cos + x1 * sin], axis=-1)


def _regroup_w_in(w_in):
    cuts = np.cumsum([0, Q_LORA, KV_LORA, ROPE_A, H_B * HEAD_DIM, KVH_B * HEAD_DIM, KVH_B * HEAD_DIM,
                      H_I * D_I, D_I, H_I, 2 * w_in.shape[0]])
    w_cq, w_ckv, w_kpe, w_qb, w_kb, w_vb, w_iq, w_ik, w_iw, w_g = [w_in[:, a:b] for a, b in zip(cuts[:-1], cuts[1:])]
    misc_pad = jnp.zeros((w_in.shape[0], COL_GATE - COL_MISC - ROPE_A - D_I - H_I), w_in.dtype)
    return jnp.concatenate([w_cq, w_ckv, w_qb, w_kb, w_vb, w_iq, w_kpe, w_ik, w_iw, misc_pad, w_g], axis=1).astype(BF16)


def _layer(l, xp, xs, page_table, caches, wts, *, t_real):
    (cache_mla, cache_dsa_kv, cache_dsa_idx) = caches
    n_b, t_pad, d = xp.shape
    n_seq, n_ds, _ = xs.shape
    past = page_table.shape[1] * PAGE_SIZE
    n_p = n_b * t_pad
    n_s = n_seq * n_ds
    n_sel_p = min(MAX_SEL, (t_real - N_META) // 4)
    n_sel_s = min(MAX_SEL, (past + n_ds) // 4)

    x_all = jnp.concatenate([xp.reshape(n_p, d), xs.reshape(n_s, d)], axis=0)
    n_all = _round_up(n_p + n_s, 1024)
    x_all = jnp.pad(x_all, ((0, n_all - n_p - n_s), (0, 0)))
    pos = jnp.concatenate([jnp.tile(jnp.arange(t_pad), n_b), jnp.tile(past + jnp.arange(n_ds), n_seq),
                           jnp.zeros((n_all - n_p - n_s,), jnp.int32)])
    z = _matmul(x_all, _regroup_w_in(wts["w_in"][l]), gain=wts["g_attn_norm"][l], tm=1024, tn=1024, name="proj_in")

    cos_a, sin_a = _rope_tables(pos, ROPE_A)
    cos_b, sin_b = _rope_tables(pos, HEAD_DIM)
    cos_i, sin_i = _rope_tables(pos, D_I_ROPE)

    w_uq = jnp.pad(wts["w_uq"][l], ((0, 0), (0, 0), (0, LANE - DQK_A))).reshape(Q_LORA, H_A * LANE).astype(BF16)
    q_a = _matmul(z, w_uq, gain=wts["g_cq"][l], col_block=COL_CQ // Q_LORA, k=Q_LORA, name="q_up")
    q_a = q_a.reshape(n_all, H_A, LANE)[..., :DQK_A]
    q_a = _rope(_rms(q_a, wts["g_qa"][l]), cos_a, sin_a, ROPE_A) * (DQK_A ** -0.5)

    c_kv = _rms(z[:, COL_CKV:COL_CKV + KV_LORA], wts["g_ckv"][l])
    k_pe = z[:, COL_MISC:COL_MISC + ROPE_A]
    q_b = _rope(_rms(z[:, COL_QB:COL_QB + H_B * HEAD_DIM].reshape(n_all, H_B, HEAD_DIM), wts["g_qb"][l]),
                cos_b, sin_b, HEAD_DIM) * (HEAD_DIM ** -0.5)
    k_b = _rope(_rms(z[:, COL_KB:COL_KB + KVH_B * HEAD_DIM].reshape(n_all, KVH_B, HEAD_DIM), wts["g_kb"][l]),
                cos_b, sin_b, HEAD_DIM)
    v_b = z[:, COL_VB:COL_VB + KVH_B * HEAD_DIM].reshape(n_all, KVH_B, HEAD_DIM)
    i_q = _rope(z[:, COL_IQ:COL_IQ + H_I * D_I].reshape(n_all, H_I, D_I), cos_i, sin_i, D_I_ROPE)
    i_k = _rope(_rms(z[:, COL_MISC + ROPE_A:COL_MISC + ROPE_A + D_I], wts["g_ik"][l])[:, None, :],
                cos_i, sin_i, D_I_ROPE)[:, 0, :]
    i_w = z[:, COL_MISC + ROPE_A + D_I:COL_MISC + ROPE_A + D_I + H_I] * ((H_I ** -0.5) * (D_I ** -0.5))

    mla_rows = jnp.concatenate([c_kv, k_pe], axis=-1)
    kv_rows = jnp.stack([k_b, v_b], axis=1)
    half_sel = (jnp.arange(H_B)[:, None] // GROUP_B == jnp.arange(KVH_B)[None, :]).astype(F32)
    q_b_pad = (q_b[:, :, None, :] * half_sel[None, :, :, None]).reshape(n_all, H_B, KVH_B * HEAD_DIM)

    tq = 128
    nq = t_pad // tq
    n_kc = max(dd for dd in range(1, 5) if nq % dd == 0)
    tk = n_kc * tq
    w_kv = jnp.concatenate([wts["w_uk"][l].reshape(KV_LORA, H_A * NOPE_A), wts["w_uv"][l].reshape(KV_LORA, H_A * V_A)],
                           axis=1).astype(BF16)
    kv_a = _matmul(z, w_kv, gain=wts["g_ckv"][l], col_block=COL_CKV // KV_LORA, k=KV_LORA, name="kv_up")
    k_nope = kv_a[:n_p, :H_A * NOPE_A].reshape(n_p, H_A, NOPE_A)
    v_a = kv_a[:n_p, H_A * NOPE_A:]
    k_a = jnp.concatenate([k_nope, jnp.broadcast_to(k_pe[:n_p, None, :], (n_p, H_A, ROPE_A))], axis=-1)
    k_a = _rope(_rms(k_a, wts["g_ka"][l]), cos_a[:n_p], sin_a[:n_p], ROPE_A)
    pad_h = lambda a: jnp.pad(a, ((0, 0), (0, 0), (0, LANE - DQK_A))).astype(BF16)
    o_a_p = _mla_prompt(pad_h(q_a[:n_p]).reshape(n_b, t_pad, H_A * LANE), pad_h(k_a).reshape(n_b, t_pad, H_A * LANE),
                        v_a.astype(BF16).reshape(n_b, t_pad, H_A * V_A), tq=tk)

    blocked = lambda a: a.reshape((n_b, nq, tq) + a.shape[1:]).swapaxes(2, 3)
    o_b_p = _dsa_prompt(blocked(i_q[:n_p].astype(BF16)), i_w[:n_p].reshape(n_b, t_pad, H_I),
                        blocked(q_b_pad[:n_p].astype(BF16)), i_k[:n_p].astype(BF16).reshape(n_b, t_pad, D_I),
                        k_b[:n_p].astype(BF16).reshape(n_b, t_pad, KVH_B * HEAD_DIM),
                        v_b[:n_p].astype(BF16).reshape(n_b, t_pad, KVH_B * HEAD_DIM), tq=tq, tk=tk, n_sel=n_sel_p)
    o_b_p = o_b_p.reshape(n_b, nq, KVH_B, GROUP_B, tq, KVH_B, HEAD_DIM)
    o_b_p = jnp.stack([o_b_p[:, :, g, :, :, g, :] for g in range(KVH_B)], axis=2)
    o_b_p = o_b_p.reshape(n_b, nq, H_B, tq, HEAD_DIM).swapaxes(2, 3).reshape(n_p, H_B * HEAD_DIM)

    sl = slice(n_p, n_p + n_s)
    pages = max(dd for dd in range(1, MAX_PAGES_PER_STEP + 1) if page_table.shape[1] % dd == 0)
    assert 8 % n_ds == 0
    n_tail = LANE
    tail_t = lambda a: jnp.pad(a.reshape(n_seq, n_ds, a.shape[-1]), ((0, 0), (0, n_tail - n_ds), (0, 0))).swapaxes(1, 2)
    cache_mla_t = jnp.swapaxes(cache_mla, 2, 3)
    cache_idx_t = jnp.swapaxes(cache_dsa_idx, 2, 3)
    cache_kv_t = jnp.moveaxis(cache_dsa_kv, 2, -1).reshape(cache_dsa_kv.shape[:2] + (2 * KVH_B * HEAD_DIM, PAGE_SIZE))
    g_ka = wts["g_ka"][l]
    pos_k = jnp.arange(past + n_tail)
    cos_k, sin_k = _rope_tables(pos_k, ROPE_A)
    g1, g2 = g_ka[NOPE_A:NOPE_A + ROPE_A // 2], g_ka[NOPE_A + ROPE_A // 2:]
    cg_t = jnp.concatenate([cos_k * g1, cos_k * g2], axis=1).T
    sg_t = jnp.concatenate([sin_k * g1, -sin_k * g2], axis=1).T
    qa_s = q_a[sl]
    rows_a = n_ds * H_A
    eye_h = jnp.eye(H_A, dtype=F32)
    qn_s = qa_s[..., :NOPE_A] * g_ka[:NOPE_A]
    qbd = (qn_s[:, :, None, :] * eye_h[None, :, :, None]).reshape(n_seq, rows_a, H_A * NOPE_A)
    q_r = qa_s[:, :, NOPE_A:].reshape(n_seq, rows_a, ROPE_A)
    q_r_swap = jnp.concatenate([q_r[..., ROPE_A // 2:], q_r[..., :ROPE_A // 2]], axis=-1)
    wt = wts["w_uk"][l].reshape(KV_LORA, H_A * NOPE_A).T
    o_lat = _mla_sample(l, page_table, cache_mla_t, wt.astype(BF16), qbd.astype(BF16), q_r.astype(BF16),
                        q_r_swap.astype(BF16), cg_t[:, :past], sg_t[:, :past], tail_t(mla_rows[sl]),
                        cg_t[:, past:], sg_t[:, past:], pages=pages, n_new=n_ds)
    o_a_s = jnp.einsum("bqhr,rhv->bqhv", o_lat.reshape(n_seq, n_ds, H_A, KV_LORA), wts["w_uv"][l])
    o_a_s = o_a_s.reshape(n_s, H_A * V_A)

    slot = jnp.arange(8) % n_ds
    iq_s = i_q[sl].reshape(n_seq, n_ds, H_I, D_I)[:, slot].swapaxes(1, 2).reshape(n_seq, H_I * 8, D_I)
    iw_s = i_w[sl].reshape(n_seq, n_ds, H_I)[:, slot].swapaxes(1, 2).reshape(n_seq, H_I * 8, 1)
    bias, bias_new = _dsa_select(l, page_table, cache_idx_t, iq_s.astype(BF16), iw_s,
                                 tail_t(i_k[sl]).astype(BF16), pages=pages, n_new=n_ds, n_sel=n_sel_s)
    qb_s = q_b_pad[sl].reshape(n_seq, n_ds, H_B, KVH_B * HEAD_DIM).swapaxes(1, 2).reshape(n_seq, H_B * n_ds, -1)
    kv_new_t = tail_t(kv_rows[sl].reshape(n_s, 2 * KVH_B * HEAD_DIM)).astype(BF16)
    o_b_s = _dsa_sample(l, page_table, cache_kv_t, qb_s.astype(BF16), bias, kv_new_t, bias_new, pages=pages)
    o_b_s = o_b_s.reshape(n_seq, KVH_B, GROUP_B, n_ds, KVH_B, HEAD_DIM)
    o_b_s = jnp.stack([o_b_s[:, g, :, :, g, :] for g in range(KVH_B)], axis=1)
    o_b_s = o_b_s.reshape(n_seq, H_B, n_ds, HEAD_DIM).swapaxes(1, 2).reshape(n_s, H_B * HEAD_DIM)

    zeros_tail = jnp.zeros((n_all - n_p - n_s, H_A * V_A), BF16)
    o_a = jnp.concatenate([o_a_p.reshape(n_p, H_A * V_A), o_a_s.astype(BF16), zeros_tail], axis=0)
    o_b = jnp.concatenate([o_b_p, o_b_s.astype(BF16), zeros_tail], axis=0)
    assert COL_GATE % (2 * d) == 0
    x_all = _merge(x_all, o_a, o_b, z, wts["w_pa"][l].astype(BF16), wts["w_pb"][l].astype(BF16),
                   wts["w_o"][l].astype(BF16))
    x_all = _moe(x_all, wts["g_ffn_norm"][l], wts["w_router"][l], wts["b_router"][l], wts["w_mlp1"][l],
                 wts["b_mlp1"][l], wts["w_mlp2"][l], wts["b_mlp2"][l])

    new_p = (mla_rows[:n_p].reshape(n_b, t_pad, -1)[:, :t_real],
             kv_rows[:n_p].reshape(n_b, t_pad, 2, KVH_B, HEAD_DIM)[:, :t_real],
             i_k[:n_p].reshape(n_b, t_pad, D_I)[:, :t_real])
    new_s = (mla_rows[sl].reshape(n_seq, n_ds, -1), kv_rows[sl].reshape(n_seq, n_ds, 2, KVH_B, HEAD_DIM),
             i_k[sl].reshape(n_seq, n_ds, D_I))
    return x_all[:n_p].reshape(n_b, t_pad, d), x_all[sl].reshape(n_seq, n_ds, d), new_p, new_s


def kernel(x_prompt, x_sample, cache_mla, cache_dsa_kv, cache_dsa_idx, page_table, meta_tokens, g_attn_norm, w_in,
           g_cq, w_uq, g_qa, g_ckv, w_uk, w_uv, g_ka, g_qb, g_kb, g_ik, w_pa, w_pb, w_o, g_ffn_norm, w_router,
           b_router, w_mlp1, b_mlp1, w_mlp2, b_mlp2):
    n_b, n_s, d = x_prompt.shape
    t_real = n_s + N_META
    t_pad = _round_up(t_real, LANE)
    wts = dict(g_attn_norm=g_attn_norm, w_in=w_in, g_cq=g_cq, w_uq=w_uq, g_qa=g_qa, g_ckv=g_ckv, w_uk=w_uk, w_uv=w_uv,
               g_ka=g_ka, g_qb=g_qb, g_kb=g_kb, g_ik=g_ik, w_pa=w_pa, w_pb=w_pb, w_o=w_o, g_ffn_norm=g_ffn_norm,
               w_router=w_router, b_router=b_router, w_mlp1=w_mlp1, b_mlp1=b_mlp1, w_mlp2=w_mlp2, b_mlp2=b_mlp2)
    meta = jnp.broadcast_to(meta_tokens[None].astype(x_prompt.dtype), (n_b, N_META, d))
    xp = jnp.concatenate([meta, x_prompt, jnp.zeros((n_b, t_pad - t_real, d), x_prompt.dtype)], axis=1)
    xs = x_sample
    outs_p, outs_s = [], []
    for l in range(w_in.shape[0]):
        xp, xs, new_p, new_s = _layer(l, xp, xs, page_table, (cache_mla, cache_dsa_kv, cache_dsa_idx), wts,
                                      t_real=t_real)
        outs_p.append(new_p)
        outs_s.append(new_s)
    stack = lambda outs, i: jnp.stack([o[i] for o in outs], axis=0)
    return (xp[:, N_META:t_real], xs, stack(outs_p, 0), stack(outs_p, 1), stack(outs_p, 2),
            stack(outs_s, 0), stack(outs_s, 1), stack(outs_s, 2))
```

```python
import functools

import jax
import jax.numpy as jnp
import numpy as np
from jax import lax
from jax.experimental import pallas as pl
from jax.experimental.pallas import tpu as pltpu

N_META = 16
HEAD_DIM = 64
H_A = 8
NOPE_A = 64
ROPE_A = 32
DQK_A = NOPE_A + ROPE_A
V_A = 64
Q_LORA = 512
KV_LORA = 256
H_B = 8
KVH_B = 2
GROUP_B = H_B // KVH_B
H_I = 8
D_I = 32
D_I_ROPE = 16
MAX_SEL = 256
N_EXPERTS = 32
TOP_K = 4
SWIGLU_LIMIT = 7.0
SWIGLU_ALPHA = 1.702
ROPE_THETA = 10000.0
NORM_EPS = 1e-6
PAGE_SIZE = 128

LANE = 128
MASK_VALUE = -1e30
INT_MIN = -(2 ** 31)
VMEM_LIMIT = 56 * 1024 * 1024
MAX_PAGES_PER_STEP = 16

COL_CQ = 0
COL_CKV = 512
COL_QB = 768
COL_KB = 1280
COL_VB = 1408
COL_IQ = 1536
COL_MISC = 1792
COL_GATE = 2048
N_PROJ = 4096

F32 = jnp.float32
BF16 = jnp.bfloat16
NT_DIMS = (((1,), (1,)), ((), ()))


def _cdiv(a, b):
    return (a + b - 1) // b


def _round_up(a, b):
    return _cdiv(a, b) * b


def _params(*sem):
    return pltpu.CompilerParams(dimension_semantics=sem, vmem_limit_bytes=VMEM_LIMIT)


def _mm_kernel(x_ref, w_ref, o_ref):
    o_ref[...] = jnp.dot(x_ref[...].astype(BF16), w_ref[...], preferred_element_type=F32).astype(o_ref.dtype)


def _mm_norm_kernel(x_ref, g_ref, w_ref, o_ref, xn_ref):
    @pl.when(pl.program_id(1) == 0)
    def _():
        x = x_ref[...].astype(F32)
        ms = jnp.mean(x * x, axis=-1, keepdims=True)
        xn_ref[...] = (x * lax.rsqrt(ms + NORM_EPS) * g_ref[...]).astype(BF16)

    o_ref[...] = jnp.dot(xn_ref[...], w_ref[...], preferred_element_type=F32).astype(o_ref.dtype)


def _matmul(x, w, *, gain=None, col_block=0, k=None, tm=512, tn=512, out_dtype=F32, name="mm"):
    m = x.shape[0]
    k = x.shape[1] if k is None else k
    n = w.shape[1]
    tn = min(tn, n)
    assert m % tm == 0 and n % tn == 0 and w.shape[0] == k
    grid = (m // tm, n // tn)
    x_spec = pl.BlockSpec((tm, k), lambda i, j: (i, col_block))
    w_spec = pl.BlockSpec((k, tn), lambda i, j: (0, j))
    o_spec = pl.BlockSpec((tm, tn), lambda i, j: (i, j))
    out_shape = jax.ShapeDtypeStruct((m, n), out_dtype)
    if gain is None:
        return pl.pallas_call(_mm_kernel, out_shape=out_shape, grid=grid, in_specs=[x_spec, w_spec],
                              out_specs=o_spec, compiler_params=_params("parallel", "arbitrary"), name=name)(x, w)
    g_spec = pl.BlockSpec((1, k), lambda i, j: (0, 0))
    return pl.pallas_call(_mm_norm_kernel, out_shape=out_shape, grid=grid, in_specs=[x_spec, g_spec, w_spec],
                          out_specs=o_spec, scratch_shapes=[pltpu.VMEM((tm, k), BF16)],
                          compiler_params=_params("parallel", "arbitrary"), name=name)(
                              x, gain.reshape(1, k).astype(F32), w)


def _lane_ids(tm):
    return lax.broadcasted_iota(jnp.int32, (tm, LANE), 1)


def _rotate_pairs(t, cos_t, sin_t, lane, group, split, half):
    first = (lane & (group - 1)) < split
    partner = jnp.where(first, pltpu.roll(t, LANE - half, 1), pltpu.roll(t, half, 1))
    return t * cos_t + partner * sin_t


def _head_tiles(x, fn):
    tiles = [fn(x[:, i * LANE:(i + 1) * LANE]) for i in range(x.shape[1] // LANE)]
    return tiles[0] if len(tiles) == 1 else jnp.concatenate(tiles, axis=1)


def _mla_head_post(t, gain, cos_t, sin_t, lane):
    ms = jnp.sum(t * t, axis=1, keepdims=True) * (1.0 / DQK_A)
    return _rotate_pairs(t * lax.rsqrt(ms + NORM_EPS) * gain, cos_t, sin_t, lane, LANE, NOPE_A + ROPE_A // 2,
                         ROPE_A // 2)


def _q_up_kernel(z_ref, gz_ref, w_ref, gq_ref, cos_ref, sin_ref, o_ref):
    x = z_ref[...]
    xn = (x * lax.rsqrt(jnp.mean(x * x, axis=-1, keepdims=True) + NORM_EPS) * gz_ref[...]).astype(BF16)
    q = jnp.dot(xn, w_ref[...], preferred_element_type=F32)
    lane = _lane_ids(x.shape[0])
    post = lambda t: _mla_head_post(t, gq_ref[...], cos_ref[...], sin_ref[...], lane)
    o_ref[...] = _head_tiles(q, post).astype(o_ref.dtype)


def _kv_up_kernel(z_ref, gz_ref, misc_ref, w_ref, place_ref, gk_ref, cos_ref, sin_ref, k_ref, v_ref):
    x = z_ref[...]
    c = (x * lax.rsqrt(jnp.mean(x * x, axis=-1, keepdims=True) + NORM_EPS) * gz_ref[...]).astype(BF16)
    kv = jnp.dot(c, w_ref[...], preferred_element_type=F32)
    n_k = H_A * LANE
    misc = misc_ref[...]
    hi = misc.astype(BF16)
    lo = (misc - hi.astype(F32)).astype(BF16)
    k = kv[:, :n_k] + (jnp.dot(hi, place_ref[...], preferred_element_type=F32)
                       + jnp.dot(lo, place_ref[...], preferred_element_type=F32))
    lane = _lane_ids(x.shape[0])
    post = lambda t: _mla_head_post(t, gk_ref[...], cos_ref[...], sin_ref[...], lane)
    k_ref[...] = _head_tiles(k, post).astype(k_ref.dtype)
    v_ref[...] = kv[:, n_k:].astype(v_ref.dtype)


def _row_call(kernel_fn, n, tm, ins, outs, name):
    def spec(width, col_block, shape):
        if width is None:
            return pl.BlockSpec(shape, lambda i: tuple(0 for _ in shape))
        return pl.BlockSpec((tm, width), lambda i: (i, col_block))
    in_specs = [spec(w, cb, a.shape) for a, w, cb in ins]
    out_specs = tuple(pl.BlockSpec((tm, w), lambda i: (i, 0)) for w, _ in outs)
    out_shape = tuple(jax.ShapeDtypeStruct((n, w), dt) for w, dt in outs)
    return pl.pallas_call(kernel_fn, out_shape=out_shape, grid=(n // tm,), in_specs=in_specs, out_specs=out_specs,
                          compiler_params=_params("parallel"), name=name)(*[a for a, _, _ in ins])


def _post_z_kernel(z_ref, gckv_ref, gqb_ref, gkb_ref, gik_ref, cb_ref, sb_ref, ci_ref, si_ref, expand_ref,
                   mla_ref, kv_ref, ik_ref, iw_ref, qb_ref, k2_ref, v2_ref, iq_ref, ikb_ref):
    z = z_ref[...]
    tm = z.shape[0]
    lane = _lane_ids(tm)
    cb, sb, ci, si = cb_ref[...], sb_ref[...], ci_ref[...], si_ref[...]
    misc = z[:, COL_MISC:COL_MISC + LANE]

    x = z[:, COL_CKV:COL_CKV + KV_LORA]
    mla_ref[:, 0:KV_LORA] = x * lax.rsqrt(jnp.mean(x * x, axis=-1, keepdims=True) + NORM_EPS) * gckv_ref[...]
    mla_ref[:, KV_LORA:] = misc[:, 0:ROPE_A]

    low = lane < HEAD_DIM

    def head_pair(t, gain):
        t2 = t * t
        ms = jnp.where(low, jnp.sum(jnp.where(low, t2, 0.0), axis=1, keepdims=True),
                       jnp.sum(jnp.where(low, 0.0, t2), axis=1, keepdims=True)) * (1.0 / HEAD_DIM)
        return _rotate_pairs(t * lax.rsqrt(ms + NORM_EPS) * gain, cb, sb, lane, HEAD_DIM, HEAD_DIM // 2, HEAD_DIM // 2)

    qb = _head_tiles(z[:, COL_QB:COL_QB + H_B * HEAD_DIM], lambda t: head_pair(t, gqb_ref[...])).astype(BF16)
    qb_ref[...] = jnp.dot(qb, expand_ref[...], preferred_element_type=F32).astype(qb_ref.dtype)

    kb = head_pair(z[:, COL_KB:COL_KB + LANE], gkb_ref[...])
    vb = z[:, COL_VB:COL_VB + LANE]
    kv_ref[:, 0:LANE] = kb
    kv_ref[:, LANE:] = vb
    k2_ref[...] = kb.astype(k2_ref.dtype)
    v2_ref[...] = vb.astype(v2_ref.dtype)

    rot_i = lambda t: _rotate_pairs(t, ci, si, lane, D_I, D_I - D_I_ROPE // 2, D_I_ROPE // 2)
    iq_ref[...] = _head_tiles(z[:, COL_IQ:COL_IQ + H_I * D_I], rot_i).astype(iq_ref.dtype)

    in_ik = (lane >= ROPE_A) & (lane < ROPE_A + D_I)
    ms = jnp.sum(jnp.where(in_ik, misc * misc, 0.0), axis=1, keepdims=True) * (1.0 / D_I)
    ik = pltpu.roll(rot_i(misc * lax.rsqrt(ms + NORM_EPS) * gik_ref[...]), LANE - ROPE_A, 1)
    ik_ref[...] = ik[:, 0:D_I]
    ikb_ref[...] = ik[:, 0:D_I].astype(ikb_ref.dtype)
    iw_ref[...] = pltpu.roll(misc, LANE - ROPE_A - D_I, 1)[:, 0:H_I] * ((H_I ** -0.5) * (D_I ** -0.5))


def _merge_kernel(x_ref, oa_ref, ob_ref, zg_ref, wpa_ref, wpb_ref, wo_ref, o_ref):
    d = x_ref.shape[1]
    y_a = jnp.dot(oa_ref[...], wpa_ref[...], preferred_element_type=F32)
    y_b = jnp.dot(ob_ref[...], wpb_ref[...], preferred_element_type=F32)
    zg = zg_ref[...]
    merged = jax.nn.sigmoid(zg[:, :d]) * y_a + jax.nn.sigmoid(zg[:, d:]) * y_b
    o_ref[...] = x_ref[...] + jnp.dot(merged.astype(BF16), wo_ref[...], preferred_element_type=F32)


def _merge(x, o_a, o_b, z, w_pa, w_pb, w_o, *, tm=512):
    n, d = x.shape
    ka, kb = o_a.shape[1], o_b.shape[1]
    const = lambda shape: pl.BlockSpec(shape, lambda i: (0, 0))
    return pl.pallas_call(
        _merge_kernel,
        out_shape=jax.ShapeDtypeStruct((n, d), F32),
        grid=(n // tm,),
        in_specs=[pl.BlockSpec((tm, d), lambda i: (i, 0)), pl.BlockSpec((tm, ka), lambda i: (i, 0)),
                  pl.BlockSpec((tm, kb), lambda i: (i, 0)), pl.BlockSpec((tm, 2 * d), lambda i: (i, COL_GATE // (2 * d))),
                  const((ka, d)), const((kb, d)), const((d, d))],
        out_specs=pl.BlockSpec((tm, d), lambda i: (i, 0)),
        compiler_params=_params("parallel"),
        name="merge")(x, o_a, o_b, z, w_pa, w_pb, w_o)


def _softmax_step(s, m_prev, l_prev, acc_prev, v_blk):
    m_new = jnp.maximum(m_prev, jnp.max(s, axis=1, keepdims=True))
    alpha = jnp.exp(m_prev - m_new)
    p = jnp.exp(s - m_new)
    l_new = alpha * l_prev + jnp.sum(p, axis=1, keepdims=True)
    acc_new = alpha * acc_prev + jnp.dot(p.astype(BF16), v_blk, preferred_element_type=F32)
    return m_new, l_new, acc_new


def _sortable_key(x):
    b = lax.bitcast_convert_type(x + 0.0, jnp.int32)
    return jnp.where(b < 0, b ^ jnp.int32(0x7FFFFFFF), b)


KEY_NEG_INF = INT_MIN + 0x7FFFFF


def _has_excess_ties(n_ge, thr, n_sel):
    return jnp.max(jnp.where(n_ge > n_sel, jnp.where(thr > KEY_NEG_INF, 1, 0), 0)) > 0


def _kth_largest_key(count_ge, n_sel, rows):
    def step(s, t):
        bit = lax.shift_left(jnp.int32(1), jnp.int32(31) - s)
        cand = t + bit
        return jnp.where(count_ge(cand) >= n_sel, cand, t)

    return lax.fori_loop(0, 32, step, jnp.full((rows, 1), INT_MIN, jnp.int32))


MLA_HEADS_PER_STEP = 4


def _mla_prompt_kernel(q_ref, k_ref, v_ref, o_ref, *, tq):
    i = pl.program_id(2)
    nh = MLA_HEADS_PER_STEP
    q = q_ref[...]
    causal = lax.broadcasted_iota(jnp.int32, (tq, tq), 1) <= lax.broadcasted_iota(jnp.int32, (tq, tq), 0)

    def chunk(c, carry, diagonal):
        m, l, acc = carry
        start = pl.multiple_of(c * tq, tq)
        k_blk = k_ref[pl.ds(start, tq), :]
        v_blk = v_ref[pl.ds(start, tq), :]
        s = jnp.concatenate([lax.dot_general(q[:, h * LANE:(h + 1) * LANE], k_blk[:, h * LANE:(h + 1) * LANE],
                                             NT_DIMS, preferred_element_type=F32) for h in range(nh)], axis=0)
        if diagonal:
            s = jnp.where(causal[None], s.reshape(nh, tq, tq), MASK_VALUE).reshape(nh * tq, tq)
        m_new = jnp.maximum(m, jnp.max(s, axis=1, keepdims=True))
        alpha = jnp.exp(m - m_new)
        p = jnp.exp(s - m_new)
        l_new = alpha * l + jnp.sum(p, axis=1, keepdims=True)
        p = p.astype(BF16)
        pv = jnp.concatenate([jnp.dot(p[h * tq:(h + 1) * tq], v_blk[:, (h // 2) * LANE:(h // 2 + 1) * LANE],
                                      preferred_element_type=F32) for h in range(nh)], axis=0)
        return m_new, l_new, alpha * acc + pv

    init = (jnp.full((nh * tq, 1), MASK_VALUE, F32), jnp.zeros((nh * tq, 1), F32), jnp.zeros((nh * tq, LANE), F32))
    carry = lax.fori_loop(0, i, functools.partial(chunk, diagonal=False), init)
    _, l, acc = chunk(i, carry, True)
    o = acc / l
    lane = lax.broadcasted_iota(jnp.int32, (tq, LANE), 1)
    pairs = [jnp.where(lane < V_A, o[(2 * g) * tq:(2 * g + 1) * tq], o[(2 * g + 1) * tq:(2 * g + 2) * tq])
             for g in range(nh // 2)]
    o_ref[...] = jnp.concatenate(pairs, axis=1).astype(o_ref.dtype)


def _mla_prompt(q, k, v, *, tq):
    b, t, _ = q.shape
    nh = MLA_HEADS_PER_STEP
    grid = (b, H_A // nh, t // tq)
    return pl.pallas_call(
        functools.partial(_mla_prompt_kernel, tq=tq),
        out_shape=jax.ShapeDtypeStruct((b, t, H_A * V_A), BF16),
        grid=grid,
        in_specs=[pl.BlockSpec((None, tq, nh * LANE), lambda bi, g, i: (bi, i, g)),
                  pl.BlockSpec((None, t, nh * LANE), lambda bi, g, i: (bi, 0, g)),
                  pl.BlockSpec((None, t, nh * V_A), lambda bi, g, i: (bi, 0, g))],
        out_specs=pl.BlockSpec((None, tq, nh * V_A), lambda bi, g, i: (bi, i, g)),
        compiler_params=_params("parallel", "parallel", "arbitrary"),
        name="mla_prompt")(q, k, v)


def _select_bias(key, thr, need, carry, valid, tri_ref):
    w = min(tri_ref.shape[0], key.shape[1])
    tri = tri_ref[0:w, 0:w]
    parts = []
    for t in range(key.shape[1] // w):
        k_t = key[:, t * w:(t + 1) * w]
        eq = k_t == thr
        rank = jnp.dot(jnp.where(eq, 1.0, 0.0).astype(BF16), tri, preferred_element_type=F32) + carry
        carry = jnp.max(rank, axis=1, keepdims=True)
        tie = jnp.where(rank <= need, 0.0, MASK_VALUE)
        parts.append(jnp.where(k_t > thr, 0.0, jnp.where(eq, tie, MASK_VALUE)))
    bias = parts[0] if len(parts) == 1 else jnp.concatenate(parts, axis=1)
    if valid is not None:
        bias = jnp.where(valid, bias, MASK_VALUE)
    return bias, carry


def _dsa_prompt_kernel(iq_ref, w_ref, q_ref, ik_ref, k_ref, v_ref, tri_ref, o_ref, key_ref, bias_ref, wb_ref,
                       *, tq, tk, n_sel):
    i = pl.program_id(1)
    n_c = ((i + 1) * tq + tk - 1) // tk
    q_pos = i * tq + lax.broadcasted_iota(jnp.int32, (tq, tk), 0)
    lane_pos = lax.broadcasted_iota(jnp.int32, (tq, tk), 1)

    w = w_ref[...]
    for h in range(H_I):
        wb_ref[h] = jnp.broadcast_to(w[:, h:h + 1], (tq, LANE))
    iq = iq_ref[...].reshape(H_I * tq, D_I)

    def score_chunk(c, _):
        start = pl.multiple_of(c * tk, tk)
        logit = lax.dot_general(iq, ik_ref[pl.ds(start, tk), :], NT_DIMS, preferred_element_type=F32)
        cols = []
        for t in range(tk // LANE):
            acc = jnp.zeros((tq, LANE), F32)
            for h in range(H_I):
                acc = acc + jnp.maximum(logit[h * tq:(h + 1) * tq, t * LANE:(t + 1) * LANE], 0.0) * wb_ref[h]
            cols.append(acc)
        score = jnp.concatenate(cols, axis=1)
        score = jnp.where(start + lane_pos <= q_pos, score, -jnp.inf)
        key_ref[c] = _sortable_key(score)
        return 0

    lax.fori_loop(0, n_c, score_chunk, 0)

    def count_ge(cand):
        cand_b = jnp.broadcast_to(cand, (tq, LANE))

        def body(c, cnt):
            key = key_ref[c]
            for t in range(tk // LANE):
                cnt = cnt + jnp.where(key[:, t * LANE:(t + 1) * LANE] >= cand_b, 1, 0)
            return cnt

        cnt = lax.fori_loop(0, n_c, body, jnp.zeros((tq, LANE), jnp.int32))
        return jnp.sum(cnt, axis=1, keepdims=True)

    thr = _kth_largest_key(count_ge, n_sel, tq)

    def bias_by_threshold():
        thr_eff = jnp.maximum(thr, KEY_NEG_INF + 1)

        def body(c, _):
            bias_ref[c] = jnp.where(key_ref[c] >= thr_eff, 0.0, MASK_VALUE)
            return 0

        lax.fori_loop(0, n_c, body, 0)

    def bias_with_ties():
        need = (n_sel - count_ge(thr + 1)).astype(F32)

        def body(c, carry):
            start = pl.multiple_of(c * tk, tk)
            bias, carry = _select_bias(key_ref[c], thr, need, carry, start + lane_pos <= q_pos, tri_ref)
            bias_ref[c] = bias
            return carry

        lax.fori_loop(0, n_c, body, jnp.zeros((tq, 1), F32))

    lax.cond(_has_excess_ties(count_ge(thr), thr, n_sel), bias_with_ties, bias_by_threshold)

    rows = H_B * tq
    qa = q_ref[...].reshape(rows, LANE)

    def attend(c, carry):
        start = pl.multiple_of(c * tk, tk)
        s = lax.dot_general(qa, k_ref[pl.ds(start, tk), :], NT_DIMS, preferred_element_type=F32)
        s = (s.reshape(H_B, tq, tk) + bias_ref[c][None]).reshape(rows, tk)
        return _softmax_step(s, *carry, v_ref[pl.ds(start, tk), :])

    m, l, acc = lax.fori_loop(0, n_c, attend, (jnp.full((rows, 1), MASK_VALUE, F32),
                                               jnp.zeros((rows, 1), F32), jnp.zeros((rows, LANE), F32)))
    o_ref[...] = (acc / l).reshape(H_B, tq, LANE).astype(o_ref.dtype)


def _dsa_prompt(iq, iw, q, ik, k, v, *, tq, tk, n_sel):
    b, nq = iq.shape[:2]
    t = ik.shape[1]
    n_chunks = t // tk
    tri_w = 2 * LANE if tk % (2 * LANE) == 0 else LANE
    tri = jnp.asarray(np.arange(tri_w)[:, None] <= np.arange(tri_w)[None, :], BF16)
    return pl.pallas_call(
        functools.partial(_dsa_prompt_kernel, tq=tq, tk=tk, n_sel=n_sel),
        out_shape=jax.ShapeDtypeStruct((b, nq, H_B, tq, LANE), BF16),
        grid=(b, nq),
        in_specs=[pl.BlockSpec((None, None, H_I, tq, D_I), lambda bi, i: (bi, i, 0, 0, 0)),
                  pl.BlockSpec((None, tq, H_I), lambda bi, i: (bi, i, 0)),
                  pl.BlockSpec((None, None, H_B, tq, LANE), lambda bi, i: (bi, i, 0, 0, 0)),
                  pl.BlockSpec((None, t, D_I), lambda bi, i: (bi, 0, 0)),
                  pl.BlockSpec((None, t, LANE), lambda bi, i: (bi, 0, 0)),
                  pl.BlockSpec((None, t, LANE), lambda bi, i: (bi, 0, 0)),
                  pl.BlockSpec((tri_w, tri_w), lambda bi, i: (0, 0))],
        out_specs=pl.BlockSpec((None, None, H_B, tq, LANE), lambda bi, i: (bi, i, 0, 0, 0)),
        scratch_shapes=[pltpu.VMEM((n_chunks, tq, tk), jnp.int32),
                        pltpu.VMEM((n_chunks, tq, tk), F32),
                        pltpu.VMEM((H_I, tq, LANE), F32)],
        compiler_params=_params("parallel", "arbitrary"),
        name="dsa_prompt")(iq, iw, q, ik, k, v, tri)


def _seq_spec(shape):
    return pl.BlockSpec((None,) + shape, lambda b, j, pt: (b,) + tuple(0 for _ in shape))


def _const_spec(shape):
    return pl.BlockSpec(shape, lambda b, j, pt: tuple(0 for _ in shape))


HBM_SPEC = pl.BlockSpec(memory_space=pl.ANY)


def _page_copy(cache_hbm, buf, sem, layer, page, slot, p):
    return pltpu.make_async_copy(cache_hbm.at[layer, page], buf.at[slot, :, pl.ds(p * PAGE_SIZE, PAGE_SIZE)],
                                 sem.at[slot])


def _stream_pages(pt_ref, cache_hbm, buf, sem, *, layer, pages):
    b = pl.program_id(0)
    j = pl.program_id(1)
    n_seq = pl.num_programs(0)
    n_steps = pl.num_programs(1)
    t = b * n_steps + j
    slot = lax.rem(t, 2)

    def start(seq, step, dst_slot):
        for p in range(pages):
            _page_copy(cache_hbm, buf, sem, layer, pt_ref[seq, step * pages + p], dst_slot, p).start()

    @pl.when(t == 0)
    def _():
        start(b, j, slot)

    @pl.when(t + 1 < n_seq * n_steps)
    def _():
        wrap = j + 1 == n_steps
        start(jnp.where(wrap, b + 1, b), jnp.where(wrap, 0, j + 1), 1 - slot)

    for p in range(pages):
        _page_copy(cache_hbm, buf, sem, layer, 0, slot, p).wait()
    return slot


def _page_scratch(pages, width):
    return [pltpu.VMEM((2, width, pages * PAGE_SIZE), F32), pltpu.SemaphoreType.DMA((2,))]


def _softmax_step_t(s, m_ref, l_ref, acc_ref, v_t):
    m_prev = m_ref[...]
    m_new = jnp.maximum(m_prev, jnp.max(s, axis=1, keepdims=True))
    alpha = jnp.exp(m_prev - m_new)
    p = jnp.exp(s - m_new)
    l_ref[...] = alpha * l_ref[...] + jnp.sum(p, axis=1, keepdims=True)
    acc_ref[...] = alpha * acc_ref[...] + lax.dot_general(p.astype(BF16), v_t, NT_DIMS, preferred_element_type=F32)
    m_ref[...] = m_new


def _init_softmax(m_ref, l_ref, acc_ref):
    m_ref[...] = jnp.full(m_ref.shape, MASK_VALUE, F32)
    l_ref[...] = jnp.zeros(l_ref.shape, F32)
    acc_ref[...] = jnp.zeros(acc_ref.shape, F32)


def _mla_sample_kernel(pt_ref, cache_hbm, wt_ref, qbd_ref, qr_ref, qs_ref, cg_ref, sg_ref, new_ref, cg_new_ref,
                       sg_new_ref, o_ref, buf, sem, lhs_ref, m_ref, l_ref, acc_ref, *, layer, pages, n_new):
    j = pl.program_id(1)
    rows = o_ref.shape[0]
    n_k = H_A * NOPE_A
    slot = _stream_pages(pt_ref, cache_hbm, buf, sem, layer=layer, pages=pages)

    @pl.when(j == 0)
    def _():
        wt = wt_ref[...]
        lhs_ref[0:n_k, :] = wt
        lhs_ref[n_k:, :] = jnp.dot(qbd_ref[...], wt, preferred_element_type=F32).astype(BF16)
        _init_softmax(m_ref, l_ref, acc_ref)

    def attend(c_t, pe_t, cg, sg, mask):
        n = c_t.shape[1]
        big = jnp.dot(lhs_ref[...], c_t, preferred_element_type=F32)
        ssq = jnp.zeros((H_A, n), F32)
        row_id = lax.broadcasted_iota(jnp.int32, (H_A, n), 0)
        for h in range(H_A):
            kh = big[h * NOPE_A:(h + 1) * NOPE_A]
            ssq = jnp.where(row_id == h, jnp.sum(kh * kh, axis=0, keepdims=True), ssq)
        ssq = ssq + jnp.sum(pe_t * pe_t, axis=0, keepdims=True)
        inv_rms = lax.rsqrt(ssq * (1.0 / DQK_A) + NORM_EPS)
        s_rope = (jnp.dot(qr_ref[...], (pe_t * cg).astype(BF16), preferred_element_type=F32)
                  + jnp.dot(qs_ref[...], (pe_t * sg).astype(BF16), preferred_element_type=F32))
        s = ((big[n_k:] + s_rope).reshape(rows // H_A, H_A, n) * inv_rms[None]).reshape(rows, n)
        if mask is not None:
            s = jnp.where(mask, s, MASK_VALUE)
        _softmax_step_t(s, m_ref, l_ref, acc_ref, c_t)

    attend(buf[slot, 0:KV_LORA, :].astype(BF16), buf[slot, KV_LORA:, :], cg_ref[...], sg_ref[...], None)

    @pl.when(j == pl.num_programs(1) - 1)
    def _():
        new = new_ref[...]
        n = new.shape[1]
        key_i = lax.broadcasted_iota(jnp.int32, (rows, n), 1)
        q_i = lax.broadcasted_iota(jnp.int32, (rows, n), 0) // H_A
        attend(new[:KV_LORA].astype(BF16), new[KV_LORA:], cg_new_ref[...], sg_new_ref[...],
               (key_i <= q_i) & (key_i < n_new))
        o_ref[...] = acc_ref[...] / l_ref[...]


def _mla_sample(layer, page_table, cache_t, wt, qbd, qr, qs, cg_t, sg_t, new_t, cg_new_t, sg_new_t, *, pages, n_new):
    n_seq, n_pages = page_table.shape
    rows = qbd.shape[1]
    n_main = pages * PAGE_SIZE
    n_tail = new_t.shape[2]
    width = KV_LORA + ROPE_A
    in_specs = [HBM_SPEC,
                _const_spec((H_A * NOPE_A, KV_LORA)),
                _seq_spec((rows, H_A * NOPE_A)), _seq_spec((rows, ROPE_A)), _seq_spec((rows, ROPE_A)),
                pl.BlockSpec((ROPE_A, n_main), lambda b, j, pt: (0, j)),
                pl.BlockSpec((ROPE_A, n_main), lambda b, j, pt: (0, j)),
                _seq_spec((width, n_tail)), _const_spec((ROPE_A, n_tail)), _const_spec((ROPE_A, n_tail))]
    return pl.pallas_call(
        functools.partial(_mla_sample_kernel, layer=layer, pages=pages, n_new=n_new),
        out_shape=jax.ShapeDtypeStruct((n_seq, rows, KV_LORA), F32),
        grid_spec=pltpu.PrefetchScalarGridSpec(
            num_scalar_prefetch=1, grid=(n_seq, n_pages // pages), in_specs=in_specs,
            out_specs=pl.BlockSpec((None, rows, KV_LORA), lambda b, j, pt: (b, 0, 0)),
            scratch_shapes=_page_scratch(pages, width) + [
                pltpu.VMEM((H_A * NOPE_A + rows, KV_LORA), BF16),
                pltpu.VMEM((rows, 1), F32), pltpu.VMEM((rows, 1), F32), pltpu.VMEM((rows, KV_LORA), F32)]),
        compiler_params=_params("arbitrary", "arbitrary"),
        name="mla_sample")(page_table, cache_t, wt, qbd, qr, qs, cg_t, sg_t, new_t, cg_new_t, sg_new_t)


def _dsa_select_kernel(pt_ref, cache_hbm, iq_ref, w_ref, ik_new_ref, tri_ref, bias_ref, bias_new_ref, buf, sem,
                       key_ref, key_new_ref, *, layer, pages, n_new, n_sel):
    j = pl.program_id(1)
    n_steps = key_ref.shape[0]
    n_main = key_ref.shape[2]
    n_tail = ik_new_ref.shape[1]
    nq = 8
    slot = _stream_pages(pt_ref, cache_hbm, buf, sem, layer=layer, pages=pages)

    def scores(ik_t):
        logit = jnp.dot(iq_ref[...], ik_t, preferred_element_type=F32)
        weighted = jnp.maximum(logit, 0.0) * w_ref[...]
        return jnp.sum(weighted.reshape(H_I, nq, ik_t.shape[1]), axis=0)

    key_ref[j] = _sortable_key(scores(buf[slot].astype(BF16)))

    @pl.when(j == n_steps - 1)
    def _():
        key_i = lax.broadcasted_iota(jnp.int32, (nq, n_tail), 1)
        q_i = lax.broadcasted_iota(jnp.int32, (nq, n_tail), 0) % n_new
        valid_new = (key_i <= q_i) & (key_i < n_new)
        key_new_ref[...] = _sortable_key(jnp.where(valid_new, scores(ik_new_ref[...]), -jnp.inf))

        def count_ge(cand):
            cand_b = jnp.broadcast_to(cand, (nq, LANE))
            cnts = [jnp.zeros((nq, LANE), jnp.int32) for _ in range(4)]
            tiles = [key_ref[c, :, t * LANE:(t + 1) * LANE] for c in range(n_steps) for t in range(n_main // LANE)]
            tiles += [key_new_ref[:, t * LANE:(t + 1) * LANE] for t in range(n_tail // LANE)]
            for idx, tile in enumerate(tiles):
                cnts[idx % 4] = cnts[idx % 4] + jnp.where(tile >= cand_b, 1, 0)
            return jnp.sum((cnts[0] + cnts[1]) + (cnts[2] + cnts[3]), axis=1, keepdims=True)

        thr = _kth_largest_key(count_ge, n_sel, nq)

        def bias_by_threshold():
            thr_eff = jnp.maximum(thr, KEY_NEG_INF + 1)

            def body(c, _):
                bias_ref[c] = jnp.where(key_ref[c] >= thr_eff, 0.0, MASK_VALUE)
                return 0

            lax.fori_loop(0, n_steps, body, 0)
            bias_new_ref[...] = jnp.where(key_new_ref[...] >= thr_eff, 0.0, MASK_VALUE)

        def bias_with_ties():
            need = (n_sel - count_ge(thr + 1)).astype(F32)

            def body(c, carry):
                bias, carry = _select_bias(key_ref[c], thr, need, carry, None, tri_ref)
                bias_ref[c] = bias
                return carry

            carry = lax.fori_loop(0, n_steps, body, jnp.zeros((nq, 1), F32))
            bias_new, _ = _select_bias(key_new_ref[...], thr, need, carry, valid_new, tri_ref)
            bias_new_ref[...] = bias_new

        lax.cond(_has_excess_ties(count_ge(thr), thr, n_sel), bias_with_ties, bias_by_threshold)


def _dsa_select(layer, page_table, cache_idx_t, iq, iw, ik_new_t, *, pages, n_new, n_sel):
    n_seq, n_pages = page_table.shape
    n_steps = n_pages // pages
    n_main = pages * PAGE_SIZE
    n_tail = ik_new_t.shape[2]
    rows = iq.shape[1]
    tri = jnp.asarray(np.arange(2 * LANE)[:, None] <= np.arange(2 * LANE)[None, :], BF16)
    in_specs = [HBM_SPEC, _seq_spec((rows, D_I)), _seq_spec((rows, 1)), _seq_spec((D_I, n_tail)),
                _const_spec((2 * LANE, 2 * LANE))]
    return pl.pallas_call(
        functools.partial(_dsa_select_kernel, layer=layer, pages=pages, n_new=n_new, n_sel=n_sel),
        out_shape=(jax.ShapeDtypeStruct((n_seq, n_steps, 8, n_main), F32),
                   jax.ShapeDtypeStruct((n_seq, 8, n_tail), F32)),
        grid_spec=pltpu.PrefetchScalarGridSpec(
            num_scalar_prefetch=1, grid=(n_seq, n_steps), in_specs=in_specs,
            out_specs=(pl.BlockSpec((None, n_steps, 8, n_main), lambda b, j, pt: (b, 0, 0, 0)),
                       pl.BlockSpec((None, 8, n_tail), lambda b, j, pt: (b, 0, 0))),
            scratch_shapes=_page_scratch(pages, D_I) + [
                pltpu.VMEM((n_steps, 8, n_main), jnp.int32), pltpu.VMEM((8, n_tail), jnp.int32)]),
        compiler_params=_params("arbitrary", "arbitrary"),
        name="dsa_select")(page_table, cache_idx_t, iq, iw, ik_new_t, tri)


def _dsa_sample_kernel(pt_ref, cache_hbm, q_ref, bias_ref, kv_new_ref, bias_new_ref, o_ref, buf, sem,
                       m_ref, l_ref, acc_ref, *, layer, pages):
    j = pl.program_id(1)
    rows = q_ref.shape[0]
    kw = KVH_B * HEAD_DIM
    slot = _stream_pages(pt_ref, cache_hbm, buf, sem, layer=layer, pages=pages)

    @pl.when(j == 0)
    def _():
        _init_softmax(m_ref, l_ref, acc_ref)

    def attend(k_t, v_t, bias):
        n = k_t.shape[1]
        s = jnp.dot(q_ref[...], k_t, preferred_element_type=F32)
        s = (s.reshape(rows // 8, 8, n) + bias[None]).reshape(rows, n)
        _softmax_step_t(s, m_ref, l_ref, acc_ref, v_t)

    attend(buf[slot, 0:kw, :].astype(BF16), buf[slot, kw:, :].astype(BF16), bias_ref[...])

    @pl.when(j == pl.num_programs(1) - 1)
    def _():
        kv_new = kv_new_ref[...]
        attend(kv_new[:kw], kv_new[kw:], bias_new_ref[...])
        o_ref[...] = acc_ref[...] / l_ref[...]


def _dsa_sample(layer, page_table, cache_kv_t, q, bias, kv_new_t, bias_new, *, pages):
    n_seq, n_pages = page_table.shape
    n_steps = n_pages // pages
    n_main = pages * PAGE_SIZE
    rows = q.shape[1]
    n_tail = kv_new_t.shape[2]
    kw = KVH_B * HEAD_DIM
    in_specs = [HBM_SPEC, _seq_spec((rows, kw)),
                pl.BlockSpec((None, None, 8, n_main), lambda b, j, pt: (b, j, 0, 0)),
                _seq_spec((2 * kw, n_tail)), _seq_spec((8, n_tail))]
    return pl.pallas_call(
        functools.partial(_dsa_sample_kernel, layer=layer, pages=pages),
        out_shape=jax.ShapeDtypeStruct((n_seq, rows, kw), F32),
        grid_spec=pltpu.PrefetchScalarGridSpec(
            num_scalar_prefetch=1, grid=(n_seq, n_steps), in_specs=in_specs,
            out_specs=pl.BlockSpec((None, rows, kw), lambda b, j, pt: (b, 0, 0)),
            scratch_shapes=_page_scratch(pages, 2 * kw) + [
                pltpu.VMEM((rows, 1), F32), pltpu.VMEM((rows, 1), F32), pltpu.VMEM((rows, kw), F32)]),
        compiler_params=_params("arbitrary", "arbitrary"),
        name="dsa_sample")(page_table, cache_kv_t, q, bias, kv_new_t, bias_new)


def _router_kernel(x_ref, g_ref, w_ref, b_ref, logit_ref):
    x = x_ref[...]
    ms = jnp.mean(x * x, axis=-1, keepdims=True)
    h = x * lax.rsqrt(ms + NORM_EPS) * g_ref[...]
    logit_ref[...] = jnp.dot(h, w_ref[...], preferred_element_type=F32, precision=lax.Precision.HIGHEST) + b_ref[...]


def _router(x, gain, w_router, b_router, *, tm=512):
    n, d = x.shape
    e = w_router.shape[1]
    return pl.pallas_call(
        _router_kernel,
        out_shape=jax.ShapeDtypeStruct((n, e), F32),
        grid=(n // tm,),
        in_specs=[pl.BlockSpec((tm, d), lambda i: (i, 0)), pl.BlockSpec((1, d), lambda i: (0, 0)),
                  pl.BlockSpec((d, e), lambda i: (0, 0)), pl.BlockSpec((1, e), lambda i: (0, 0))],
        out_specs=pl.BlockSpec((tm, e), lambda i: (i, 0)),
        compiler_params=_params("parallel"),
        name="router")(x, gain.reshape(1, d), w_router, b_router.reshape(1, e))


def _expert_kernel(be_ref, bv_ref, x_ref, g_ref, w1_ref, b1_ref, w2_ref, b2_ref, o_ref):
    i = pl.program_id(0)

    @pl.when(bv_ref[i] != 0)
    def _():
        d_ff = w2_ref.shape[0]
        x = x_ref[...]
        ms = jnp.mean(x * x, axis=-1, keepdims=True)
        h = (x * lax.rsqrt(ms + NORM_EPS) * g_ref[...]).astype(BF16)
        u = jnp.dot(h, w1_ref[...].astype(BF16), preferred_element_type=F32) + b1_ref[...]
        g = jnp.minimum(u[:, :d_ff], SWIGLU_LIMIT)
        up = jnp.clip(u[:, d_ff:], -SWIGLU_LIMIT, SWIGLU_LIMIT)
        a = g * jax.nn.sigmoid(SWIGLU_ALPHA * g) * (up + 1.0)
        o_ref[...] = jnp.dot(a.astype(BF16), w2_ref[...].astype(BF16), preferred_element_type=F32) + b2_ref[...]

    @pl.when(bv_ref[i] == 0)
    def _():
        o_ref[...] = jnp.zeros(o_ref.shape, o_ref.dtype)


def _experts(block_e, block_valid, xb, gain, w1, b1, w2, b2, *, tm):
    n_rows, d = xb.shape
    e, _, two_ff = w1.shape
    d_ff = w2.shape[1]
    return pl.pallas_call(
        _expert_kernel,
        out_shape=jax.ShapeDtypeStruct((n_rows, d), F32),
        grid_spec=pltpu.PrefetchScalarGridSpec(
            num_scalar_prefetch=2, grid=(n_rows // tm,),
            in_specs=[pl.BlockSpec((tm, d), lambda i, be, bv: (i, 0)),
                      pl.BlockSpec((1, d), lambda i, be, bv: (0, 0)),
                      pl.BlockSpec((None, d, two_ff), lambda i, be, bv: (be[i], 0, 0)),
                      pl.BlockSpec((None, 1, two_ff), lambda i, be, bv: (be[i], 0, 0)),
                      pl.BlockSpec((None, d_ff, d), lambda i, be, bv: (be[i], 0, 0)),
                      pl.BlockSpec((None, 1, d), lambda i, be, bv: (be[i], 0, 0))],
            out_specs=pl.BlockSpec((tm, d), lambda i, be, bv: (i, 0))),
        compiler_params=_params("arbitrary"),
        name="experts")(block_e, block_valid, xb, gain.reshape(1, d), w1, b1.reshape(e, 1, two_ff), w2,
                        b2.reshape(e, 1, d))


def _moe(x, gain, w_router, b_router, w1, b1, w2, b2, *, tm_e=256):
    n, d = x.shape
    logits = _router(x, gain, w_router, b_router)
    top_val, top_idx = lax.top_k(logits, TOP_K)
    gate = jax.nn.softmax(top_val, axis=-1)
    flat_e = top_idx.reshape(-1).astype(jnp.int32)
    nk = flat_e.shape[0]
    order = jnp.argsort(flat_e, stable=True).astype(jnp.int32)
    e_sorted = flat_e[order]
    experts = jnp.arange(N_EXPERTS, dtype=jnp.int32)
    counts = jnp.sum((flat_e[None, :] == experts[:, None]).astype(jnp.int32), axis=1)
    start = jnp.cumsum(counts) - counts
    padded = _cdiv(counts, tm_e) * tm_e
    pad_end = jnp.cumsum(padded)
    pad_start = pad_end - padded
    n_blocks = _cdiv(nk, tm_e) + N_EXPERTS
    block_start = jnp.arange(n_blocks, dtype=jnp.int32) * tm_e
    block_e = jnp.minimum(jnp.sum((block_start[:, None] >= pad_end[None, :]).astype(jnp.int32), axis=1), N_EXPERTS - 1)
    block_valid = (block_start < pad_end[-1]).astype(jnp.int32)
    row = jnp.arange(n_blocks * tm_e, dtype=jnp.int32)
    row_e = jnp.repeat(block_e, tm_e)
    offset = row - pad_start[row_e]
    src = jnp.clip(start[row_e] + offset, 0, nk - 1)
    row_tok = jnp.where(offset < counts[row_e], order[src] // TOP_K, 0)
    dest_sorted = pad_start[e_sorted] + jnp.arange(nk, dtype=jnp.int32) - start[e_sorted]
    dest = dest_sorted[jnp.argsort(order)]
    yb = _experts(block_e, block_valid, x[row_tok], gain, w1, b1, w2, b2, tm=tm_e)
    dest_k = dest.reshape(n, TOP_K)
    y = x
    for k in range(TOP_K):
        y = y + yb[dest_k[:, k]] * gate[:, k:k + 1]
    return y


def _rms(x, g, n=None):
    n = x.shape[-1] if n is None else n
    y = x * lax.rsqrt(jnp.sum(x * x, axis=-1, keepdims=True) / n + NORM_EPS)
    return y * g


def _rope_tables(pos, n_rot):
    half = n_rot // 2
    inv_freq = ROPE_THETA ** (-jnp.arange(half, dtype=F32) / half)
    ang = pos.astype(F32)[:, None] * inv_freq[None, :]
    return jnp.cos(ang), jnp.sin(ang)


def _rope(x, cos, sin, n_rot):
    d = x.shape[-1]
    half = n_rot // 2
    cos = cos[:, None, :]
    sin = sin[:, None, :]
    x1 = x[..., d - n_rot:d - half]
    x2 = x[..., d - half:]
    return jnp.concatenate([x[..., :d - n_rot], x1 * cos - x2 * sin, x2 * cos + x1 * sin], axis=-1)


def _regroup_w_in(w_in):
    cuts = np.cumsum([0, Q_LORA, KV_LORA, ROPE_A, H_B * HEAD_DIM, KVH_B * HEAD_DIM, KVH_B * HEAD_DIM,
                      H_I * D_I, D_I, H_I, 2 * w_in.shape[0]])
    w_cq, w_ckv, w_kpe, w_qb, w_kb, w_vb, w_iq, w_ik, w_iw, w_g = [w_in[:, a:b] for a, b in zip(cuts[:-1], cuts[1:])]
    misc_pad = jnp.zeros((w_in.shape[0], COL_GATE - COL_MISC - ROPE_A - D_I - H_I), w_in.dtype)
    return jnp.concatenate([w_cq, w_ckv, w_qb, w_kb, w_vb, w_iq, w_kpe, w_ik, w_iw, misc_pad, w_g], axis=1).astype(BF16)


def _layer(l, xp, xs, page_table, caches, wts, *, t_real):
    (cache_mla, cache_dsa_kv, cache_dsa_idx) = caches
    n_b, t_pad, d = xp.shape
    n_seq, n_ds, _ = xs.shape
    past = page_table.shape[1] * PAGE_SIZE
    n_p = n_b * t_pad
    n_s = n_seq * n_ds
    n_sel_p = min(MAX_SEL, (t_real - N_META) // 4)
    n_sel_s = min(MAX_SEL, (past + n_ds) // 4)

    x_all = jnp.concatenate([xp.reshape(n_p, d), xs.reshape(n_s, d)], axis=0)
    n_all = _round_up(n_p + n_s, 1024)
    x_all = jnp.pad(x_all, ((0, n_all - n_p - n_s), (0, 0)))
    pos = jnp.concatenate([jnp.tile(jnp.arange(t_pad), n_b), jnp.tile(past + jnp.arange(n_ds), n_seq),
                           jnp.zeros((n_all - n_p - n_s,), jnp.int32)])
    z = _matmul(x_all, _regroup_w_in(wts["w_in"][l]), gain=wts["g_attn_norm"][l], tm=1024, tn=1024, name="proj_in")

    cos_a, sin_a = _rope_tables(pos, ROPE_A)
    cos_b, sin_b = _rope_tables(pos, HEAD_DIM)
    cos_i, sin_i = _rope_tables(pos, D_I_ROPE)
    ones = lambda w: jnp.ones((n_all, w), F32)
    zeros = lambda w: jnp.zeros((n_all, w), F32)
    cos_a_t = jnp.concatenate([ones(NOPE_A), cos_a, cos_a, zeros(LANE - DQK_A)], axis=1)
    sin_a_t = jnp.concatenate([zeros(NOPE_A), -sin_a, sin_a, zeros(LANE - DQK_A)], axis=1)
    cos_b_t = jnp.tile(jnp.concatenate([cos_b, cos_b], axis=1), (1, LANE // HEAD_DIM))
    sin_b_t = jnp.tile(jnp.concatenate([-sin_b, sin_b], axis=1), (1, LANE // HEAD_DIM))
    cos_i_t = jnp.tile(jnp.concatenate([ones(D_I - D_I_ROPE), cos_i, cos_i], axis=1), (1, LANE // D_I))
    sin_i_t = jnp.tile(jnp.concatenate([zeros(D_I - D_I_ROPE), -sin_i, sin_i], axis=1), (1, LANE // D_I))
    pad_gain = lambda g: jnp.pad(g, (0, LANE - DQK_A)).reshape(1, LANE)
    tm = 512

    w_uq = jnp.pad(wts["w_uq"][l], ((0, 0), (0, 0), (0, LANE - DQK_A))).reshape(Q_LORA, H_A * LANE).astype(BF16)
    (q_a,) = _row_call(_q_up_kernel, n_all, tm,
                       [(z, Q_LORA, COL_CQ // Q_LORA), (wts["g_cq"][l].reshape(1, Q_LORA), None, 0), (w_uq, None, 0),
                        (pad_gain(wts["g_qa"][l] * (DQK_A ** -0.5)), None, 0), (cos_a_t, LANE, 0), (sin_a_t, LANE, 0)],
                       [(H_A * LANE, BF16)], "q_up")
    sl = slice(n_p, n_p + n_s)
    tm_s = min(tm, n_s)
    (q_a_s,) = _row_call(_q_up_kernel, n_s, tm_s,
                         [(z[sl], Q_LORA, COL_CQ // Q_LORA), (wts["g_cq"][l].reshape(1, Q_LORA), None, 0), (w_uq, None, 0),
                          (pad_gain(wts["g_qa"][l] * (DQK_A ** -0.5)), None, 0), (cos_a_t[sl], LANE, 0),
                          (sin_a_t[sl], LANE, 0)],
                         [(H_A * LANE, F32)], "q_up_sample")

    w_uk = jnp.pad(wts["w_uk"][l], ((0, 0), (0, 0), (0, LANE - NOPE_A))).reshape(KV_LORA, H_A * LANE)
    w_kv = jnp.concatenate([w_uk, wts["w_uv"][l].reshape(KV_LORA, H_A * V_A)], axis=1).astype(BF16)
    place = np.zeros((LANE, H_A * LANE), np.float32)
    for h in range(H_A):
        place[np.arange(ROPE_A), h * LANE + NOPE_A + np.arange(ROPE_A)] = 1.0
    k_a, v_a = _row_call(_kv_up_kernel, n_all, tm,
                         [(z, KV_LORA, COL_CKV // KV_LORA), (wts["g_ckv"][l].reshape(1, KV_LORA), None, 0),
                          (z, LANE, COL_MISC // LANE), (w_kv, None, 0), (jnp.asarray(place, BF16), None, 0),
                          (pad_gain(wts["g_ka"][l]), None, 0), (cos_a_t, LANE, 0), (sin_a_t, LANE, 0)],
                         [(H_A * LANE, BF16), (H_A * V_A, BF16)], "kv_up")

    expand = np.zeros((H_B * HEAD_DIM, H_B * KVH_B * HEAD_DIM), np.float32)
    for h in range(H_B):
        cols = h * KVH_B * HEAD_DIM + (h // GROUP_B) * HEAD_DIM + np.arange(HEAD_DIM)
        expand[h * HEAD_DIM + np.arange(HEAD_DIM), cols] = 1.0
    two = lambda g: jnp.tile(g, LANE // HEAD_DIM).reshape(1, LANE)
    g_ik_t = jnp.pad(wts["g_ik"][l], (ROPE_A, LANE - ROPE_A - D_I)).reshape(1, LANE)
    mla_rows, kv_rows, i_k, i_w, q_b_pad, k_b2, v_b2, i_q, i_k_bf = _row_call(
        _post_z_kernel, n_all, tm,
        [(z, COL_GATE, 0), (wts["g_ckv"][l].reshape(1, KV_LORA), None, 0),
         (two(wts["g_qb"][l] * (HEAD_DIM ** -0.5)), None, 0), (two(wts["g_kb"][l]), None, 0), (g_ik_t, None, 0),
         (cos_b_t, LANE, 0), (sin_b_t, LANE, 0), (cos_i_t, LANE, 0), (sin_i_t, LANE, 0),
         (jnp.asarray(expand, BF16), None, 0)],
        [(KV_LORA + ROPE_A, F32), (2 * KVH_B * HEAD_DIM, F32), (D_I, F32), (H_I, F32), (H_B * LANE, BF16),
         (KVH_B * HEAD_DIM, BF16), (KVH_B * HEAD_DIM, BF16), (H_I * D_I, BF16), (D_I, BF16)], "post_z")
    q_b_pad = q_b_pad.reshape(n_all, H_B, KVH_B * HEAD_DIM)
    i_q = i_q.reshape(n_all, H_I, D_I)

    tq = 128
    nq = t_pad // tq
    n_kc = max(dd for dd in range(1, 5) if nq % dd == 0)
    tk = n_kc * tq
    o_a_p = _mla_prompt(q_a[:n_p].reshape(n_b, t_pad, H_A * LANE), k_a[:n_p].reshape(n_b, t_pad, H_A * LANE),
                        v_a[:n_p].reshape(n_b, t_pad, H_A * V_A), tq=tk)

    blocked = lambda a: a.reshape((n_b, nq, tq) + a.shape[1:]).swapaxes(2, 3)
    o_b_p = _dsa_prompt(blocked(i_q[:n_p]), i_w[:n_p].reshape(n_b, t_pad, H_I), blocked(q_b_pad[:n_p]),
                        i_k_bf[:n_p].reshape(n_b, t_pad, D_I), k_b2[:n_p].reshape(n_b, t_pad, KVH_B * HEAD_DIM),
                        v_b2[:n_p].reshape(n_b, t_pad, KVH_B * HEAD_DIM), tq=tq, tk=tk, n_sel=n_sel_p)
    o_b_p = o_b_p.reshape(n_b, nq, KVH_B, GROUP_B, tq, KVH_B, HEAD_DIM)
    o_b_p = jnp.stack([o_b_p[:, :, g, :, :, g, :] for g in range(KVH_B)], axis=2)
    o_b_p = o_b_p.reshape(n_b, nq, H_B, tq, HEAD_DIM).swapaxes(2, 3).reshape(n_p, H_B * HEAD_DIM)

    pages = max(dd for dd in range(1, MAX_PAGES_PER_STEP + 1) if page_table.shape[1] % dd == 0)
    assert 8 % n_ds == 0
    n_tail = LANE
    tail_t = lambda a: jnp.pad(a.reshape(n_seq, n_ds, a.shape[-1]), ((0, 0), (0, n_tail - n_ds), (0, 0))).swapaxes(1, 2)
    cache_mla_t = jnp.swapaxes(cache_mla, 2, 3)
    cache_idx_t = jnp.swapaxes(cache_dsa_idx, 2, 3)
    cache_kv_t = jnp.moveaxis(cache_dsa_kv, 2, -1).reshape(cache_dsa_kv.shape[:2] + (2 * KVH_B * HEAD_DIM, PAGE_SIZE))
    g_ka = wts["g_ka"][l]
    pos_k = jnp.arange(past + n_tail)
    cos_k, sin_k = _rope_tables(pos_k, ROPE_A)
    g1, g2 = g_ka[NOPE_A:NOPE_A + ROPE_A // 2], g_ka[NOPE_A + ROPE_A // 2:]
    cg_t = jnp.concatenate([cos_k * g1, cos_k * g2], axis=1).T
    sg_t = jnp.concatenate([sin_k * g1, -sin_k * g2], axis=1).T
    qa_s = q_a_s.reshape(n_s, H_A, LANE)
    rows_a = n_ds * H_A
    eye_h = jnp.eye(H_A, dtype=F32)
    qn_s = qa_s[..., :NOPE_A] * g_ka[:NOPE_A]
    qbd = (qn_s[:, :, None, :] * eye_h[None, :, :, None]).reshape(n_seq, rows_a, H_A * NOPE_A)
    q_r = qa_s[:, :, NOPE_A:DQK_A].reshape(n_seq, rows_a, ROPE_A)
    q_r_swap = jnp.concatenate([q_r[..., ROPE_A // 2:], q_r[..., :ROPE_A // 2]], axis=-1)
    wt = wts["w_uk"][l].reshape(KV_LORA, H_A * NOPE_A).T
    o_lat = _mla_sample(l, page_table, cache_mla_t, wt.astype(BF16), qbd.astype(BF16), q_r.astype(BF16),
                        q_r_swap.astype(BF16), cg_t[:, :past], sg_t[:, :past], tail_t(mla_rows[sl]),
                        cg_t[:, past:], sg_t[:, past:], pages=pages, n_new=n_ds)
    o_a_s = jnp.einsum("bqhr,rhv->bqhv", o_lat.reshape(n_seq, n_ds, H_A, KV_LORA), wts["w_uv"][l])
    o_a_s = o_a_s.reshape(n_s, H_A * V_A)

    slot = jnp.arange(8) % n_ds
    iq_s = i_q[sl].reshape(n_seq, n_ds, H_I, D_I)[:, slot].swapaxes(1, 2).reshape(n_seq, H_I * 8, D_I)
    iw_s = i_w[sl].reshape(n_seq, n_ds, H_I)[:, slot].swapaxes(1, 2).reshape(n_seq, H_I * 8, 1)
    bias, bias_new = _dsa_select(l, page_table, cache_idx_t, iq_s, iw_s, tail_t(i_k_bf[sl]), pages=pages,
                                 n_new=n_ds, n_sel=n_sel_s)
    qb_s = q_b_pad[sl].reshape(n_seq, n_ds, H_B, KVH_B * HEAD_DIM).swapaxes(1, 2).reshape(n_seq, H_B * n_ds, -1)
    kv_new_t = tail_t(kv_rows[sl]).astype(BF16)
    o_b_s = _dsa_sample(l, page_table, cache_kv_t, qb_s, bias, kv_new_t, bias_new, pages=pages)
    o_b_s = o_b_s.reshape(n_seq, KVH_B, GROUP_B, n_ds, KVH_B, HEAD_DIM)
    o_b_s = jnp.stack([o_b_s[:, g, :, :, g, :] for g in range(KVH_B)], axis=1)
    o_b_s = o_b_s.reshape(n_seq, H_B, n_ds, HEAD_DIM).swapaxes(1, 2).reshape(n_s, H_B * HEAD_DIM)

    zeros_tail = jnp.zeros((n_all - n_p - n_s, H_A * V_A), BF16)
    o_a = jnp.concatenate([o_a_p.reshape(n_p, H_A * V_A), o_a_s.astype(BF16), zeros_tail], axis=0)
    o_b = jnp.concatenate([o_b_p, o_b_s.astype(BF16), zeros_tail], axis=0)
    assert COL_GATE % (2 * d) == 0
    x_all = _merge(x_all, o_a, o_b, z, wts["w_pa"][l].astype(BF16), wts["w_pb"][l].astype(BF16),
                   wts["w_o"][l].astype(BF16))
    x_all = _moe(x_all, wts["g_ffn_norm"][l], wts["w_router"][l], wts["b_router"][l], wts["w_mlp1"][l],
                 wts["b_mlp1"][l], wts["w_mlp2"][l], wts["b_mlp2"][l])

    new_p = (mla_rows[:n_p].reshape(n_b, t_pad, -1)[:, :t_real],
             kv_rows[:n_p].reshape(n_b, t_pad, 2, KVH_B, HEAD_DIM)[:, :t_real],
             i_k[:n_p].reshape(n_b, t_pad, D_I)[:, :t_real])
    new_s = (mla_rows[sl].reshape(n_seq, n_ds, -1), kv_rows[sl].reshape(n_seq, n_ds, 2, KVH_B, HEAD_DIM),
             i_k[sl].reshape(n_seq, n_ds, D_I))
    return x_all[:n_p].reshape(n_b, t_pad, d), x_all[sl].reshape(n_seq, n_ds, d), new_p, new_s


def kernel(x_prompt, x_sample, cache_mla, cache_dsa_kv, cache_dsa_idx, page_table, meta_tokens, g_attn_norm, w_in,
           g_cq, w_uq, g_qa, g_ckv, w_uk, w_uv, g_ka, g_qb, g_kb, g_ik, w_pa, w_pb, w_o, g_ffn_norm, w_router,
           b_router, w_mlp1, b_mlp1, w_mlp2, b_mlp2):
    n_b, n_s, d = x_prompt.shape
    t_real = n_s + N_META
    t_pad = _round_up(t_real, LANE)
    wts = dict(g_attn_norm=g_attn_norm, w_in=w_in, g_cq=g_cq, w_uq=w_uq, g_qa=g_qa, g_ckv=g_ckv, w_uk=w_uk, w_uv=w_uv,
               g_ka=g_ka, g_qb=g_qb, g_kb=g_kb, g_ik=g_ik, w_pa=w_pa, w_pb=w_pb, w_o=w_o, g_ffn_norm=g_ffn_norm,
               w_router=w_router, b_router=b_router, w_mlp1=w_mlp1, b_mlp1=b_mlp1, w_mlp2=w_mlp2, b_mlp2=b_mlp2)
    meta = jnp.broadcast_to(meta_tokens[None].astype(x_prompt.dtype), (n_b, N_META, d))
    xp = jnp.concatenate([meta, x_prompt, jnp.zeros((n_b, t_pad - t_real, d), x_prompt.dtype)], axis=1)
    xs = x_sample
    outs_p, outs_s = [], []
    for l in range(w_in.shape[0]):
        xp, xs, new_p, new_s = _layer(l, xp, xs, page_table, (cache_mla, cache_dsa_kv, cache_dsa_idx), wts,
                                      t_real=t_real)
        outs_p.append(new_p)
        outs_s.append(new_s)
    stack = lambda outs, i: jnp.stack([o[i] for o in outs], axis=0)
    return (xp[:, N_META:t_real], xs, stack(outs_p, 0), stack(outs_p, 1), stack(outs_p, 2),
            stack(outs_s, 0), stack(outs_s, 1), stack(outs_s, 2))
```

```python
import functools

import jax
import jax.numpy as jnp
import numpy as np
from jax import lax
from jax.experimental import pallas as pl
from jax.experimental.pallas import tpu as pltpu

N_META = 16
HEAD_DIM = 64
H_A = 8
NOPE_A = 64
ROPE_A = 32
DQK_A = NOPE_A + ROPE_A
V_A = 64
Q_LORA = 512
KV_LORA = 256
H_B = 8
KVH_B = 2
GROUP_B = H_B // KVH_B
H_I = 8
D_I = 32
D_I_ROPE = 16
MAX_SEL = 256
N_EXPERTS = 32
TOP_K = 4
SWIGLU_LIMIT = 7.0
SWIGLU_ALPHA = 1.702
ROPE_THETA = 10000.0
NORM_EPS = 1e-6
PAGE_SIZE = 128

LANE = 128
MASK_VALUE = -1e30
INT_MIN = -(2 ** 31)
VMEM_LIMIT = 56 * 1024 * 1024
MAX_PAGES_PER_STEP = 16

COL_CQ = 0
COL_CKV = 512
COL_QB = 768
COL_KB = 1280
COL_VB = 1408
COL_IQ = 1536
COL_MISC = 1792
COL_GATE = 2048
N_PROJ = 4096

F32 = jnp.float32
BF16 = jnp.bfloat16
NT_DIMS = (((1,), (1,)), ((), ()))


def _cdiv(a, b):
    return (a + b - 1) // b


def _round_up(a, b):
    return _cdiv(a, b) * b


def _params(*sem):
    return pltpu.CompilerParams(dimension_semantics=sem, vmem_limit_bytes=VMEM_LIMIT)


def _mm_kernel(x_ref, w_ref, o_ref):
    o_ref[...] = jnp.dot(x_ref[...].astype(BF16), w_ref[...], preferred_element_type=F32).astype(o_ref.dtype)


def _mm_norm_kernel(x_ref, g_ref, w_ref, o_ref, xn_ref):
    @pl.when(pl.program_id(1) == 0)
    def _():
        x = x_ref[...].astype(F32)
        ms = jnp.mean(x * x, axis=-1, keepdims=True)
        xn_ref[...] = (x * lax.rsqrt(ms + NORM_EPS) * g_ref[...]).astype(BF16)

    o_ref[...] = jnp.dot(xn_ref[...], w_ref[...], preferred_element_type=F32).astype(o_ref.dtype)


def _matmul(x, w, *, gain=None, col_block=0, k=None, tm=512, tn=512, out_dtype=F32, name="mm"):
    m = x.shape[0]
    k = x.shape[1] if k is None else k
    n = w.shape[1]
    tn = min(tn, n)
    assert m % tm == 0 and n % tn == 0 and w.shape[0] == k
    grid = (m // tm, n // tn)
    x_spec = pl.BlockSpec((tm, k), lambda i, j: (i, col_block))
    w_spec = pl.BlockSpec((k, tn), lambda i, j: (0, j))
    o_spec = pl.BlockSpec((tm, tn), lambda i, j: (i, j))
    out_shape = jax.ShapeDtypeStruct((m, n), out_dtype)
    if gain is None:
        return pl.pallas_call(_mm_kernel, out_shape=out_shape, grid=grid, in_specs=[x_spec, w_spec],
                              out_specs=o_spec, compiler_params=_params("parallel", "arbitrary"), name=name)(x, w)
    g_spec = pl.BlockSpec((1, k), lambda i, j: (0, 0))
    return pl.pallas_call(_mm_norm_kernel, out_shape=out_shape, grid=grid, in_specs=[x_spec, g_spec, w_spec],
                          out_specs=o_spec, scratch_shapes=[pltpu.VMEM((tm, k), BF16)],
                          compiler_params=_params("parallel", "arbitrary"), name=name)(
                              x, gain.reshape(1, k).astype(F32), w)


def _lane_ids(tm):
    return lax.broadcasted_iota(jnp.int32, (tm, LANE), 1)


def _rotate_pairs(t, cos_t, sin_t, lane, group, split, half):
    first = (lane & (group - 1)) < split
    partner = jnp.where(first, pltpu.roll(t, LANE - half, 1), pltpu.roll(t, half, 1))
    return t * cos_t + partner * sin_t


def _head_tiles(x, fn):
    tiles = [fn(x[:, i * LANE:(i + 1) * LANE]) for i in range(x.shape[1] // LANE)]
    return tiles[0] if len(tiles) == 1 else jnp.concatenate(tiles, axis=1)


def _mla_head_post(t, gain, cos_t, sin_t, lane):
    ms = jnp.sum(t * t, axis=1, keepdims=True) * (1.0 / DQK_A)
    return _rotate_pairs(t * lax.rsqrt(ms + NORM_EPS) * gain, cos_t, sin_t, lane, LANE, NOPE_A + ROPE_A // 2,
                         ROPE_A // 2)


def _q_up_kernel(z_ref, gz_ref, w_ref, gq_ref, cos_ref, sin_ref, o_ref):
    x = z_ref[...]
    xn = (x * lax.rsqrt(jnp.mean(x * x, axis=-1, keepdims=True) + NORM_EPS) * gz_ref[...]).astype(BF16)
    q = jnp.dot(xn, w_ref[...], preferred_element_type=F32)
    lane = _lane_ids(x.shape[0])
    post = lambda t: _mla_head_post(t, gq_ref[...], cos_ref[...], sin_ref[...], lane)
    o_ref[...] = _head_tiles(q, post).astype(o_ref.dtype)


def _kv_up_kernel(z_ref, gz_ref, misc_ref, w_ref, place_ref, gk_ref, cos_ref, sin_ref, k_ref, v_ref):
    x = z_ref[...]
    c = (x * lax.rsqrt(jnp.mean(x * x, axis=-1, keepdims=True) + NORM_EPS) * gz_ref[...]).astype(BF16)
    kv = jnp.dot(c, w_ref[...], preferred_element_type=F32)
    n_k = H_A * LANE
    misc = misc_ref[...]
    hi = misc.astype(BF16)
    lo = (misc - hi.astype(F32)).astype(BF16)
    k = kv[:, :n_k] + (jnp.dot(hi, place_ref[...], preferred_element_type=F32)
                       + jnp.dot(lo, place_ref[...], preferred_element_type=F32))
    lane = _lane_ids(x.shape[0])
    post = lambda t: _mla_head_post(t, gk_ref[...], cos_ref[...], sin_ref[...], lane)
    k_ref[...] = _head_tiles(k, post).astype(k_ref.dtype)
    v_ref[...] = kv[:, n_k:].astype(v_ref.dtype)


def _row_call(kernel_fn, n, tm, ins, outs, name):
    def spec(width, col_block, shape):
        if width is None:
            return pl.BlockSpec(shape, lambda i: tuple(0 for _ in shape))
        return pl.BlockSpec((tm, width), lambda i: (i, col_block))
    in_specs = [spec(w, cb, a.shape) for a, w, cb in ins]
    out_specs = tuple(pl.BlockSpec((tm, w), lambda i: (i, 0)) for w, _ in outs)
    out_shape = tuple(jax.ShapeDtypeStruct((n, w), dt) for w, dt in outs)
    return pl.pallas_call(kernel_fn, out_shape=out_shape, grid=(n // tm,), in_specs=in_specs, out_specs=out_specs,
                          compiler_params=_params("parallel"), name=name)(*[a for a, _, _ in ins])


def _post_z_kernel(z_ref, gckv_ref, gqb_ref, gkb_ref, gik_ref, cb_ref, sb_ref, ci_ref, si_ref, expand_ref,
                   mla_ref, kv_ref, ik_ref, iw_ref, qb_ref, k2_ref, v2_ref, iq_ref, ikb_ref):
    z = z_ref[...]
    tm = z.shape[0]
    lane = _lane_ids(tm)
    cb, sb, ci, si = cb_ref[...], sb_ref[...], ci_ref[...], si_ref[...]
    misc = z[:, COL_MISC:COL_MISC + LANE]

    x = z[:, COL_CKV:COL_CKV + KV_LORA]
    mla_ref[:, 0:KV_LORA] = x * lax.rsqrt(jnp.mean(x * x, axis=-1, keepdims=True) + NORM_EPS) * gckv_ref[...]
    mla_ref[:, KV_LORA:] = misc[:, 0:ROPE_A]

    low = lane < HEAD_DIM

    def head_pair(t, gain):
        t2 = t * t
        ms = jnp.where(low, jnp.sum(jnp.where(low, t2, 0.0), axis=1, keepdims=True),
                       jnp.sum(jnp.where(low, 0.0, t2), axis=1, keepdims=True)) * (1.0 / HEAD_DIM)
        return _rotate_pairs(t * lax.rsqrt(ms + NORM_EPS) * gain, cb, sb, lane, HEAD_DIM, HEAD_DIM // 2, HEAD_DIM // 2)

    qb = _head_tiles(z[:, COL_QB:COL_QB + H_B * HEAD_DIM], lambda t: head_pair(t, gqb_ref[...])).astype(BF16)
    qb_ref[...] = jnp.dot(qb, expand_ref[...], preferred_element_type=F32).astype(qb_ref.dtype)

    kb = head_pair(z[:, COL_KB:COL_KB + LANE], gkb_ref[...])
    vb = z[:, COL_VB:COL_VB + LANE]
    kv_ref[:, 0:LANE] = kb
    kv_ref[:, LANE:] = vb
    k2_ref[...] = kb.astype(k2_ref.dtype)
    v2_ref[...] = vb.astype(v2_ref.dtype)

    rot_i = lambda t: _rotate_pairs(t, ci, si, lane, D_I, D_I - D_I_ROPE // 2, D_I_ROPE // 2)
    iq_ref[...] = _head_tiles(z[:, COL_IQ:COL_IQ + H_I * D_I], rot_i).astype(iq_ref.dtype)

    in_ik = (lane >= ROPE_A) & (lane < ROPE_A + D_I)
    ms = jnp.sum(jnp.where(in_ik, misc * misc, 0.0), axis=1, keepdims=True) * (1.0 / D_I)
    ik = pltpu.roll(rot_i(misc * lax.rsqrt(ms + NORM_EPS) * gik_ref[...]), LANE - ROPE_A, 1)
    ik_ref[...] = ik[:, 0:D_I]
    ikb_ref[...] = ik[:, 0:D_I].astype(ikb_ref.dtype)
    iw_ref[...] = pltpu.roll(misc, LANE - ROPE_A - D_I, 1)[:, 0:H_I] * ((H_I ** -0.5) * (D_I ** -0.5))


def _merge_kernel(x_ref, oa_ref, ob_ref, zg_ref, wpa_ref, wpb_ref, wo_ref, o_ref):
    d = x_ref.shape[1]
    y_a = jnp.dot(oa_ref[...], wpa_ref[...], preferred_element_type=F32)
    y_b = jnp.dot(ob_ref[...], wpb_ref[...], preferred_element_type=F32)
    zg = zg_ref[...]
    merged = jax.nn.sigmoid(zg[:, :d]) * y_a + jax.nn.sigmoid(zg[:, d:]) * y_b
    o_ref[...] = x_ref[...] + jnp.dot(merged.astype(BF16), wo_ref[...], preferred_element_type=F32)


def _merge(x, o_a, o_b, z, w_pa, w_pb, w_o, *, tm=512):
    n, d = x.shape
    ka, kb = o_a.shape[1], o_b.shape[1]
    const = lambda shape: pl.BlockSpec(shape, lambda i: (0, 0))
    return pl.pallas_call(
        _merge_kernel,
        out_shape=jax.ShapeDtypeStruct((n, d), F32),
        grid=(n // tm,),
        in_specs=[pl.BlockSpec((tm, d), lambda i: (i, 0)), pl.BlockSpec((tm, ka), lambda i: (i, 0)),
                  pl.BlockSpec((tm, kb), lambda i: (i, 0)), pl.BlockSpec((tm, 2 * d), lambda i: (i, COL_GATE // (2 * d))),
                  const((ka, d)), const((kb, d)), const((d, d))],
        out_specs=pl.BlockSpec((tm, d), lambda i: (i, 0)),
        compiler_params=_params("parallel"),
        name="merge")(x, o_a, o_b, z, w_pa, w_pb, w_o)


def _softmax_step(s, m_prev, l_prev, acc_prev, v_blk):
    m_new = jnp.maximum(m_prev, jnp.max(s, axis=1, keepdims=True))
    alpha = jnp.exp(m_prev - m_new)
    p = jnp.exp(s - m_new)
    l_new = alpha * l_prev + jnp.sum(p, axis=1, keepdims=True)
    acc_new = alpha * acc_prev + jnp.dot(p.astype(BF16), v_blk, preferred_element_type=F32)
    return m_new, l_new, acc_new


def _sortable_key(x):
    b = lax.bitcast_convert_type(x + 0.0, jnp.int32)
    return jnp.where(b < 0, b ^ jnp.int32(0x7FFFFFFF), b)


HALF_RANGE = 1 << 15
KEY_NEG_INF = INT_MIN + 0x7FFFFF


def _has_excess_ties(n_ge, thr, n_sel):
    return jnp.max(jnp.where(n_ge > n_sel, jnp.where(thr > KEY_NEG_INF, 1, 0), 0)) > 0


def _kth_largest_key(count_ge, n_sel, rows):
    def step(s, t):
        bit = lax.shift_left(jnp.int32(1), jnp.int32(31) - s)
        cand = t + bit
        return jnp.where(count_ge(cand) >= n_sel, cand, t)

    return lax.fori_loop(0, 32, step, jnp.full((rows, 1), INT_MIN, jnp.int32))


MLA_HEADS_PER_STEP = 4


def _mla_prompt_kernel(q_ref, k_ref, v_ref, o_ref, *, tq):
    i = pl.program_id(2)
    nh = MLA_HEADS_PER_STEP
    q = q_ref[...]
    causal = lax.broadcasted_iota(jnp.int32, (tq, tq), 1) <= lax.broadcasted_iota(jnp.int32, (tq, tq), 0)

    def chunk(c, carry, diagonal):
        m, l, acc = carry
        start = pl.multiple_of(c * tq, tq)
        k_blk = k_ref[pl.ds(start, tq), :]
        v_blk = v_ref[pl.ds(start, tq), :]
        s = jnp.concatenate([lax.dot_general(q[:, h * LANE:(h + 1) * LANE], k_blk[:, h * LANE:(h + 1) * LANE],
                                             NT_DIMS, preferred_element_type=F32) for h in range(nh)], axis=0)
        if diagonal:
            s = jnp.where(causal[None], s.reshape(nh, tq, tq), MASK_VALUE).reshape(nh * tq, tq)
        m_new = jnp.maximum(m, jnp.max(s, axis=1, keepdims=True))
        alpha = jnp.exp(m - m_new)
        p = jnp.exp(s - m_new)
        l_new = alpha * l + jnp.sum(p, axis=1, keepdims=True)
        p = p.astype(BF16)
        pv = jnp.concatenate([jnp.dot(p[h * tq:(h + 1) * tq], v_blk[:, (h // 2) * LANE:(h // 2 + 1) * LANE],
                                      preferred_element_type=F32) for h in range(nh)], axis=0)
        return m_new, l_new, alpha * acc + pv

    init = (jnp.full((nh * tq, 1), MASK_VALUE, F32), jnp.zeros((nh * tq, 1), F32), jnp.zeros((nh * tq, LANE), F32))
    carry = lax.fori_loop(0, i, functools.partial(chunk, diagonal=False), init)
    _, l, acc = chunk(i, carry, True)
    o = acc / l
    lane = lax.broadcasted_iota(jnp.int32, (tq, LANE), 1)
    pairs = [jnp.where(lane < V_A, o[(2 * g) * tq:(2 * g + 1) * tq], o[(2 * g + 1) * tq:(2 * g + 2) * tq])
             for g in range(nh // 2)]
    o_ref[...] = jnp.concatenate(pairs, axis=1).astype(o_ref.dtype)


def _mla_prompt(q, k, v, *, tq):
    b, t, _ = q.shape
    nh = MLA_HEADS_PER_STEP
    grid = (b, H_A // nh, t // tq)
    return pl.pallas_call(
        functools.partial(_mla_prompt_kernel, tq=tq),
        out_shape=jax.ShapeDtypeStruct((b, t, H_A * V_A), BF16),
        grid=grid,
        in_specs=[pl.BlockSpec((None, tq, nh * LANE), lambda bi, g, i: (bi, i, g)),
                  pl.BlockSpec((None, t, nh * LANE), lambda bi, g, i: (bi, 0, g)),
                  pl.BlockSpec((None, t, nh * V_A), lambda bi, g, i: (bi, 0, g))],
        out_specs=pl.BlockSpec((None, tq, nh * V_A), lambda bi, g, i: (bi, i, g)),
        compiler_params=_params("parallel", "parallel", "arbitrary"),
        name="mla_prompt")(q, k, v)


def _select_bias(key, thr, need, carry, valid, tri_ref):
    w = min(tri_ref.shape[0], key.shape[1])
    tri = tri_ref[0:w, 0:w]
    parts = []
    for t in range(key.shape[1] // w):
        k_t = key[:, t * w:(t + 1) * w]
        eq = k_t == thr
        rank = jnp.dot(jnp.where(eq, 1.0, 0.0).astype(BF16), tri, preferred_element_type=F32) + carry
        carry = jnp.max(rank, axis=1, keepdims=True)
        tie = jnp.where(rank <= need, 0.0, MASK_VALUE)
        parts.append(jnp.where(k_t > thr, 0.0, jnp.where(eq, tie, MASK_VALUE)))
    bias = parts[0] if len(parts) == 1 else jnp.concatenate(parts, axis=1)
    if valid is not None:
        bias = jnp.where(valid, bias, MASK_VALUE)
    return bias, carry


def _dsa_prompt_kernel(iq_ref, w_ref, q_ref, ik_ref, k_ref, v_ref, tri_ref, o_ref, key_ref, bias_ref, wb_ref,
                       hi_ref, lo_ref, *, tq, tk, n_sel):
    i = pl.program_id(1)
    n_c = ((i + 1) * tq + tk - 1) // tk
    q_pos = i * tq + lax.broadcasted_iota(jnp.int32, (tq, tk), 0)
    lane_pos = lax.broadcasted_iota(jnp.int32, (tq, tk), 1)

    w = w_ref[...]
    for h in range(H_I):
        wb_ref[h] = jnp.broadcast_to(w[:, h:h + 1], (tq, LANE))
    iq = iq_ref[...].reshape(H_I * tq, D_I)

    def score_chunk(c, _):
        start = pl.multiple_of(c * tk, tk)
        logit = lax.dot_general(iq, ik_ref[pl.ds(start, tk), :], NT_DIMS, preferred_element_type=F32)
        cols = []
        for t in range(tk // LANE):
            acc = jnp.zeros((tq, LANE), F32)
            for h in range(H_I):
                acc = acc + jnp.maximum(logit[h * tq:(h + 1) * tq, t * LANE:(t + 1) * LANE], 0.0) * wb_ref[h]
            cols.append(acc)
        score = jnp.concatenate(cols, axis=1)
        score = jnp.where(start + lane_pos <= q_pos, score, -jnp.inf)
        key = _sortable_key(score)
        key_ref[c] = key
        hi_ref[c] = lax.shift_right_arithmetic(key, 16).astype(jnp.int16)
        lo_ref[c] = ((key & 0xFFFF) - HALF_RANGE).astype(jnp.int16)
        return 0

    lax.fori_loop(0, n_c, score_chunk, 0)

    def count_ge(cand):
        cand_b = jnp.broadcast_to(cand, (tq, LANE))

        def body(c, cnt):
            key = key_ref[c]
            for t in range(tk // LANE):
                cnt = cnt + jnp.where(key[:, t * LANE:(t + 1) * LANE] >= cand_b, 1, 0)
            return cnt

        cnt = lax.fori_loop(0, n_c, body, jnp.zeros((tq, LANE), jnp.int32))
        return jnp.sum(cnt, axis=1, keepdims=True)

    def count_ge16(ref, cand):
        cand_b = jnp.broadcast_to(cand.astype(jnp.int16), (tq, LANE))

        def body(c, cnt):
            half = ref[c]
            for t in range(tk // LANE):
                cnt = cnt + jnp.where(half[:, t * LANE:(t + 1) * LANE] >= cand_b, jnp.int16(1), jnp.int16(0))
            return cnt

        cnt = lax.fori_loop(0, n_c, body, jnp.zeros((tq, LANE), jnp.int16))
        return jnp.sum(cnt.astype(jnp.int32), axis=1, keepdims=True)

    def kth_largest16(ref, k):
        def step(s, t):
            cand = t + lax.shift_left(jnp.int32(1), jnp.int32(15) - s)
            return jnp.where(count_ge16(ref, cand) >= k, cand, t)

        return lax.fori_loop(0, 16, step, jnp.full((tq, 1), -HALF_RANGE, jnp.int32))

    thr_hi = kth_largest16(hi_ref, n_sel)
    n_above = jnp.where(thr_hi == HALF_RANGE - 1, 0, count_ge16(hi_ref, jnp.minimum(thr_hi + 1, HALF_RANGE - 1)))
    thr_hi16 = jnp.broadcast_to(thr_hi.astype(jnp.int16), (tq, tk))

    def keep_bucket(c, _):
        lo_ref[c] = jnp.where(hi_ref[c] == thr_hi16, lo_ref[c], jnp.int16(-HALF_RANGE))
        return 0

    lax.fori_loop(0, n_c, keep_bucket, 0)
    thr_lo = kth_largest16(lo_ref, n_sel - n_above)
    thr = lax.shift_left(thr_hi, 16) | (thr_lo + HALF_RANGE)

    def bias_by_threshold():
        thr_eff = jnp.maximum(thr, KEY_NEG_INF + 1)

        def body(c, _):
            bias_ref[c] = jnp.where(key_ref[c] >= thr_eff, 0.0, MASK_VALUE)
            return 0

        lax.fori_loop(0, n_c, body, 0)

    def bias_with_ties():
        need = (n_sel - count_ge(thr + 1)).astype(F32)

        def body(c, carry):
            start = pl.multiple_of(c * tk, tk)
            bias, carry = _select_bias(key_ref[c], thr, need, carry, start + lane_pos <= q_pos, tri_ref)
            bias_ref[c] = bias
            return carry

        lax.fori_loop(0, n_c, body, jnp.zeros((tq, 1), F32))

    lax.cond(_has_excess_ties(count_ge(thr), thr, n_sel), bias_with_ties, bias_by_threshold)

    rows = H_B * tq
    qa = q_ref[...].reshape(rows, LANE)

    def attend(c, carry):
        start = pl.multiple_of(c * tk, tk)
        s = lax.dot_general(qa, k_ref[pl.ds(start, tk), :], NT_DIMS, preferred_element_type=F32)
        s = (s.reshape(H_B, tq, tk) + bias_ref[c][None]).reshape(rows, tk)
        return _softmax_step(s, *carry, v_ref[pl.ds(start, tk), :])

    m, l, acc = lax.fori_loop(0, n_c, attend, (jnp.full((rows, 1), MASK_VALUE, F32),
                                               jnp.zeros((rows, 1), F32), jnp.zeros((rows, LANE), F32)))
    o_ref[...] = (acc / l).reshape(H_B, tq, LANE).astype(o_ref.dtype)


def _dsa_prompt(iq, iw, q, ik, k, v, *, tq, tk, n_sel):
    b, nq = iq.shape[:2]
    t = ik.shape[1]
    n_chunks = t // tk
    tri_w = 2 * LANE if tk % (2 * LANE) == 0 else LANE
    tri = jnp.asarray(np.arange(tri_w)[:, None] <= np.arange(tri_w)[None, :], BF16)
    return pl.pallas_call(
        functools.partial(_dsa_prompt_kernel, tq=tq, tk=tk, n_sel=n_sel),
        out_shape=jax.ShapeDtypeStruct((b, nq, H_B, tq, LANE), BF16),
        grid=(b, nq),
        in_specs=[pl.BlockSpec((None, None, H_I, tq, D_I), lambda bi, i: (bi, i, 0, 0, 0)),
                  pl.BlockSpec((None, tq, H_I), lambda bi, i: (bi, i, 0)),
                  pl.BlockSpec((None, None, H_B, tq, LANE), lambda bi, i: (bi, i, 0, 0, 0)),
                  pl.BlockSpec((None, t, D_I), lambda bi, i: (bi, 0, 0)),
                  pl.BlockSpec((None, t, LANE), lambda bi, i: (bi, 0, 0)),
                  pl.BlockSpec((None, t, LANE), lambda bi, i: (bi, 0, 0)),
                  pl.BlockSpec((tri_w, tri_w), lambda bi, i: (0, 0))],
        out_specs=pl.BlockSpec((None, None, H_B, tq, LANE), lambda bi, i: (bi, i, 0, 0, 0)),
        scratch_shapes=[pltpu.VMEM((n_chunks, tq, tk), jnp.int32),
                        pltpu.VMEM((n_chunks, tq, tk), F32),
                        pltpu.VMEM((H_I, tq, LANE), F32),
                        pltpu.VMEM((n_chunks, tq, tk), jnp.int16),
                        pltpu.VMEM((n_chunks, tq, tk), jnp.int16)],
        compiler_params=_params("parallel", "arbitrary"),
        name="dsa_prompt")(iq, iw, q, ik, k, v, tri)


def _seq_spec(shape):
    return pl.BlockSpec((None,) + shape, lambda b, j, pt: (b,) + tuple(0 for _ in shape))


def _const_spec(shape):
    return pl.BlockSpec(shape, lambda b, j, pt: tuple(0 for _ in shape))


HBM_SPEC = pl.BlockSpec(memory_space=pl.ANY)


def _page_copy(cache_hbm, buf, sem, layer, page, slot, p):
    return pltpu.make_async_copy(cache_hbm.at[layer, page], buf.at[slot, :, pl.ds(p * PAGE_SIZE, PAGE_SIZE)],
                                 sem.at[slot])


def _stream_pages(pt_ref, cache_hbm, buf, sem, *, layer, pages):
    b = pl.program_id(0)
    j = pl.program_id(1)
    n_seq = pl.num_programs(0)
    n_steps = pl.num_programs(1)
    t = b * n_steps + j
    slot = lax.rem(t, 2)

    def start(seq, step, dst_slot):
        for p in range(pages):
            _page_copy(cache_hbm, buf, sem, layer, pt_ref[seq, step * pages + p], dst_slot, p).start()

    @pl.when(t == 0)
    def _():
        start(b, j, slot)

    @pl.when(t + 1 < n_seq * n_steps)
    def _():
        wrap = j + 1 == n_steps
        start(jnp.where(wrap, b + 1, b), jnp.where(wrap, 0, j + 1), 1 - slot)

    for p in range(pages):
        _page_copy(cache_hbm, buf, sem, layer, 0, slot, p).wait()
    return slot


def _page_scratch(pages, width):
    return [pltpu.VMEM((2, width, pages * PAGE_SIZE), F32), pltpu.SemaphoreType.DMA((2,))]


def _softmax_step_t(s, m_ref, l_ref, acc_ref, v_t):
    m_prev = m_ref[...]
    m_new = jnp.maximum(m_prev, jnp.max(s, axis=1, keepdims=True))
    alpha = jnp.exp(m_prev - m_new)
    p = jnp.exp(s - m_new)
    l_ref[...] = alpha * l_ref[...] + jnp.sum(p, axis=1, keepdims=True)
    acc_ref[...] = alpha * acc_ref[...] + lax.dot_general(p.astype(BF16), v_t, NT_DIMS, preferred_element_type=F32)
    m_ref[...] = m_new


def _init_softmax(m_ref, l_ref, acc_ref):
    m_ref[...] = jnp.full(m_ref.shape, MASK_VALUE, F32)
    l_ref[...] = jnp.zeros(l_ref.shape, F32)
    acc_ref[...] = jnp.zeros(acc_ref.shape, F32)


def _mla_sample_kernel(pt_ref, cache_hbm, wt_ref, qbd_ref, qr_ref, qs_ref, cg_ref, sg_ref, new_ref, cg_new_ref,
                       sg_new_ref, o_ref, buf, sem, lhs_ref, m_ref, l_ref, acc_ref, *, layer, pages, n_new):
    j = pl.program_id(1)
    rows = o_ref.shape[0]
    n_k = H_A * NOPE_A
    slot = _stream_pages(pt_ref, cache_hbm, buf, sem, layer=layer, pages=pages)

    @pl.when(j == 0)
    def _():
        wt = wt_ref[...]
        lhs_ref[0:n_k, :] = wt
        lhs_ref[n_k:, :] = jnp.dot(qbd_ref[...], wt, preferred_element_type=F32).astype(BF16)
        _init_softmax(m_ref, l_ref, acc_ref)

    def attend(c_t, pe_t, cg, sg, mask):
        n = c_t.shape[1]
        big = jnp.dot(lhs_ref[...], c_t, preferred_element_type=F32)
        ssq = jnp.zeros((H_A, n), F32)
        row_id = lax.broadcasted_iota(jnp.int32, (H_A, n), 0)
        for h in range(H_A):
            kh = big[h * NOPE_A:(h + 1) * NOPE_A]
            ssq = jnp.where(row_id == h, jnp.sum(kh * kh, axis=0, keepdims=True), ssq)
        ssq = ssq + jnp.sum(pe_t * pe_t, axis=0, keepdims=True)
        inv_rms = lax.rsqrt(ssq * (1.0 / DQK_A) + NORM_EPS)
        s_rope = (jnp.dot(qr_ref[...], (pe_t * cg).astype(BF16), preferred_element_type=F32)
                  + jnp.dot(qs_ref[...], (pe_t * sg).astype(BF16), preferred_element_type=F32))
        s = ((big[n_k:] + s_rope).reshape(rows // H_A, H_A, n) * inv_rms[None]).reshape(rows, n)
        if mask is not None:
            s = jnp.where(mask, s, MASK_VALUE)
        _softmax_step_t(s, m_ref, l_ref, acc_ref, c_t)

    attend(buf[slot, 0:KV_LORA, :].astype(BF16), buf[slot, KV_LORA:, :], cg_ref[...], sg_ref[...], None)

    @pl.when(j == pl.num_programs(1) - 1)
    def _():
        new = new_ref[...]
        n = new.shape[1]
        key_i = lax.broadcasted_iota(jnp.int32, (rows, n), 1)
        q_i = lax.broadcasted_iota(jnp.int32, (rows, n), 0) // H_A
        attend(new[:KV_LORA].astype(BF16), new[KV_LORA:], cg_new_ref[...], sg_new_ref[...],
               (key_i <= q_i) & (key_i < n_new))
        o_ref[...] = acc_ref[...] / l_ref[...]


def _mla_sample(layer, page_table, cache_t, wt, qbd, qr, qs, cg_t, sg_t, new_t, cg_new_t, sg_new_t, *, pages, n_new):
    n_seq, n_pages = page_table.shape
    rows = qbd.shape[1]
    n_main = pages * PAGE_SIZE
    n_tail = new_t.shape[2]
    width = KV_LORA + ROPE_A
    in_specs = [HBM_SPEC,
                _const_spec((H_A * NOPE_A, KV_LORA)),
                _seq_spec((rows, H_A * NOPE_A)), _seq_spec((rows, ROPE_A)), _seq_spec((rows, ROPE_A)),
                pl.BlockSpec((ROPE_A, n_main), lambda b, j, pt: (0, j)),
                pl.BlockSpec((ROPE_A, n_main), lambda b, j, pt: (0, j)),
                _seq_spec((width, n_tail)), _const_spec((ROPE_A, n_tail)), _const_spec((ROPE_A, n_tail))]
    return pl.pallas_call(
        functools.partial(_mla_sample_kernel, layer=layer, pages=pages, n_new=n_new),
        out_shape=jax.ShapeDtypeStruct((n_seq, rows, KV_LORA), F32),
        grid_spec=pltpu.PrefetchScalarGridSpec(
            num_scalar_prefetch=1, grid=(n_seq, n_pages // pages), in_specs=in_specs,
            out_specs=pl.BlockSpec((None, rows, KV_LORA), lambda b, j, pt: (b, 0, 0)),
            scratch_shapes=_page_scratch(pages, width) + [
                pltpu.VMEM((H_A * NOPE_A + rows, KV_LORA), BF16),
                pltpu.VMEM((rows, 1), F32), pltpu.VMEM((rows, 1), F32), pltpu.VMEM((rows, KV_LORA), F32)]),
        compiler_params=_params("arbitrary", "arbitrary"),
        name="mla_sample")(page_table, cache_t, wt, qbd, qr, qs, cg_t, sg_t, new_t, cg_new_t, sg_new_t)


def _dsa_select_kernel(pt_ref, cache_hbm, iq_ref, w_ref, ik_new_ref, tri_ref, bias_ref, bias_new_ref, buf, sem,
                       key_ref, key_new_ref, *, layer, pages, n_new, n_sel):
    j = pl.program_id(1)
    n_steps = key_ref.shape[0]
    n_main = key_ref.shape[2]
    n_tail = ik_new_ref.shape[1]
    nq = 8
    slot = _stream_pages(pt_ref, cache_hbm, buf, sem, layer=layer, pages=pages)

    def scores(ik_t):
        logit = jnp.dot(iq_ref[...], ik_t, preferred_element_type=F32)
        weighted = jnp.maximum(logit, 0.0) * w_ref[...]
        return jnp.sum(weighted.reshape(H_I, nq, ik_t.shape[1]), axis=0)

    key_ref[j] = _sortable_key(scores(buf[slot].astype(BF16)))

    @pl.when(j == n_steps - 1)
    def _():
        key_i = lax.broadcasted_iota(jnp.int32, (nq, n_tail), 1)
        q_i = lax.broadcasted_iota(jnp.int32, (nq, n_tail), 0) % n_new
        valid_new = (key_i <= q_i) & (key_i < n_new)
        key_new_ref[...] = _sortable_key(jnp.where(valid_new, scores(ik_new_ref[...]), -jnp.inf))

        def count_ge(cand):
            cand_b = jnp.broadcast_to(cand, (nq, LANE))
            cnts = [jnp.zeros((nq, LANE), jnp.int32) for _ in range(4)]
            tiles = [key_ref[c, :, t * LANE:(t + 1) * LANE] for c in range(n_steps) for t in range(n_main // LANE)]
            tiles += [key_new_ref[:, t * LANE:(t + 1) * LANE] for t in range(n_tail // LANE)]
            for idx, tile in enumerate(tiles):
                cnts[idx % 4] = cnts[idx % 4] + jnp.where(tile >= cand_b, 1, 0)
            return jnp.sum((cnts[0] + cnts[1]) + (cnts[2] + cnts[3]), axis=1, keepdims=True)

        thr = _kth_largest_key(count_ge, n_sel, nq)

        def bias_by_threshold():
            thr_eff = jnp.maximum(thr, KEY_NEG_INF + 1)

            def body(c, _):
                bias_ref[c] = jnp.where(key_ref[c] >= thr_eff, 0.0, MASK_VALUE)
                return 0

            lax.fori_loop(0, n_steps, body, 0)
            bias_new_ref[...] = jnp.where(key_new_ref[...] >= thr_eff, 0.0, MASK_VALUE)

        def bias_with_ties():
            need = (n_sel - count_ge(thr + 1)).astype(F32)

            def body(c, carry):
                bias, carry = _select_bias(key_ref[c], thr, need, carry, None, tri_ref)
                bias_ref[c] = bias
                return carry

            carry = lax.fori_loop(0, n_steps, body, jnp.zeros((nq, 1), F32))
            bias_new, _ = _select_bias(key_new_ref[...], thr, need, carry, valid_new, tri_ref)
            bias_new_ref[...] = bias_new

        lax.cond(_has_excess_ties(count_ge(thr), thr, n_sel), bias_with_ties, bias_by_threshold)


def _dsa_select(layer, page_table, cache_idx_t, iq, iw, ik_new_t, *, pages, n_new, n_sel):
    n_seq, n_pages = page_table.shape
    n_steps = n_pages // pages
    n_main = pages * PAGE_SIZE
    n_tail = ik_new_t.shape[2]
    rows = iq.shape[1]
    tri = jnp.asarray(np.arange(2 * LANE)[:, None] <= np.arange(2 * LANE)[None, :], BF16)
    in_specs = [HBM_SPEC, _seq_spec((rows, D_I)), _seq_spec((rows, 1)), _seq_spec((D_I, n_tail)),
                _const_spec((2 * LANE, 2 * LANE))]
    return pl.pallas_call(
        functools.partial(_dsa_select_kernel, layer=layer, pages=pages, n_new=n_new, n_sel=n_sel),
        out_shape=(jax.ShapeDtypeStruct((n_seq, n_steps, 8, n_main), F32),
                   jax.ShapeDtypeStruct((n_seq, 8, n_tail), F32)),
        grid_spec=pltpu.PrefetchScalarGridSpec(
            num_scalar_prefetch=1, grid=(n_seq, n_steps), in_specs=in_specs,
            out_specs=(pl.BlockSpec((None, n_steps, 8, n_main), lambda b, j, pt: (b, 0, 0, 0)),
                       pl.BlockSpec((None, 8, n_tail), lambda b, j, pt: (b, 0, 0))),
            scratch_shapes=_page_scratch(pages, D_I) + [
                pltpu.VMEM((n_steps, 8, n_main), jnp.int32), pltpu.VMEM((8, n_tail), jnp.int32)]),
        compiler_params=_params("arbitrary", "arbitrary"),
        name="dsa_select")(page_table, cache_idx_t, iq, iw, ik_new_t, tri)


def _dsa_sample_kernel(pt_ref, cache_hbm, q_ref, bias_ref, kv_new_ref, bias_new_ref, o_ref, buf, sem,
                       m_ref, l_ref, acc_ref, *, layer, pages):
    j = pl.program_id(1)
    rows = q_ref.shape[0]
    kw = KVH_B * HEAD_DIM
    slot = _stream_pages(pt_ref, cache_hbm, buf, sem, layer=layer, pages=pages)

    @pl.when(j == 0)
    def _():
        _init_softmax(m_ref, l_ref, acc_ref)

    def attend(k_t, v_t, bias):
        n = k_t.shape[1]
        s = jnp.dot(q_ref[...], k_t, preferred_element_type=F32)
        s = (s.reshape(rows // 8, 8, n) + bias[None]).reshape(rows, n)
        _softmax_step_t(s, m_ref, l_ref, acc_ref, v_t)

    attend(buf[slot, 0:kw, :].astype(BF16), buf[slot, kw:, :].astype(BF16), bias_ref[...])

    @pl.when(j == pl.num_programs(1) - 1)
    def _():
        kv_new = kv_new_ref[...]
        attend(kv_new[:kw], kv_new[kw:], bias_new_ref[...])
        o_ref[...] = acc_ref[...] / l_ref[...]


def _dsa_sample(layer, page_table, cache_kv_t, q, bias, kv_new_t, bias_new, *, pages):
    n_seq, n_pages = page_table.shape
    n_steps = n_pages // pages
    n_main = pages * PAGE_SIZE
    rows = q.shape[1]
    n_tail = kv_new_t.shape[2]
    kw = KVH_B * HEAD_DIM
    in_specs = [HBM_SPEC, _seq_spec((rows, kw)),
                pl.BlockSpec((None, None, 8, n_main), lambda b, j, pt: (b, j, 0, 0)),
                _seq_spec((2 * kw, n_tail)), _seq_spec((8, n_tail))]
    return pl.pallas_call(
        functools.partial(_dsa_sample_kernel, layer=layer, pages=pages),
        out_shape=jax.ShapeDtypeStruct((n_seq, rows, kw), F32),
        grid_spec=pltpu.PrefetchScalarGridSpec(
            num_scalar_prefetch=1, grid=(n_seq, n_steps), in_specs=in_specs,
            out_specs=pl.BlockSpec((None, rows, kw), lambda b, j, pt: (b, 0, 0)),
            scratch_shapes=_page_scratch(pages, 2 * kw) + [
                pltpu.VMEM((rows, 1), F32), pltpu.VMEM((rows, 1), F32), pltpu.VMEM((rows, kw), F32)]),
        compiler_params=_params("arbitrary", "arbitrary"),
        name="dsa_sample")(page_table, cache_kv_t, q, bias, kv_new_t, bias_new)


def _router_kernel(x_ref, g_ref, w_ref, b_ref, logit_ref):
    x = x_ref[...]
    ms = jnp.mean(x * x, axis=-1, keepdims=True)
    h = x * lax.rsqrt(ms + NORM_EPS) * g_ref[...]
    logit_ref[...] = jnp.dot(h, w_ref[...], preferred_element_type=F32, precision=lax.Precision.HIGHEST) + b_ref[...]


def _router(x, gain, w_router, b_router, *, tm=512):
    n, d = x.shape
    e = w_router.shape[1]
    return pl.pallas_call(
        _router_kernel,
        out_shape=jax.ShapeDtypeStruct((n, e), F32),
        grid=(n // tm,),
        in_specs=[pl.BlockSpec((tm, d), lambda i: (i, 0)), pl.BlockSpec((1, d), lambda i: (0, 0)),
                  pl.BlockSpec((d, e), lambda i: (0, 0)), pl.BlockSpec((1, e), lambda i: (0, 0))],
        out_specs=pl.BlockSpec((tm, e), lambda i: (i, 0)),
        compiler_params=_params("parallel"),
        name="router")(x, gain.reshape(1, d), w_router, b_router.reshape(1, e))


def _expert_kernel(be_ref, bv_ref, x_ref, g_ref, w1_ref, b1_ref, w2_ref, b2_ref, o_ref):
    i = pl.program_id(0)

    @pl.when(bv_ref[i] != 0)
    def _():
        d_ff = w2_ref.shape[0]
        x = x_ref[...]
        ms = jnp.mean(x * x, axis=-1, keepdims=True)
        h = (x * lax.rsqrt(ms + NORM_EPS) * g_ref[...]).astype(BF16)
        u = jnp.dot(h, w1_ref[...].astype(BF16), preferred_element_type=F32) + b1_ref[...]
        g = jnp.minimum(u[:, :d_ff], SWIGLU_LIMIT)
        up = jnp.clip(u[:, d_ff:], -SWIGLU_LIMIT, SWIGLU_LIMIT)
        a = g * jax.nn.sigmoid(SWIGLU_ALPHA * g) * (up + 1.0)
        o_ref[...] = jnp.dot(a.astype(BF16), w2_ref[...].astype(BF16), preferred_element_type=F32) + b2_ref[...]

    @pl.when(bv_ref[i] == 0)
    def _():
        o_ref[...] = jnp.zeros(o_ref.shape, o_ref.dtype)


def _experts(block_e, block_valid, xb, gain, w1, b1, w2, b2, *, tm):
    n_rows, d = xb.shape
    e, _, two_ff = w1.shape
    d_ff = w2.shape[1]
    return pl.pallas_call(
        _expert_kernel,
        out_shape=jax.ShapeDtypeStruct((n_rows, d), F32),
        grid_spec=pltpu.PrefetchScalarGridSpec(
            num_scalar_prefetch=2, grid=(n_rows // tm,),
            in_specs=[pl.BlockSpec((tm, d), lambda i, be, bv: (i, 0)),
                      pl.BlockSpec((1, d), lambda i, be, bv: (0, 0)),
                      pl.BlockSpec((None, d, two_ff), lambda i, be, bv: (be[i], 0, 0)),
                      pl.BlockSpec((None, 1, two_ff), lambda i, be, bv: (be[i], 0, 0)),
                      pl.BlockSpec((None, d_ff, d), lambda i, be, bv: (be[i], 0, 0)),
                      pl.BlockSpec((None, 1, d), lambda i, be, bv: (be[i], 0, 0))],
            out_specs=pl.BlockSpec((tm, d), lambda i, be, bv: (i, 0))),
        compiler_params=_params("arbitrary"),
        name="experts")(block_e, block_valid, xb, gain.reshape(1, d), w1, b1.reshape(e, 1, two_ff), w2,
                        b2.reshape(e, 1, d))


def _moe(x, gain, w_router, b_router, w1, b1, w2, b2, *, tm_e=256):
    n, d = x.shape
    logits = _router(x, gain, w_router, b_router)
    top_val, top_idx = lax.top_k(logits, TOP_K)
    gate = jax.nn.softmax(top_val, axis=-1)
    flat_e = top_idx.reshape(-1).astype(jnp.int32)
    nk = flat_e.shape[0]
    order = jnp.argsort(flat_e, stable=True).astype(jnp.int32)
    e_sorted = flat_e[order]
    experts = jnp.arange(N_EXPERTS, dtype=jnp.int32)
    counts = jnp.sum((flat_e[None, :] == experts[:, None]).astype(jnp.int32), axis=1)
    start = jnp.cumsum(counts) - counts
    padded = _cdiv(counts, tm_e) * tm_e
    pad_end = jnp.cumsum(padded)
    pad_start = pad_end - padded
    n_blocks = _cdiv(nk, tm_e) + N_EXPERTS
    block_start = jnp.arange(n_blocks, dtype=jnp.int32) * tm_e
    block_e = jnp.minimum(jnp.sum((block_start[:, None] >= pad_end[None, :]).astype(jnp.int32), axis=1), N_EXPERTS - 1)
    block_valid = (block_start < pad_end[-1]).astype(jnp.int32)
    row = jnp.arange(n_blocks * tm_e, dtype=jnp.int32)
    row_e = jnp.repeat(block_e, tm_e)
    offset = row - pad_start[row_e]
    src = jnp.clip(start[row_e] + offset, 0, nk - 1)
    row_tok = jnp.where(offset < counts[row_e], order[src] // TOP_K, 0)
    dest_sorted = pad_start[e_sorted] + jnp.arange(nk, dtype=jnp.int32) - start[e_sorted]
    dest = dest_sorted[jnp.argsort(order)]
    yb = _experts(block_e, block_valid, x[row_tok], gain, w1, b1, w2, b2, tm=tm_e)
    dest_k = dest.reshape(n, TOP_K)
    y = x
    for k in range(TOP_K):
        y = y + yb[dest_k[:, k]] * gate[:, k:k + 1]
    return y


def _rms(x, g, n=None):
    n = x.shape[-1] if n is None else n
    y = x * lax.rsqrt(jnp.sum(x * x, axis=-1, keepdims=True) / n + NORM_EPS)
    return y * g


def _rope_tables(pos, n_rot):
    half = n_rot // 2
    inv_freq = ROPE_THETA ** (-jnp.arange(half, dtype=F32) / half)
    ang = pos.astype(F32)[:, None] * inv_freq[None, :]
    return jnp.cos(ang), jnp.sin(ang)


def _rope(x, cos, sin, n_rot):
    d = x.shape[-1]
    half = n_rot // 2
    cos = cos[:, None, :]
    sin = sin[:, None, :]
    x1 = x[..., d - n_rot:d - half]
    x2 = x[..., d - half:]
    return jnp.concatenate([x[..., :d - n_rot], x1 * cos - x2 * sin, x2 * cos + x1 * sin], axis=-1)


def _regroup_w_in(w_in):
    cuts = np.cumsum([0, Q_LORA, KV_LORA, ROPE_A, H_B * HEAD_DIM, KVH_B * HEAD_DIM, KVH_B * HEAD_DIM,
                      H_I * D_I, D_I, H_I, 2 * w_in.shape[0]])
    w_cq, w_ckv, w_kpe, w_qb, w_kb, w_vb, w_iq, w_ik, w_iw, w_g = [w_in[:, a:b] for a, b in zip(cuts[:-1], cuts[1:])]
    misc_pad = jnp.zeros((w_in.shape[0], COL_GATE - COL_MISC - ROPE_A - D_I - H_I), w_in.dtype)
    return jnp.concatenate([w_cq, w_ckv, w_qb, w_kb, w_vb, w_iq, w_kpe, w_ik, w_iw, misc_pad, w_g], axis=1).astype(BF16)


def _layer(l, xp, xs, page_table, caches, wts, *, t_real):
    (cache_mla, cache_dsa_kv, cache_dsa_idx) = caches
    n_b, t_pad, d = xp.shape
    n_seq, n_ds, _ = xs.shape
    past = page_table.shape[1] * PAGE_SIZE
    n_p = n_b * t_pad
    n_s = n_seq * n_ds
    n_sel_p = min(MAX_SEL, (t_real - N_META) // 4)
    n_sel_s = min(MAX_SEL, (past + n_ds) // 4)

    x_all = jnp.concatenate([xp.reshape(n_p, d), xs.reshape(n_s, d)], axis=0)
    n_all = _round_up(n_p + n_s, 1024)
    x_all = jnp.pad(x_all, ((0, n_all - n_p - n_s), (0, 0)))
    pos = jnp.concatenate([jnp.tile(jnp.arange(t_pad), n_b), jnp.tile(past + jnp.arange(n_ds), n_seq),
                           jnp.zeros((n_all - n_p - n_s,), jnp.int32)])
    z = _matmul(x_all, _regroup_w_in(wts["w_in"][l]), gain=wts["g_attn_norm"][l], tm=1024, tn=1024, name="proj_in")

    cos_a, sin_a = _rope_tables(pos, ROPE_A)
    cos_b, sin_b = _rope_tables(pos, HEAD_DIM)
    cos_i, sin_i = _rope_tables(pos, D_I_ROPE)
    ones = lambda w: jnp.ones((n_all, w), F32)
    zeros = lambda w: jnp.zeros((n_all, w), F32)
    cos_a_t = jnp.concatenate([ones(NOPE_A), cos_a, cos_a, zeros(LANE - DQK_A)], axis=1)
    sin_a_t = jnp.concatenate([zeros(NOPE_A), -sin_a, sin_a, zeros(LANE - DQK_A)], axis=1)
    cos_b_t = jnp.tile(jnp.concatenate([cos_b, cos_b], axis=1), (1, LANE // HEAD_DIM))
    sin_b_t = jnp.tile(jnp.concatenate([-sin_b, sin_b], axis=1), (1, LANE // HEAD_DIM))
    cos_i_t = jnp.tile(jnp.concatenate([ones(D_I - D_I_ROPE), cos_i, cos_i], axis=1), (1, LANE // D_I))
    sin_i_t = jnp.tile(jnp.concatenate([zeros(D_I - D_I_ROPE), -sin_i, sin_i], axis=1), (1, LANE // D_I))
    pad_gain = lambda g: jnp.pad(g, (0, LANE - DQK_A)).reshape(1, LANE)
    tm = 512

    w_uq = jnp.pad(wts["w_uq"][l], ((0, 0), (0, 0), (0, LANE - DQK_A))).reshape(Q_LORA, H_A * LANE).astype(BF16)
    (q_a,) = _row_call(_q_up_kernel, n_all, tm,
                       [(z, Q_LORA, COL_CQ // Q_LORA), (wts["g_cq"][l].reshape(1, Q_LORA), None, 0), (w_uq, None, 0),
                        (pad_gain(wts["g_qa"][l] * (DQK_A ** -0.5)), None, 0), (cos_a_t, LANE, 0), (sin_a_t, LANE, 0)],
                       [(H_A * LANE, BF16)], "q_up")
    sl = slice(n_p, n_p + n_s)
    tm_s = min(tm, n_s)
    (q_a_s,) = _row_call(_q_up_kernel, n_s, tm_s,
                         [(z[sl], Q_LORA, COL_CQ // Q_LORA), (wts["g_cq"][l].reshape(1, Q_LORA), None, 0), (w_uq, None, 0),
                          (pad_gain(wts["g_qa"][l] * (DQK_A ** -0.5)), None, 0), (cos_a_t[sl], LANE, 0),
                          (sin_a_t[sl], LANE, 0)],
                         [(H_A * LANE, F32)], "q_up_sample")

    w_uk = jnp.pad(wts["w_uk"][l], ((0, 0), (0, 0), (0, LANE - NOPE_A))).reshape(KV_LORA, H_A * LANE)
    w_kv = jnp.concatenate([w_uk, wts["w_uv"][l].reshape(KV_LORA, H_A * V_A)], axis=1).astype(BF16)
    place = np.zeros((LANE, H_A * LANE), np.float32)
    for h in range(H_A):
        place[np.arange(ROPE_A), h * LANE + NOPE_A + np.arange(ROPE_A)] = 1.0
    k_a, v_a = _row_call(_kv_up_kernel, n_all, tm,
                         [(z, KV_LORA, COL_CKV // KV_LORA), (wts["g_ckv"][l].reshape(1, KV_LORA), None, 0),
                          (z, LANE, COL_MISC // LANE), (w_kv, None, 0), (jnp.asarray(place, BF16), None, 0),
                          (pad_gain(wts["g_ka"][l]), None, 0), (cos_a_t, LANE, 0), (sin_a_t, LANE, 0)],
                         [(H_A * LANE, BF16), (H_A * V_A, BF16)], "kv_up")

    expand = np.zeros((H_B * HEAD_DIM, H_B * KVH_B * HEAD_DIM), np.float32)
    for h in range(H_B):
        cols = h * KVH_B * HEAD_DIM + (h // GROUP_B) * HEAD_DIM + np.arange(HEAD_DIM)
        expand[h * HEAD_DIM + np.arange(HEAD_DIM), cols] = 1.0
    two = lambda g: jnp.tile(g, LANE // HEAD_DIM).reshape(1, LANE)
    g_ik_t = jnp.pad(wts["g_ik"][l], (ROPE_A, LANE - ROPE_A - D_I)).reshape(1, LANE)
    mla_rows, kv_rows, i_k, i_w, q_b_pad, k_b2, v_b2, i_q, i_k_bf = _row_call(
        _post_z_kernel, n_all, tm,
        [(z, COL_GATE, 0), (wts["g_ckv"][l].reshape(1, KV_LORA), None, 0),
         (two(wts["g_qb"][l] * (HEAD_DIM ** -0.5)), None, 0), (two(wts["g_kb"][l]), None, 0), (g_ik_t, None, 0),
         (cos_b_t, LANE, 0), (sin_b_t, LANE, 0), (cos_i_t, LANE, 0), (sin_i_t, LANE, 0),
         (jnp.asarray(expand, BF16), None, 0)],
        [(KV_LORA + ROPE_A, F32), (2 * KVH_B * HEAD_DIM, F32), (D_I, F32), (H_I, F32), (H_B * LANE, BF16),
         (KVH_B * HEAD_DIM, BF16), (KVH_B * HEAD_DIM, BF16), (H_I * D_I, BF16), (D_I, BF16)], "post_z")
    q_b_pad = q_b_pad.reshape(n_all, H_B, KVH_B * HEAD_DIM)
    i_q = i_q.reshape(n_all, H_I, D_I)

    tq = 128
    nq = t_pad // tq
    n_kc = max(dd for dd in range(1, 5) if nq % dd == 0)
    tk = n_kc * tq
    o_a_p = _mla_prompt(q_a[:n_p].reshape(n_b, t_pad, H_A * LANE), k_a[:n_p].reshape(n_b, t_pad, H_A * LANE),
                        v_a[:n_p].reshape(n_b, t_pad, H_A * V_A), tq=tk)

    blocked = lambda a: a.reshape((n_b, nq, tq) + a.shape[1:]).swapaxes(2, 3)
    o_b_p = _dsa_prompt(blocked(i_q[:n_p]), i_w[:n_p].reshape(n_b, t_pad, H_I), blocked(q_b_pad[:n_p]),
                        i_k_bf[:n_p].reshape(n_b, t_pad, D_I), k_b2[:n_p].reshape(n_b, t_pad, KVH_B * HEAD_DIM),
                        v_b2[:n_p].reshape(n_b, t_pad, KVH_B * HEAD_DIM), tq=tq, tk=tk, n_sel=n_sel_p)
    o_b_p = o_b_p.reshape(n_b, nq, KVH_B, GROUP_B, tq, KVH_B, HEAD_DIM)
    o_b_p = jnp.stack([o_b_p[:, :, g, :, :, g, :] for g in range(KVH_B)], axis=2)
    o_b_p = o_b_p.reshape(n_b, nq, H_B, tq, HEAD_DIM).swapaxes(2, 3).reshape(n_p, H_B * HEAD_DIM)

    pages = max(dd for dd in range(1, MAX_PAGES_PER_STEP + 1) if page_table.shape[1] % dd == 0)
    assert 8 % n_ds == 0
    n_tail = LANE
    tail_t = lambda a: jnp.pad(a.reshape(n_seq, n_ds, a.shape[-1]), ((0, 0), (0, n_tail - n_ds), (0, 0))).swapaxes(1, 2)
    cache_mla_t = jnp.swapaxes(cache_mla, 2, 3)
    cache_idx_t = jnp.swapaxes(cache_dsa_idx, 2, 3)
    cache_kv_t = jnp.moveaxis(cache_dsa_kv, 2, -1).reshape(cache_dsa_kv.shape[:2] + (2 * KVH_B * HEAD_DIM, PAGE_SIZE))
    g_ka = wts["g_ka"][l]
    pos_k = jnp.arange(past + n_tail)
    cos_k, sin_k = _rope_tables(pos_k, ROPE_A)
    g1, g2 = g_ka[NOPE_A:NOPE_A + ROPE_A // 2], g_ka[NOPE_A + ROPE_A // 2:]
    cg_t = jnp.concatenate([cos_k * g1, cos_k * g2], axis=1).T
    sg_t = jnp.concatenate([sin_k * g1, -sin_k * g2], axis=1).T
    qa_s = q_a_s.reshape(n_s, H_A, LANE)
    rows_a = n_ds * H_A
    eye_h = jnp.eye(H_A, dtype=F32)
    qn_s = qa_s[..., :NOPE_A] * g_ka[:NOPE_A]
    qbd = (qn_s[:, :, None, :] * eye_h[None, :, :, None]).reshape(n_seq, rows_a, H_A * NOPE_A)
    q_r = qa_s[:, :, NOPE_A:DQK_A].reshape(n_seq, rows_a, ROPE_A)
    q_r_swap = jnp.concatenate([q_r[..., ROPE_A // 2:], q_r[..., :ROPE_A // 2]], axis=-1)
    wt = wts["w_uk"][l].reshape(KV_LORA, H_A * NOPE_A).T
    o_lat = _mla_sample(l, page_table, cache_mla_t, wt.astype(BF16), qbd.astype(BF16), q_r.astype(BF16),
                        q_r_swap.astype(BF16), cg_t[:, :past], sg_t[:, :past], tail_t(mla_rows[sl]),
                        cg_t[:, past:], sg_t[:, past:], pages=pages, n_new=n_ds)
    o_a_s = jnp.einsum("bqhr,rhv->bqhv", o_lat.reshape(n_seq, n_ds, H_A, KV_LORA), wts["w_uv"][l])
    o_a_s = o_a_s.reshape(n_s, H_A * V_A)

    slot = jnp.arange(8) % n_ds
    iq_s = i_q[sl].reshape(n_seq, n_ds, H_I, D_I)[:, slot].swapaxes(1, 2).reshape(n_seq, H_I * 8, D_I)
    iw_s = i_w[sl].reshape(n_seq, n_ds, H_I)[:, slot].swapaxes(1, 2).reshape(n_seq, H_I * 8, 1)
    bias, bias_new = _dsa_select(l, page_table, cache_idx_t, iq_s, iw_s, tail_t(i_k_bf[sl]), pages=pages,
                                 n_new=n_ds, n_sel=n_sel_s)
    qb_s = q_b_pad[sl].reshape(n_seq, n_ds, H_B, KVH_B * HEAD_DIM).swapaxes(1, 2).reshape(n_seq, H_B * n_ds, -1)
    kv_new_t = tail_t(kv_rows[sl]).astype(BF16)
    o_b_s = _dsa_sample(l, page_table, cache_kv_t, qb_s, bias, kv_new_t, bias_new, pages=pages)
    o_b_s = o_b_s.reshape(n_seq, KVH_B, GROUP_B, n_ds, KVH_B, HEAD_DIM)
    o_b_s = jnp.stack([o_b_s[:, g, :, :, g, :] for g in range(KVH_B)], axis=1)
    o_b_s = o_b_s.reshape(n_seq, H_B, n_ds, HEAD_DIM).swapaxes(1, 2).reshape(n_s, H_B * HEAD_DIM)

    zeros_tail = jnp.zeros((n_all - n_p - n_s, H_A * V_A), BF16)
    o_a = jnp.concatenate([o_a_p.reshape(n_p, H_A * V_A), o_a_s.astype(BF16), zeros_tail], axis=0)
    o_b = jnp.concatenate([o_b_p, o_b_s.astype(BF16), zeros_tail], axis=0)
    assert COL_GATE % (2 * d) == 0
    x_all = _merge(x_all, o_a, o_b, z, wts["w_pa"][l].astype(BF16), wts["w_pb"][l].astype(BF16),
                   wts["w_o"][l].astype(BF16))
    x_all = _moe(x_all, wts["g_ffn_norm"][l], wts["w_router"][l], wts["b_router"][l], wts["w_mlp1"][l],
                 wts["b_mlp1"][l], wts["w_mlp2"][l], wts["b_mlp2"][l])

    new_p = (mla_rows[:n_p].reshape(n_b, t_pad, -1)[:, :t_real],
             kv_rows[:n_p].reshape(n_b, t_pad, 2, KVH_B, HEAD_DIM)[:, :t_real],
             i_k[:n_p].reshape(n_b, t_pad, D_I)[:, :t_real])
    new_s = (mla_rows[sl].reshape(n_seq, n_ds, -1), kv_rows[sl].reshape(n_seq, n_ds, 2, KVH_B, HEAD_DIM),
             i_k[sl].reshape(n_seq, n_ds, D_I))
    return x_all[:n_p].reshape(n_b, t_pad, d), x_all[sl].reshape(n_seq, n_ds, d), new_p, new_s


def kernel(x_prompt, x_sample, cache_mla, cache_dsa_kv, cache_dsa_idx, page_table, meta_tokens, g_attn_norm, w_in,
           g_cq, w_uq, g_qa, g_ckv, w_uk, w_uv, g_ka, g_qb, g_kb, g_ik, w_pa, w_pb, w_o, g_ffn_norm, w_router,
           b_router, w_mlp1, b_mlp1, w_mlp2, b_mlp2):
    n_b, n_s, d = x_prompt.shape
    t_real = n_s + N_META
    t_pad = _round_up(t_real, LANE)
    wts = dict(g_attn_norm=g_attn_norm, w_in=w_in, g_cq=g_cq, w_uq=w_uq, g_qa=g_qa, g_ckv=g_ckv, w_uk=w_uk, w_uv=w_uv,
               g_ka=g_ka, g_qb=g_qb, g_kb=g_kb, g_ik=g_ik, w_pa=w_pa, w_pb=w_pb, w_o=w_o, g_ffn_norm=g_ffn_norm,
               w_router=w_router, b_router=b_router, w_mlp1=w_mlp1, b_mlp1=b_mlp1, w_mlp2=w_mlp2, b_mlp2=b_mlp2)
    meta = jnp.broadcast_to(meta_tokens[None].astype(x_prompt.dtype), (n_b, N_META, d))
    xp = jnp.concatenate([meta, x_prompt, jnp.zeros((n_b, t_pad - t_real, d), x_prompt.dtype)], axis=1)
    xs = x_sample
    outs_p, outs_s = [], []
    for l in range(w_in.shape[0]):
        xp, xs, new_p, new_s = _layer(l, xp, xs, page_table, (cache_mla, cache_dsa_kv, cache_dsa_idx), wts,
                                      t_real=t_real)
        outs_p.append(new_p)
        outs_s.append(new_s)
    stack = lambda outs, i: jnp.stack([o[i] for o in outs], axis=0)
    return (xp[:, N_META:t_real], xs, stack(outs_p, 0), stack(outs_p, 1), stack(outs_p, 2),
            stack(outs_s, 0), stack(outs_s, 1), stack(outs_s, 2))
```

```python
import functools

import jax
import jax.numpy as jnp
import numpy as np
from jax import lax
from jax.experimental import pallas as pl
from jax.experimental.pallas import tpu as pltpu

N_META = 16
HEAD_DIM = 64
H_A = 8
NOPE_A = 64
ROPE_A = 32
DQK_A = NOPE_A + ROPE_A
V_A = 64
Q_LORA = 512
KV_LORA = 256
H_B = 8
KVH_B = 2
GROUP_B = H_B // KVH_B
H_I = 8
D_I = 32
D_I_ROPE = 16
MAX_SEL = 256
N_EXPERTS = 32
TOP_K = 4
SWIGLU_LIMIT = 7.0
SWIGLU_ALPHA = 1.702
ROPE_THETA = 10000.0
NORM_EPS = 1e-6
PAGE_SIZE = 128

LANE = 128
MASK_VALUE = -1e30
INT_MIN = -(2 ** 31)
VMEM_LIMIT = 56 * 1024 * 1024
MAX_PAGES_PER_STEP = 16

COL_CQ = 0
COL_CKV = 512
COL_QB = 768
COL_KB = 1280
COL_VB = 1408
COL_IQ = 1536
COL_MISC = 1792
COL_GATE = 2048
N_PROJ = 4096

F32 = jnp.float32
BF16 = jnp.bfloat16
NT_DIMS = (((1,), (1,)), ((), ()))


def _cdiv(a, b):
    return (a + b - 1) // b


def _round_up(a, b):
    return _cdiv(a, b) * b


def _params(*sem):
    return pltpu.CompilerParams(dimension_semantics=sem, vmem_limit_bytes=VMEM_LIMIT)


def _mm_kernel(x_ref, w_ref, o_ref):
    o_ref[...] = jnp.dot(x_ref[...].astype(BF16), w_ref[...], preferred_element_type=F32).astype(o_ref.dtype)


def _mm_norm_kernel(x_ref, g_ref, w_ref, o_ref, xn_ref):
    @pl.when(pl.program_id(1) == 0)
    def _():
        x = x_ref[...].astype(F32)
        ms = jnp.mean(x * x, axis=-1, keepdims=True)
        xn_ref[...] = (x * lax.rsqrt(ms + NORM_EPS) * g_ref[...]).astype(BF16)

    o_ref[...] = jnp.dot(xn_ref[...], w_ref[...], preferred_element_type=F32).astype(o_ref.dtype)


def _matmul(x, w, *, gain=None, col_block=0, k=None, tm=512, tn=512, out_dtype=F32, name="mm"):
    m = x.shape[0]
    k = x.shape[1] if k is None else k
    n = w.shape[1]
    tn = min(tn, n)
    assert m % tm == 0 and n % tn == 0 and w.shape[0] == k
    grid = (m // tm, n // tn)
    x_spec = pl.BlockSpec((tm, k), lambda i, j: (i, col_block))
    w_spec = pl.BlockSpec((k, tn), lambda i, j: (0, j))
    o_spec = pl.BlockSpec((tm, tn), lambda i, j: (i, j))
    out_shape = jax.ShapeDtypeStruct((m, n), out_dtype)
    if gain is None:
        return pl.pallas_call(_mm_kernel, out_shape=out_shape, grid=grid, in_specs=[x_spec, w_spec],
                              out_specs=o_spec, compiler_params=_params("parallel", "arbitrary"), name=name)(x, w)
    g_spec = pl.BlockSpec((1, k), lambda i, j: (0, 0))
    return pl.pallas_call(_mm_norm_kernel, out_shape=out_shape, grid=grid, in_specs=[x_spec, g_spec, w_spec],
                          out_specs=o_spec, scratch_shapes=[pltpu.VMEM((tm, k), BF16)],
                          compiler_params=_params("parallel", "arbitrary"), name=name)(
                              x, gain.reshape(1, k).astype(F32), w)


def _lane_ids(tm):
    return lax.broadcasted_iota(jnp.int32, (tm, LANE), 1)


def _rotate_pairs(t, cos_t, sin_t, lane, group, split, half):
    first = (lane & (group - 1)) < split
    partner = jnp.where(first, pltpu.roll(t, LANE - half, 1), pltpu.roll(t, half, 1))
    return t * cos_t + partner * sin_t


def _head_tiles(x, fn):
    tiles = [fn(x[:, i * LANE:(i + 1) * LANE]) for i in range(x.shape[1] // LANE)]
    return tiles[0] if len(tiles) == 1 else jnp.concatenate(tiles, axis=1)


def _mla_head_post(t, gain, cos_t, sin_t, lane):
    ms = jnp.sum(t * t, axis=1, keepdims=True) * (1.0 / DQK_A)
    return _rotate_pairs(t * lax.rsqrt(ms + NORM_EPS) * gain, cos_t, sin_t, lane, LANE, NOPE_A + ROPE_A // 2,
                         ROPE_A // 2)


def _q_up_kernel(z_ref, gz_ref, w_ref, gq_ref, cos_ref, sin_ref, o_ref):
    x = z_ref[...]
    xn = (x * lax.rsqrt(jnp.mean(x * x, axis=-1, keepdims=True) + NORM_EPS) * gz_ref[...]).astype(BF16)
    q = jnp.dot(xn, w_ref[...], preferred_element_type=F32)
    lane = _lane_ids(x.shape[0])
    post = lambda t: _mla_head_post(t, gq_ref[...], cos_ref[...], sin_ref[...], lane)
    o_ref[...] = _head_tiles(q, post).astype(o_ref.dtype)


def _kv_up_kernel(z_ref, gz_ref, misc_ref, w_ref, place_ref, gk_ref, cos_ref, sin_ref, k_ref, v_ref):
    x = z_ref[...]
    c = (x * lax.rsqrt(jnp.mean(x * x, axis=-1, keepdims=True) + NORM_EPS) * gz_ref[...]).astype(BF16)
    kv = jnp.dot(c, w_ref[...], preferred_element_type=F32)
    n_k = H_A * LANE
    misc = misc_ref[...]
    hi = misc.astype(BF16)
    lo = (misc - hi.astype(F32)).astype(BF16)
    k = kv[:, :n_k] + (jnp.dot(hi, place_ref[...], preferred_element_type=F32)
                       + jnp.dot(lo, place_ref[...], preferred_element_type=F32))
    lane = _lane_ids(x.shape[0])
    post = lambda t: _mla_head_post(t, gk_ref[...], cos_ref[...], sin_ref[...], lane)
    k_ref[...] = _head_tiles(k, post).astype(k_ref.dtype)
    v_ref[...] = kv[:, n_k:].astype(v_ref.dtype)


def _row_call(kernel_fn, n, tm, ins, outs, name):
    def spec(width, col_block, shape):
        if width is None:
            return pl.BlockSpec(shape, lambda i: tuple(0 for _ in shape))
        return pl.BlockSpec((tm, width), lambda i: (i, col_block))
    in_specs = [spec(w, cb, a.shape) for a, w, cb in ins]
    out_specs = tuple(pl.BlockSpec((tm, w), lambda i: (i, 0)) for w, _ in outs)
    out_shape = tuple(jax.ShapeDtypeStruct((n, w), dt) for w, dt in outs)
    return pl.pallas_call(kernel_fn, out_shape=out_shape, grid=(n // tm,), in_specs=in_specs, out_specs=out_specs,
                          compiler_params=_params("parallel"), name=name)(*[a for a, _, _ in ins])


def _post_z_kernel(z_ref, gckv_ref, gqb_ref, gkb_ref, gik_ref, cb_ref, sb_ref, ci_ref, si_ref, expand_ref,
                   mla_ref, kv_ref, ik_ref, iw_ref, qb_ref, k2_ref, v2_ref, iq_ref, ikb_ref):
    z = z_ref[...]
    tm = z.shape[0]
    lane = _lane_ids(tm)
    cb, sb, ci, si = cb_ref[...], sb_ref[...], ci_ref[...], si_ref[...]
    misc = z[:, COL_MISC:COL_MISC + LANE]

    x = z[:, COL_CKV:COL_CKV + KV_LORA]
    mla_ref[:, 0:KV_LORA] = x * lax.rsqrt(jnp.mean(x * x, axis=-1, keepdims=True) + NORM_EPS) * gckv_ref[...]
    mla_ref[:, KV_LORA:] = misc[:, 0:ROPE_A]

    low = lane < HEAD_DIM

    def head_pair(t, gain):
        t2 = t * t
        ms = jnp.where(low, jnp.sum(jnp.where(low, t2, 0.0), axis=1, keepdims=True),
                       jnp.sum(jnp.where(low, 0.0, t2), axis=1, keepdims=True)) * (1.0 / HEAD_DIM)
        return _rotate_pairs(t * lax.rsqrt(ms + NORM_EPS) * gain, cb, sb, lane, HEAD_DIM, HEAD_DIM // 2, HEAD_DIM // 2)

    qb = _head_tiles(z[:, COL_QB:COL_QB + H_B * HEAD_DIM], lambda t: head_pair(t, gqb_ref[...])).astype(BF16)
    qb_ref[...] = jnp.dot(qb, expand_ref[...], preferred_element_type=F32).astype(qb_ref.dtype)

    kb = head_pair(z[:, COL_KB:COL_KB + LANE], gkb_ref[...])
    vb = z[:, COL_VB:COL_VB + LANE]
    kv_ref[:, 0:LANE] = kb
    kv_ref[:, LANE:] = vb
    k2_ref[...] = kb.astype(k2_ref.dtype)
    v2_ref[...] = vb.astype(v2_ref.dtype)

    rot_i = lambda t: _rotate_pairs(t, ci, si, lane, D_I, D_I - D_I_ROPE // 2, D_I_ROPE // 2)
    iq_ref[...] = _head_tiles(z[:, COL_IQ:COL_IQ + H_I * D_I], rot_i).astype(iq_ref.dtype)

    in_ik = (lane >= ROPE_A) & (lane < ROPE_A + D_I)
    ms = jnp.sum(jnp.where(in_ik, misc * misc, 0.0), axis=1, keepdims=True) * (1.0 / D_I)
    ik = pltpu.roll(rot_i(misc * lax.rsqrt(ms + NORM_EPS) * gik_ref[...]), LANE - ROPE_A, 1)
    ik_ref[...] = ik[:, 0:D_I]
    ikb_ref[...] = ik[:, 0:D_I].astype(ikb_ref.dtype)
    iw_ref[...] = pltpu.roll(misc, LANE - ROPE_A - D_I, 1)[:, 0:H_I] * ((H_I ** -0.5) * (D_I ** -0.5))


def _merge_kernel(x_ref, oa_ref, ob_ref, zg_ref, wpa_ref, wpb_ref, wo_ref, o_ref):
    d = x_ref.shape[1]
    y_a = jnp.dot(oa_ref[...], wpa_ref[...], preferred_element_type=F32)
    y_b = jnp.dot(ob_ref[...], wpb_ref[...], preferred_element_type=F32)
    zg = zg_ref[...]
    merged = jax.nn.sigmoid(zg[:, :d]) * y_a + jax.nn.sigmoid(zg[:, d:]) * y_b
    o_ref[...] = x_ref[...] + jnp.dot(merged.astype(BF16), wo_ref[...], preferred_element_type=F32)


def _merge(x, o_a, o_b, z, w_pa, w_pb, w_o, *, tm=512):
    n, d = x.shape
    ka, kb = o_a.shape[1], o_b.shape[1]
    const = lambda shape: pl.BlockSpec(shape, lambda i: (0, 0))
    return pl.pallas_call(
        _merge_kernel,
        out_shape=jax.ShapeDtypeStruct((n, d), F32),
        grid=(n // tm,),
        in_specs=[pl.BlockSpec((tm, d), lambda i: (i, 0)), pl.BlockSpec((tm, ka), lambda i: (i, 0)),
                  pl.BlockSpec((tm, kb), lambda i: (i, 0)), pl.BlockSpec((tm, 2 * d), lambda i: (i, COL_GATE // (2 * d))),
                  const((ka, d)), const((kb, d)), const((d, d))],
        out_specs=pl.BlockSpec((tm, d), lambda i: (i, 0)),
        compiler_params=_params("parallel"),
        name="merge")(x, o_a, o_b, z, w_pa, w_pb, w_o)


def _softmax_step(s, m_prev, l_prev, acc_prev, v_blk):
    m_new = jnp.maximum(m_prev, jnp.max(s, axis=1, keepdims=True))
    alpha = jnp.exp(m_prev - m_new)
    p = jnp.exp(s - m_new)
    l_new = alpha * l_prev + jnp.sum(p, axis=1, keepdims=True)
    acc_new = alpha * acc_prev + jnp.dot(p.astype(BF16), v_blk, preferred_element_type=F32)
    return m_new, l_new, acc_new


def _sortable_key(x):
    b = lax.bitcast_convert_type(x + 0.0, jnp.int32)
    return jnp.where(b < 0, b ^ jnp.int32(0x7FFFFFFF), b)


KEY_NEG_INF = INT_MIN + 0x7FFFFF


def _has_excess_ties(n_ge, thr, n_sel):
    return jnp.max(jnp.where(n_ge > n_sel, jnp.where(thr > KEY_NEG_INF, 1, 0), 0)) > 0


def _kth_largest_key(count_ge, n_sel, rows):
    def step(s, t):
        bit = lax.shift_left(jnp.int32(1), jnp.int32(31) - s)
        cand = t + bit
        return jnp.where(count_ge(cand) >= n_sel, cand, t)

    return lax.fori_loop(0, 32, step, jnp.full((rows, 1), INT_MIN, jnp.int32))


MLA_HEADS_PER_STEP = 4


def _mla_prompt_kernel(q_ref, k_ref, v_ref, o_ref, *, tq):
    i = pl.program_id(2)
    nh = MLA_HEADS_PER_STEP
    q = q_ref[...]
    causal = lax.broadcasted_iota(jnp.int32, (tq, tq), 1) <= lax.broadcasted_iota(jnp.int32, (tq, tq), 0)

    def chunk(c, carry, diagonal):
        m, l, acc = carry
        start = pl.multiple_of(c * tq, tq)
        k_blk = k_ref[pl.ds(start, tq), :]
        v_blk = v_ref[pl.ds(start, tq), :]
        s = jnp.concatenate([lax.dot_general(q[:, h * LANE:(h + 1) * LANE], k_blk[:, h * LANE:(h + 1) * LANE],
                                             NT_DIMS, preferred_element_type=F32) for h in range(nh)], axis=0)
        if diagonal:
            s = jnp.where(causal[None], s.reshape(nh, tq, tq), MASK_VALUE).reshape(nh * tq, tq)
        m_new = jnp.maximum(m, jnp.max(s, axis=1, keepdims=True))
        alpha = jnp.exp(m - m_new)
        p = jnp.exp(s - m_new)
        l_new = alpha * l + jnp.sum(p, axis=1, keepdims=True)
        p = p.astype(BF16)
        pv = jnp.concatenate([jnp.dot(p[h * tq:(h + 1) * tq], v_blk[:, (h // 2) * LANE:(h // 2 + 1) * LANE],
                                      preferred_element_type=F32) for h in range(nh)], axis=0)
        return m_new, l_new, alpha * acc + pv

    init = (jnp.full((nh * tq, 1), MASK_VALUE, F32), jnp.zeros((nh * tq, 1), F32), jnp.zeros((nh * tq, LANE), F32))
    carry = lax.fori_loop(0, i, functools.partial(chunk, diagonal=False), init)
    _, l, acc = chunk(i, carry, True)
    o = acc / l
    lane = lax.broadcasted_iota(jnp.int32, (tq, LANE), 1)
    pairs = [jnp.where(lane < V_A, o[(2 * g) * tq:(2 * g + 1) * tq], o[(2 * g + 1) * tq:(2 * g + 2) * tq])
             for g in range(nh // 2)]
    o_ref[...] = jnp.concatenate(pairs, axis=1).astype(o_ref.dtype)


def _mla_prompt(q, k, v, *, tq):
    b, t, _ = q.shape
    nh = MLA_HEADS_PER_STEP
    grid = (b, H_A // nh, t // tq)
    return pl.pallas_call(
        functools.partial(_mla_prompt_kernel, tq=tq),
        out_shape=jax.ShapeDtypeStruct((b, t, H_A * V_A), BF16),
        grid=grid,
        in_specs=[pl.BlockSpec((None, tq, nh * LANE), lambda bi, g, i: (bi, i, g)),
                  pl.BlockSpec((None, t, nh * LANE), lambda bi, g, i: (bi, 0, g)),
                  pl.BlockSpec((None, t, nh * V_A), lambda bi, g, i: (bi, 0, g))],
        out_specs=pl.BlockSpec((None, tq, nh * V_A), lambda bi, g, i: (bi, i, g)),
        compiler_params=_params("parallel", "parallel", "arbitrary"),
        name="mla_prompt")(q, k, v)


def _select_bias(key, thr, need, carry, valid, tri_ref):
    w = min(tri_ref.shape[0], key.shape[1])
    tri = tri_ref[0:w, 0:w]
    parts = []
    for t in range(key.shape[1] // w):
        k_t = key[:, t * w:(t + 1) * w]
        eq = k_t == thr
        rank = jnp.dot(jnp.where(eq, 1.0, 0.0).astype(BF16), tri, preferred_element_type=F32) + carry
        carry = jnp.max(rank, axis=1, keepdims=True)
        tie = jnp.where(rank <= need, 0.0, MASK_VALUE)
        parts.append(jnp.where(k_t > thr, 0.0, jnp.where(eq, tie, MASK_VALUE)))
    bias = parts[0] if len(parts) == 1 else jnp.concatenate(parts, axis=1)
    if valid is not None:
        bias = jnp.where(valid, bias, MASK_VALUE)
    return bias, carry


def _dsa_prompt_kernel(iq_ref, w_ref, q_ref, ik_ref, k_ref, v_ref, tri_ref, o_ref, key_ref, bias_ref, wb_ref,
                       *, tq, tk, n_sel):
    i = pl.program_id(1)
    n_c = ((i + 1) * tq + tk - 1) // tk
    q_pos = i * tq + lax.broadcasted_iota(jnp.int32, (tq, tk), 0)
    lane_pos = lax.broadcasted_iota(jnp.int32, (tq, tk), 1)

    w = w_ref[...]
    for h in range(H_I):
        wb_ref[h] = jnp.broadcast_to(w[:, h:h + 1], (tq, LANE))
    iq = iq_ref[...].reshape(H_I * tq, D_I)

    def score_chunk(c, _):
        start = pl.multiple_of(c * tk, tk)
        logit = lax.dot_general(iq, ik_ref[pl.ds(start, tk), :], NT_DIMS, preferred_element_type=F32)
        cols = []
        for t in range(tk // LANE):
            acc = jnp.zeros((tq, LANE), F32)
            for h in range(H_I):
                acc = acc + jnp.maximum(logit[h * tq:(h + 1) * tq, t * LANE:(t + 1) * LANE], 0.0) * wb_ref[h]
            cols.append(acc)
        score = jnp.concatenate(cols, axis=1)
        score = jnp.where(start + lane_pos <= q_pos, score, -jnp.inf)
        key_ref[c] = _sortable_key(score)
        return 0

    lax.fori_loop(0, n_c, score_chunk, 0)

    def count_ge(cand):
        cand_b = jnp.broadcast_to(cand, (tq, LANE))

        def body(c, cnt):
            key = key_ref[c]
            for t in range(tk // LANE):
                cnt = cnt + jnp.where(key[:, t * LANE:(t + 1) * LANE] >= cand_b, 1, 0)
            return cnt

        cnt = lax.fori_loop(0, n_c, body, jnp.zeros((tq, LANE), jnp.int32))
        return jnp.sum(cnt, axis=1, keepdims=True)

    thr = _kth_largest_key(count_ge, n_sel, tq)

    def bias_by_threshold():
        thr_eff = jnp.maximum(thr, KEY_NEG_INF + 1)

        def body(c, _):
            bias_ref[c] = jnp.where(key_ref[c] >= thr_eff, 0.0, MASK_VALUE)
            return 0

        lax.fori_loop(0, n_c, body, 0)

    def bias_with_ties():
        need = (n_sel - count_ge(thr + 1)).astype(F32)

        def body(c, carry):
            start = pl.multiple_of(c * tk, tk)
            bias, carry = _select_bias(key_ref[c], thr, need, carry, start + lane_pos <= q_pos, tri_ref)
            bias_ref[c] = bias
            return carry

        lax.fori_loop(0, n_c, body, jnp.zeros((tq, 1), F32))

    lax.cond(_has_excess_ties(count_ge(thr), thr, n_sel), bias_with_ties, bias_by_threshold)

    rows = H_B * tq
    qa = q_ref[...].reshape(rows, LANE)

    def attend(c, carry):
        start = pl.multiple_of(c * tk, tk)
        s = lax.dot_general(qa, k_ref[pl.ds(start, tk), :], NT_DIMS, preferred_element_type=F32)
        s = (s.reshape(H_B, tq, tk) + bias_ref[c][None]).reshape(rows, tk)
        return _softmax_step(s, *carry, v_ref[pl.ds(start, tk), :])

    m, l, acc = lax.fori_loop(0, n_c, attend, (jnp.full((rows, 1), MASK_VALUE, F32),
                                               jnp.zeros((rows, 1), F32), jnp.zeros((rows, LANE), F32)))
    o_ref[...] = (acc / l).reshape(H_B, tq, LANE).astype(o_ref.dtype)


def _dsa_prompt(iq, iw, q, ik, k, v, *, tq, tk, n_sel):
    b, nq = iq.shape[:2]
    t = ik.shape[1]
    n_chunks = t // tk
    tri_w = 2 * LANE if tk % (2 * LANE) == 0 else LANE
    tri = jnp.asarray(np.arange(tri_w)[:, None] <= np.arange(tri_w)[None, :], BF16)
    return pl.pallas_call(
        functools.partial(_dsa_prompt_kernel, tq=tq, tk=tk, n_sel=n_sel),
        out_shape=jax.ShapeDtypeStruct((b, nq, H_B, tq, LANE), BF16),
        grid=(b, nq),
        in_specs=[pl.BlockSpec((None, None, H_I, tq, D_I), lambda bi, i: (bi, i, 0, 0, 0)),
                  pl.BlockSpec((None, tq, H_I), lambda bi, i: (bi, i, 0)),
                  pl.BlockSpec((None, None, H_B, tq, LANE), lambda bi, i: (bi, i, 0, 0, 0)),
                  pl.BlockSpec((None, t, D_I), lambda bi, i: (bi, 0, 0)),
                  pl.BlockSpec((None, t, LANE), lambda bi, i: (bi, 0, 0)),
                  pl.BlockSpec((None, t, LANE), lambda bi, i: (bi, 0, 0)),
                  pl.BlockSpec((tri_w, tri_w), lambda bi, i: (0, 0))],
        out_specs=pl.BlockSpec((None, None, H_B, tq, LANE), lambda bi, i: (bi, i, 0, 0, 0)),
        scratch_shapes=[pltpu.VMEM((n_chunks, tq, tk), jnp.int32),
                        pltpu.VMEM((n_chunks, tq, tk), F32),
                        pltpu.VMEM((H_I, tq, LANE), F32)],
        compiler_params=_params("parallel", "arbitrary"),
        name="dsa_prompt")(iq, iw, q, ik, k, v, tri)


def _seq_spec(shape):
    return pl.BlockSpec((None,) + shape, lambda b, j, pt: (b,) + tuple(0 for _ in shape))


def _const_spec(shape):
    return pl.BlockSpec(shape, lambda b, j, pt: tuple(0 for _ in shape))


HBM_SPEC = pl.BlockSpec(memory_space=pl.ANY)


def _page_copy(cache_hbm, buf, sem, layer, page, slot, p):
    return pltpu.make_async_copy(cache_hbm.at[layer, page], buf.at[slot, :, pl.ds(p * PAGE_SIZE, PAGE_SIZE)],
                                 sem.at[slot])


def _stream_pages(pt_ref, cache_hbm, buf, sem, *, layer, pages):
    b = pl.program_id(0)
    j = pl.program_id(1)
    n_seq = pl.num_programs(0)
    n_steps = pl.num_programs(1)
    t = b * n_steps + j
    slot = lax.rem(t, 2)

    def start(seq, step, dst_slot):
        for p in range(pages):
            _page_copy(cache_hbm, buf, sem, layer, pt_ref[seq, step * pages + p], dst_slot, p).start()

    @pl.when(t == 0)
    def _():
        start(b, j, slot)

    @pl.when(t + 1 < n_seq * n_steps)
    def _():
        wrap = j + 1 == n_steps
        start(jnp.where(wrap, b + 1, b), jnp.where(wrap, 0, j + 1), 1 - slot)

    for p in range(pages):
        _page_copy(cache_hbm, buf, sem, layer, 0, slot, p).wait()
    return slot


def _page_scratch(pages, width):
    return [pltpu.VMEM((2, width, pages * PAGE_SIZE), F32), pltpu.SemaphoreType.DMA((2,))]


def _softmax_step_t(s, m_ref, l_ref, acc_ref, v_t):
    m_prev = m_ref[...]
    m_new = jnp.maximum(m_prev, jnp.max(s, axis=1, keepdims=True))
    alpha = jnp.exp(m_prev - m_new)
    p = jnp.exp(s - m_new)
    l_ref[...] = alpha * l_ref[...] + jnp.sum(p, axis=1, keepdims=True)
    acc_ref[...] = alpha * acc_ref[...] + lax.dot_general(p.astype(BF16), v_t, NT_DIMS, preferred_element_type=F32)
    m_ref[...] = m_new


def _init_softmax(m_ref, l_ref, acc_ref):
    m_ref[...] = jnp.full(m_ref.shape, MASK_VALUE, F32)
    l_ref[...] = jnp.zeros(l_ref.shape, F32)
    acc_ref[...] = jnp.zeros(acc_ref.shape, F32)


def _mla_sample_kernel(pt_ref, cache_hbm, wt_ref, qbd_ref, qr_ref, qs_ref, cg_ref, sg_ref, new_ref, cg_new_ref,
                       sg_new_ref, o_ref, buf, sem, lhs_ref, m_ref, l_ref, acc_ref, *, layer, pages, n_new):
    j = pl.program_id(1)
    rows = o_ref.shape[0]
    n_k = H_A * NOPE_A
    slot = _stream_pages(pt_ref, cache_hbm, buf, sem, layer=layer, pages=pages)

    @pl.when(j == 0)
    def _():
        wt = wt_ref[...]
        lhs_ref[0:n_k, :] = wt
        lhs_ref[n_k:, :] = jnp.dot(qbd_ref[...], wt, preferred_element_type=F32).astype(BF16)
        _init_softmax(m_ref, l_ref, acc_ref)

    def attend(c_t, pe_t, cg, sg, mask):
        n = c_t.shape[1]
        big = jnp.dot(lhs_ref[...], c_t, preferred_element_type=F32)
        ssq = jnp.zeros((H_A, n), F32)
        row_id = lax.broadcasted_iota(jnp.int32, (H_A, n), 0)
        for h in range(H_A):
            kh = big[h * NOPE_A:(h + 1) * NOPE_A]
            ssq = jnp.where(row_id == h, jnp.sum(kh * kh, axis=0, keepdims=True), ssq)
        ssq = ssq + jnp.sum(pe_t * pe_t, axis=0, keepdims=True)
        inv_rms = lax.rsqrt(ssq * (1.0 / DQK_A) + NORM_EPS)
        s_rope = (jnp.dot(qr_ref[...], (pe_t * cg).astype(BF16), preferred_element_type=F32)
                  + jnp.dot(qs_ref[...], (pe_t * sg).astype(BF16), preferred_element_type=F32))
        s = ((big[n_k:] + s_rope).reshape(rows // H_A, H_A, n) * inv_rms[None]).reshape(rows, n)
        if mask is not None:
            s = jnp.where(mask, s, MASK_VALUE)
        _softmax_step_t(s, m_ref, l_ref, acc_ref, c_t)

    attend(buf[slot, 0:KV_LORA, :].astype(BF16), buf[slot, KV_LORA:, :], cg_ref[...], sg_ref[...], None)

    @pl.when(j == pl.num_programs(1) - 1)
    def _():
        new = new_ref[...]
        n = new.shape[1]
        key_i = lax.broadcasted_iota(jnp.int32, (rows, n), 1)
        q_i = lax.broadcasted_iota(jnp.int32, (rows, n), 0) // H_A
        attend(new[:KV_LORA].astype(BF16), new[KV_LORA:], cg_new_ref[...], sg_new_ref[...],
               (key_i <= q_i) & (key_i < n_new))
        o_ref[...] = acc_ref[...] / l_ref[...]


def _mla_sample(layer, page_table, cache_t, wt, qbd, qr, qs, cg_t, sg_t, new_t, cg_new_t, sg_new_t, *, pages, n_new):
    n_seq, n_pages = page_table.shape
    rows = qbd.shape[1]
    n_main = pages * PAGE_SIZE
    n_tail = new_t.shape[2]
    width = KV_LORA + ROPE_A
    in_specs = [HBM_SPEC,
                _const_spec((H_A * NOPE_A, KV_LORA)),
                _seq_spec((rows, H_A * NOPE_A)), _seq_spec((rows, ROPE_A)), _seq_spec((rows, ROPE_A)),
                pl.BlockSpec((ROPE_A, n_main), lambda b, j, pt: (0, j)),
                pl.BlockSpec((ROPE_A, n_main), lambda b, j, pt: (0, j)),
                _seq_spec((width, n_tail)), _const_spec((ROPE_A, n_tail)), _const_spec((ROPE_A, n_tail))]
    return pl.pallas_call(
        functools.partial(_mla_sample_kernel, layer=layer, pages=pages, n_new=n_new),
        out_shape=jax.ShapeDtypeStruct((n_seq, rows, KV_LORA), F32),
        grid_spec=pltpu.PrefetchScalarGridSpec(
            num_scalar_prefetch=1, grid=(n_seq, n_pages // pages), in_specs=in_specs,
            out_specs=pl.BlockSpec((None, rows, KV_LORA), lambda b, j, pt: (b, 0, 0)),
            scratch_shapes=_page_scratch(pages, width) + [
                pltpu.VMEM((H_A * NOPE_A + rows, KV_LORA), BF16),
                pltpu.VMEM((rows, 1), F32), pltpu.VMEM((rows, 1), F32), pltpu.VMEM((rows, KV_LORA), F32)]),
        compiler_params=_params("arbitrary", "arbitrary"),
        name="mla_sample")(page_table, cache_t, wt, qbd, qr, qs, cg_t, sg_t, new_t, cg_new_t, sg_new_t)


def _dsa_select_kernel(pt_ref, cache_hbm, iq_ref, w_ref, ik_new_ref, tri_ref, bias_ref, bias_new_ref, buf, sem,
                       key_ref, key_new_ref, *, layer, pages, n_new, n_sel):
    j = pl.program_id(1)
    n_steps = key_ref.shape[0]
    n_main = key_ref.shape[2]
    n_tail = ik_new_ref.shape[1]
    nq = 8
    slot = _stream_pages(pt_ref, cache_hbm, buf, sem, layer=layer, pages=pages)

    def scores(ik_t):
        logit = jnp.dot(iq_ref[...], ik_t, preferred_element_type=F32)
        weighted = jnp.maximum(logit, 0.0) * w_ref[...]
        return jnp.sum(weighted.reshape(H_I, nq, ik_t.shape[1]), axis=0)

    key_ref[j] = _sortable_key(scores(buf[slot].astype(BF16)))

    @pl.when(j == n_steps - 1)
    def _():
        key_i = lax.broadcasted_iota(jnp.int32, (nq, n_tail), 1)
        q_i = lax.broadcasted_iota(jnp.int32, (nq, n_tail), 0) % n_new
        valid_new = (key_i <= q_i) & (key_i < n_new)
        key_new_ref[...] = _sortable_key(jnp.where(valid_new, scores(ik_new_ref[...]), -jnp.inf))

        def count_ge(cand):
            cand_b = jnp.broadcast_to(cand, (nq, LANE))
            cnts = [jnp.zeros((nq, LANE), jnp.int32) for _ in range(4)]
            tiles = [key_ref[c, :, t * LANE:(t + 1) * LANE] for c in range(n_steps) for t in range(n_main // LANE)]
            tiles += [key_new_ref[:, t * LANE:(t + 1) * LANE] for t in range(n_tail // LANE)]
            for idx, tile in enumerate(tiles):
                cnts[idx % 4] = cnts[idx % 4] + jnp.where(tile >= cand_b, 1, 0)
            return jnp.sum((cnts[0] + cnts[1]) + (cnts[2] + cnts[3]), axis=1, keepdims=True)

        thr = _kth_largest_key(count_ge, n_sel, nq)

        def bias_by_threshold():
            thr_eff = jnp.maximum(thr, KEY_NEG_INF + 1)

            def body(c, _):
                bias_ref[c] = jnp.where(key_ref[c] >= thr_eff, 0.0, MASK_VALUE)
                return 0

            lax.fori_loop(0, n_steps, body, 0)
            bias_new_ref[...] = jnp.where(key_new_ref[...] >= thr_eff, 0.0, MASK_VALUE)

        def bias_with_ties():
            need = (n_sel - count_ge(thr + 1)).astype(F32)

            def body(c, carry):
                bias, carry = _select_bias(key_ref[c], thr, need, carry, None, tri_ref)
                bias_ref[c] = bias
                return carry

            carry = lax.fori_loop(0, n_steps, body, jnp.zeros((nq, 1), F32))
            bias_new, _ = _select_bias(key_new_ref[...], thr, need, carry, valid_new, tri_ref)
            bias_new_ref[...] = bias_new

        lax.cond(_has_excess_ties(count_ge(thr), thr, n_sel), bias_with_ties, bias_by_threshold)


def _dsa_select(layer, page_table, cache_idx_t, iq, iw, ik_new_t, *, pages, n_new, n_sel):
    n_seq, n_pages = page_table.shape
    n_steps = n_pages // pages
    n_main = pages * PAGE_SIZE
    n_tail = ik_new_t.shape[2]
    rows = iq.shape[1]
    tri = jnp.asarray(np.arange(2 * LANE)[:, None] <= np.arange(2 * LANE)[None, :], BF16)
    in_specs = [HBM_SPEC, _seq_spec((rows, D_I)), _seq_spec((rows, 1)), _seq_spec((D_I, n_tail)),
                _const_spec((2 * LANE, 2 * LANE))]
    return pl.pallas_call(
        functools.partial(_dsa_select_kernel, layer=layer, pages=pages, n_new=n_new, n_sel=n_sel),
        out_shape=(jax.ShapeDtypeStruct((n_seq, n_steps, 8, n_main), F32),
                   jax.ShapeDtypeStruct((n_seq, 8, n_tail), F32)),
        grid_spec=pltpu.PrefetchScalarGridSpec(
            num_scalar_prefetch=1, grid=(n_seq, n_steps), in_specs=in_specs,
            out_specs=(pl.BlockSpec((None, n_steps, 8, n_main), lambda b, j, pt: (b, 0, 0, 0)),
                       pl.BlockSpec((None, 8, n_tail), lambda b, j, pt: (b, 0, 0))),
            scratch_shapes=_page_scratch(pages, D_I) + [
                pltpu.VMEM((n_steps, 8, n_main), jnp.int32), pltpu.VMEM((8, n_tail), jnp.int32)]),
        compiler_params=_params("arbitrary", "arbitrary"),
        name="dsa_select")(page_table, cache_idx_t, iq, iw, ik_new_t, tri)


def _dsa_sample_kernel(pt_ref, cache_hbm, q_ref, bias_ref, kv_new_ref, bias_new_ref, o_ref, buf, sem,
                       m_ref, l_ref, acc_ref, *, layer, pages):
    j = pl.program_id(1)
    rows = q_ref.shape[0]
    kw = KVH_B * HEAD_DIM
    slot = _stream_pages(pt_ref, cache_hbm, buf, sem, layer=layer, pages=pages)

    @pl.when(j == 0)
    def _():
        _init_softmax(m_ref, l_ref, acc_ref)

    def attend(k_t, v_t, bias):
        n = k_t.shape[1]
        s = jnp.dot(q_ref[...], k_t, preferred_element_type=F32)
        s = (s.reshape(rows // 8, 8, n) + bias[None]).reshape(rows, n)
        _softmax_step_t(s, m_ref, l_ref, acc_ref, v_t)

    attend(buf[slot, 0:kw, :].astype(BF16), buf[slot, kw:, :].astype(BF16), bias_ref[...])

    @pl.when(j == pl.num_programs(1) - 1)
    def _():
        kv_new = kv_new_ref[...]
        attend(kv_new[:kw], kv_new[kw:], bias_new_ref[...])
        o_ref[...] = acc_ref[...] / l_ref[...]


def _dsa_sample(layer, page_table, cache_kv_t, q, bias, kv_new_t, bias_new, *, pages):
    n_seq, n_pages = page_table.shape
    n_steps = n_pages // pages
    n_main = pages * PAGE_SIZE
    rows = q.shape[1]
    n_tail = kv_new_t.shape[2]
    kw = KVH_B * HEAD_DIM
    in_specs = [HBM_SPEC, _seq_spec((rows, kw)),
                pl.BlockSpec((None, None, 8, n_main), lambda b, j, pt: (b, j, 0, 0)),
                _seq_spec((2 * kw, n_tail)), _seq_spec((8, n_tail))]
    return pl.pallas_call(
        functools.partial(_dsa_sample_kernel, layer=layer, pages=pages),
        out_shape=jax.ShapeDtypeStruct((n_seq, rows, kw), F32),
        grid_spec=pltpu.PrefetchScalarGridSpec(
            num_scalar_prefetch=1, grid=(n_seq, n_steps), in_specs=in_specs,
            out_specs=pl.BlockSpec((None, rows, kw), lambda b, j, pt: (b, 0, 0)),
            scratch_shapes=_page_scratch(pages, 2 * kw) + [
                pltpu.VMEM((rows, 1), F32), pltpu.VMEM((rows, 1), F32), pltpu.VMEM((rows, kw), F32)]),
        compiler_params=_params("arbitrary", "arbitrary"),
        name="dsa_sample")(page_table, cache_kv_t, q, bias, kv_new_t, bias_new)


def _router_kernel(x_ref, g_ref, w_ref, b_ref, tri_ref, idx_ref, gate_ref, rank_ref, count_ref, carry_ref):
    @pl.when(pl.program_id(0) == 0)
    def _():
        carry_ref[...] = jnp.zeros(carry_ref.shape, F32)

    x = x_ref[...]
    ms = jnp.mean(x * x, axis=-1, keepdims=True)
    h = x * lax.rsqrt(ms + NORM_EPS) * g_ref[...]
    logit = jnp.dot(h, w_ref[...], preferred_element_type=F32, precision=lax.Precision.HIGHEST) + b_ref[...]
    tm, n_e = logit.shape
    lane = lax.broadcasted_iota(jnp.int32, (tm, n_e), 1)
    picks, vals = [], []
    for _ in range(TOP_K):
        top = jnp.max(logit, axis=1, keepdims=True)
        first = jnp.min(jnp.where(logit == top, lane, n_e), axis=1, keepdims=True)
        picks.append(first)
        vals.append(top)
        logit = jnp.where(lane == first, -jnp.inf, logit)
    exps = [jnp.exp(v - vals[0]) for v in vals]
    denom = exps[0]
    for e in exps[1:]:
        denom = denom + e
    chosen = jnp.zeros((tm, n_e), F32)
    for first in picks:
        chosen = chosen + jnp.where(lane == first, 1.0, 0.0)
    before = jnp.dot(tri_ref[...], chosen.astype(BF16), preferred_element_type=F32) + carry_ref[...]
    for k in range(TOP_K):
        idx_ref[:, k:k + 1] = picks[k]
        gate_ref[:, k:k + 1] = exps[k] / denom
        rank_ref[:, k:k + 1] = jnp.sum(jnp.where(lane == picks[k], before, 0.0), axis=1, keepdims=True).astype(jnp.int32)
    carry_ref[...] = carry_ref[...] + jnp.sum(chosen, axis=0, keepdims=True)
    count_ref[...] = carry_ref[...].astype(jnp.int32)


def _router(x, gain, w_router, b_router, *, tm=512):
    n, d = x.shape
    e = w_router.shape[1]
    tri = jnp.asarray(np.arange(tm)[:, None] > np.arange(tm)[None, :], BF16)
    pick = lambda dt: jax.ShapeDtypeStruct((n, TOP_K), dt)
    pick_spec = pl.BlockSpec((tm, TOP_K), lambda i: (i, 0))
    return pl.pallas_call(
        _router_kernel,
        out_shape=(pick(jnp.int32), pick(F32), pick(jnp.int32), jax.ShapeDtypeStruct((1, e), jnp.int32)),
        grid=(n // tm,),
        in_specs=[pl.BlockSpec((tm, d), lambda i: (i, 0)), pl.BlockSpec((1, d), lambda i: (0, 0)),
                  pl.BlockSpec((d, e), lambda i: (0, 0)), pl.BlockSpec((1, e), lambda i: (0, 0)),
                  pl.BlockSpec((tm, tm), lambda i: (0, 0))],
        out_specs=(pick_spec, pick_spec, pick_spec, pl.BlockSpec((1, e), lambda i: (0, 0))),
        scratch_shapes=[pltpu.VMEM((1, e), F32)],
        compiler_params=_params("arbitrary"),
        name="router")(x, gain.reshape(1, d), w_router, b_router.reshape(1, e), tri)


def _expert_kernel(be_ref, bv_ref, x_ref, g_ref, w1_ref, b1_ref, w2_ref, b2_ref, o_ref):
    i = pl.program_id(0)

    @pl.when(bv_ref[i] != 0)
    def _():
        d_ff = w2_ref.shape[0]
        x = x_ref[...]
        ms = jnp.mean(x * x, axis=-1, keepdims=True)
        h = (x * lax.rsqrt(ms + NORM_EPS) * g_ref[...]).astype(BF16)
        u = jnp.dot(h, w1_ref[...].astype(BF16), preferred_element_type=F32) + b1_ref[...]
        g = jnp.minimum(u[:, :d_ff], SWIGLU_LIMIT)
        up = jnp.clip(u[:, d_ff:], -SWIGLU_LIMIT, SWIGLU_LIMIT)
        a = g * jax.nn.sigmoid(SWIGLU_ALPHA * g) * (up + 1.0)
        o_ref[...] = jnp.dot(a.astype(BF16), w2_ref[...].astype(BF16), preferred_element_type=F32) + b2_ref[...]

    @pl.when(bv_ref[i] == 0)
    def _():
        o_ref[...] = jnp.zeros(o_ref.shape, o_ref.dtype)


def _experts(block_e, block_valid, xb, gain, w1, b1, w2, b2, *, tm):
    n_rows, d = xb.shape
    e, _, two_ff = w1.shape
    d_ff = w2.shape[1]
    return pl.pallas_call(
        _expert_kernel,
        out_shape=jax.ShapeDtypeStruct((n_rows, d), F32),
        grid_spec=pltpu.PrefetchScalarGridSpec(
            num_scalar_prefetch=2, grid=(n_rows // tm,),
            in_specs=[pl.BlockSpec((tm, d), lambda i, be, bv: (i, 0)),
                      pl.BlockSpec((1, d), lambda i, be, bv: (0, 0)),
                      pl.BlockSpec((None, d, two_ff), lambda i, be, bv: (be[i], 0, 0)),
                      pl.BlockSpec((None, 1, two_ff), lambda i, be, bv: (be[i], 0, 0)),
                      pl.BlockSpec((None, d_ff, d), lambda i, be, bv: (be[i], 0, 0)),
                      pl.BlockSpec((None, 1, d), lambda i, be, bv: (be[i], 0, 0))],
            out_specs=pl.BlockSpec((tm, d), lambda i, be, bv: (i, 0))),
        compiler_params=_params("arbitrary"),
        name="experts")(block_e, block_valid, xb, gain.reshape(1, d), w1, b1.reshape(e, 1, two_ff), w2,
                        b2.reshape(e, 1, d))


def _moe(x, gain, w_router, b_router, w1, b1, w2, b2, *, tm_e=256):
    n, d = x.shape
    top_idx, gate, rank, counts = _router(x, gain, w_router, b_router)
    flat_e = top_idx.reshape(-1)
    nk = flat_e.shape[0]
    counts = counts[0]
    start = jnp.cumsum(counts) - counts
    padded = _cdiv(counts, tm_e) * tm_e
    pad_end = jnp.cumsum(padded)
    pad_start = pad_end - padded
    dest = pad_start[flat_e] + rank.reshape(-1)
    order = jnp.argsort(dest).astype(jnp.int32)
    n_blocks = _cdiv(nk, tm_e) + N_EXPERTS
    block_start = jnp.arange(n_blocks, dtype=jnp.int32) * tm_e
    block_e = jnp.minimum(jnp.sum((block_start[:, None] >= pad_end[None, :]).astype(jnp.int32), axis=1), N_EXPERTS - 1)
    block_valid = (block_start < pad_end[-1]).astype(jnp.int32)
    row = jnp.arange(n_blocks * tm_e, dtype=jnp.int32)
    row_e = jnp.repeat(block_e, tm_e)
    offset = row - pad_start[row_e]
    src = jnp.clip(start[row_e] + offset, 0, nk - 1)
    row_tok = jnp.where(offset < counts[row_e], order[src] // TOP_K, 0)
    yb = _experts(block_e, block_valid, x[row_tok], gain, w1, b1, w2, b2, tm=tm_e)
    dest_k = dest.reshape(n, TOP_K)
    y = x
    for k in range(TOP_K):
        y = y + yb[dest_k[:, k]] * gate[:, k:k + 1]
    return y


def _rms(x, g, n=None):
    n = x.shape[-1] if n is None else n
    y = x * lax.rsqrt(jnp.sum(x * x, axis=-1, keepdims=True) / n + NORM_EPS)
    return y * g


def _rope_tables(pos, n_rot):
    half = n_rot // 2
    inv_freq = ROPE_THETA ** (-jnp.arange(half, dtype=F32) / half)
    ang = pos.astype(F32)[:, None] * inv_freq[None, :]
    return jnp.cos(ang), jnp.sin(ang)


def _rope(x, cos, sin, n_rot):
    d = x.shape[-1]
    half = n_rot // 2
    cos = cos[:, None, :]
    sin = sin[:, None, :]
    x1 = x[..., d - n_rot:d - half]
    x2 = x[..., d - half:]
    return jnp.concatenate([x[..., :d - n_rot], x1 * cos - x2 * sin, x2 * cos + x1 * sin], axis=-1)


def _regroup_w_in(w_in):
    cuts = np.cumsum([0, Q_LORA, KV_LORA, ROPE_A, H_B * HEAD_DIM, KVH_B * HEAD_DIM, KVH_B * HEAD_DIM,
                      H_I * D_I, D_I, H_I, 2 * w_in.shape[0]])
    w_cq, w_ckv, w_kpe, w_qb, w_kb, w_vb, w_iq, w_ik, w_iw, w_g = [w_in[:, a:b] for a, b in zip(cuts[:-1], cuts[1:])]
    misc_pad = jnp.zeros((w_in.shape[0], COL_GATE - COL_MISC - ROPE_A - D_I - H_I), w_in.dtype)
    return jnp.concatenate([w_cq, w_ckv, w_qb, w_kb, w_vb, w_iq, w_kpe, w_ik, w_iw, misc_pad, w_g], axis=1).astype(BF16)


def _layer(l, xp, xs, page_table, caches, wts, *, t_real):
    (cache_mla, cache_dsa_kv, cache_dsa_idx) = caches
    n_b, t_pad, d = xp.shape
    n_seq, n_ds, _ = xs.shape
    past = page_table.shape[1] * PAGE_SIZE
    n_p = n_b * t_pad
    n_s = n_seq * n_ds
    n_sel_p = min(MAX_SEL, (t_real - N_META) // 4)
    n_sel_s = min(MAX_SEL, (past + n_ds) // 4)

    x_all = jnp.concatenate([xp.reshape(n_p, d), xs.reshape(n_s, d)], axis=0)
    n_all = _round_up(n_p + n_s, 1024)
    x_all = jnp.pad(x_all, ((0, n_all - n_p - n_s), (0, 0)))
    pos = jnp.concatenate([jnp.tile(jnp.arange(t_pad), n_b), jnp.tile(past + jnp.arange(n_ds), n_seq),
                           jnp.zeros((n_all - n_p - n_s,), jnp.int32)])
    z = _matmul(x_all, _regroup_w_in(wts["w_in"][l]), gain=wts["g_attn_norm"][l], tm=1024, tn=1024, name="proj_in")

    cos_a, sin_a = _rope_tables(pos, ROPE_A)
    cos_b, sin_b = _rope_tables(pos, HEAD_DIM)
    cos_i, sin_i = _rope_tables(pos, D_I_ROPE)
    ones = lambda w: jnp.ones((n_all, w), F32)
    zeros = lambda w: jnp.zeros((n_all, w), F32)
    cos_a_t = jnp.concatenate([ones(NOPE_A), cos_a, cos_a, zeros(LANE - DQK_A)], axis=1)
    sin_a_t = jnp.concatenate([zeros(NOPE_A), -sin_a, sin_a, zeros(LANE - DQK_A)], axis=1)
    cos_b_t = jnp.tile(jnp.concatenate([cos_b, cos_b], axis=1), (1, LANE // HEAD_DIM))
    sin_b_t = jnp.tile(jnp.concatenate([-sin_b, sin_b], axis=1), (1, LANE // HEAD_DIM))
    cos_i_t = jnp.tile(jnp.concatenate([ones(D_I - D_I_ROPE), cos_i, cos_i], axis=1), (1, LANE // D_I))
    sin_i_t = jnp.tile(jnp.concatenate([zeros(D_I - D_I_ROPE), -sin_i, sin_i], axis=1), (1, LANE // D_I))
    pad_gain = lambda g: jnp.pad(g, (0, LANE - DQK_A)).reshape(1, LANE)
    tm = 512

    w_uq = jnp.pad(wts["w_uq"][l], ((0, 0), (0, 0), (0, LANE - DQK_A))).reshape(Q_LORA, H_A * LANE).astype(BF16)
    (q_a,) = _row_call(_q_up_kernel, n_all, tm,
                       [(z, Q_LORA, COL_CQ // Q_LORA), (wts["g_cq"][l].reshape(1, Q_LORA), None, 0), (w_uq, None, 0),
                        (pad_gain(wts["g_qa"][l] * (DQK_A ** -0.5)), None, 0), (cos_a_t, LANE, 0), (sin_a_t, LANE, 0)],
                       [(H_A * LANE, BF16)], "q_up")
    sl = slice(n_p, n_p + n_s)
    tm_s = min(tm, n_s)
    (q_a_s,) = _row_call(_q_up_kernel, n_s, tm_s,
                         [(z[sl], Q_LORA, COL_CQ // Q_LORA), (wts["g_cq"][l].reshape(1, Q_LORA), None, 0), (w_uq, None, 0),
                          (pad_gain(wts["g_qa"][l] * (DQK_A ** -0.5)), None, 0), (cos_a_t[sl], LANE, 0),
                          (sin_a_t[sl], LANE, 0)],
                         [(H_A * LANE, F32)], "q_up_sample")

    w_uk = jnp.pad(wts["w_uk"][l], ((0, 0), (0, 0), (0, LANE - NOPE_A))).reshape(KV_LORA, H_A * LANE)
    w_kv = jnp.concatenate([w_uk, wts["w_uv"][l].reshape(KV_LORA, H_A * V_A)], axis=1).astype(BF16)
    place = np.zeros((LANE, H_A * LANE), np.float32)
    for h in range(H_A):
        place[np.arange(ROPE_A), h * LANE + NOPE_A + np.arange(ROPE_A)] = 1.0
    k_a, v_a = _row_call(_kv_up_kernel, n_all, tm,
                         [(z, KV_LORA, COL_CKV // KV_LORA), (wts["g_ckv"][l].reshape(1, KV_LORA), None, 0),
                          (z, LANE, COL_MISC // LANE), (w_kv, None, 0), (jnp.asarray(place, BF16), None, 0),
                          (pad_gain(wts["g_ka"][l]), None, 0), (cos_a_t, LANE, 0), (sin_a_t, LANE, 0)],
                         [(H_A * LANE, BF16), (H_A * V_A, BF16)], "kv_up")

    expand = np.zeros((H_B * HEAD_DIM, H_B * KVH_B * HEAD_DIM), np.float32)
    for h in range(H_B):
        cols = h * KVH_B * HEAD_DIM + (h // GROUP_B) * HEAD_DIM + np.arange(HEAD_DIM)
        expand[h * HEAD_DIM + np.arange(HEAD_DIM), cols] = 1.0
    two = lambda g: jnp.tile(g, LANE // HEAD_DIM).reshape(1, LANE)
    g_ik_t = jnp.pad(wts["g_ik"][l], (ROPE_A, LANE - ROPE_A - D_I)).reshape(1, LANE)
    mla_rows, kv_rows, i_k, i_w, q_b_pad, k_b2, v_b2, i_q, i_k_bf = _row_call(
        _post_z_kernel, n_all, tm,
        [(z, COL_GATE, 0), (wts["g_ckv"][l].reshape(1, KV_LORA), None, 0),
         (two(wts["g_qb"][l] * (HEAD_DIM ** -0.5)), None, 0), (two(wts["g_kb"][l]), None, 0), (g_ik_t, None, 0),
         (cos_b_t, LANE, 0), (sin_b_t, LANE, 0), (cos_i_t, LANE, 0), (sin_i_t, LANE, 0),
         (jnp.asarray(expand, BF16), None, 0)],
        [(KV_LORA + ROPE_A, F32), (2 * KVH_B * HEAD_DIM, F32), (D_I, F32), (H_I, F32), (H_B * LANE, BF16),
         (KVH_B * HEAD_DIM, BF16), (KVH_B * HEAD_DIM, BF16), (H_I * D_I, BF16), (D_I, BF16)], "post_z")
    q_b_pad = q_b_pad.reshape(n_all, H_B, KVH_B * HEAD_DIM)
    i_q = i_q.reshape(n_all, H_I, D_I)

    tq = 128
    nq = t_pad // tq
    n_kc = max(dd for dd in range(1, 5) if nq % dd == 0)
    tk = n_kc * tq
    o_a_p = _mla_prompt(q_a[:n_p].reshape(n_b, t_pad, H_A * LANE), k_a[:n_p].reshape(n_b, t_pad, H_A * LANE),
                        v_a[:n_p].reshape(n_b, t_pad, H_A * V_A), tq=tk)

    blocked = lambda a: a.reshape((n_b, nq, tq) + a.shape[1:]).swapaxes(2, 3)
    o_b_p = _dsa_prompt(blocked(i_q[:n_p]), i_w[:n_p].reshape(n_b, t_pad, H_I), blocked(q_b_pad[:n_p]),
                        i_k_bf[:n_p].reshape(n_b, t_pad, D_I), k_b2[:n_p].reshape(n_b, t_pad, KVH_B * HEAD_DIM),
                        v_b2[:n_p].reshape(n_b, t_pad, KVH_B * HEAD_DIM), tq=tq, tk=tk, n_sel=n_sel_p)
    o_b_p = o_b_p.reshape(n_b, nq, KVH_B, GROUP_B, tq, KVH_B, HEAD_DIM)
    o_b_p = jnp.stack([o_b_p[:, :, g, :, :, g, :] for g in range(KVH_B)], axis=2)
    o_b_p = o_b_p.reshape(n_b, nq, H_B, tq, HEAD_DIM).swapaxes(2, 3).reshape(n_p, H_B * HEAD_DIM)

    pages = max(dd for dd in range(1, MAX_PAGES_PER_STEP + 1) if page_table.shape[1] % dd == 0)
    assert 8 % n_ds == 0
    n_tail = LANE
    tail_t = lambda a: jnp.pad(a.reshape(n_seq, n_ds, a.shape[-1]), ((0, 0), (0, n_tail - n_ds), (0, 0))).swapaxes(1, 2)
    cache_mla_t = jnp.swapaxes(cache_mla, 2, 3)
    cache_idx_t = jnp.swapaxes(cache_dsa_idx, 2, 3)
    cache_kv_t = jnp.moveaxis(cache_dsa_kv, 2, -1).reshape(cache_dsa_kv.shape[:2] + (2 * KVH_B * HEAD_DIM, PAGE_SIZE))
    g_ka = wts["g_ka"][l]
    pos_k = jnp.arange(past + n_tail)
    cos_k, sin_k = _rope_tables(pos_k, ROPE_A)
    g1, g2 = g_ka[NOPE_A:NOPE_A + ROPE_A // 2], g_ka[NOPE_A + ROPE_A // 2:]
    cg_t = jnp.concatenate([cos_k * g1, cos_k * g2], axis=1).T
    sg_t = jnp.concatenate([sin_k * g1, -sin_k * g2], axis=1).T
    qa_s = q_a_s.reshape(n_s, H_A, LANE)
    rows_a = n_ds * H_A
    eye_h = jnp.eye(H_A, dtype=F32)
    qn_s = qa_s[..., :NOPE_A] * g_ka[:NOPE_A]
    qbd = (qn_s[:, :, None, :] * eye_h[None, :, :, None]).reshape(n_seq, rows_a, H_A * NOPE_A)
    q_r = qa_s[:, :, NOPE_A:DQK_A].reshape(n_seq, rows_a, ROPE_A)
    q_r_swap = jnp.concatenate([q_r[..., ROPE_A // 2:], q_r[..., :ROPE_A // 2]], axis=-1)
    wt = wts["w_uk"][l].reshape(KV_LORA, H_A * NOPE_A).T
    o_lat = _mla_sample(l, page_table, cache_mla_t, wt.astype(BF16), qbd.astype(BF16), q_r.astype(BF16),
                        q_r_swap.astype(BF16), cg_t[:, :past], sg_t[:, :past], tail_t(mla_rows[sl]),
                        cg_t[:, past:], sg_t[:, past:], pages=pages, n_new=n_ds)
    o_a_s = jnp.einsum("bqhr,rhv->bqhv", o_lat.reshape(n_seq, n_ds, H_A, KV_LORA), wts["w_uv"][l])
    o_a_s = o_a_s.reshape(n_s, H_A * V_A)

    slot = jnp.arange(8) % n_ds
    iq_s = i_q[sl].reshape(n_seq, n_ds, H_I, D_I)[:, slot].swapaxes(1, 2).reshape(n_seq, H_I * 8, D_I)
    iw_s = i_w[sl].reshape(n_seq, n_ds, H_I)[:, slot].swapaxes(1, 2).reshape(n_seq, H_I * 8, 1)
    bias, bias_new = _dsa_select(l, page_table, cache_idx_t, iq_s, iw_s, tail_t(i_k_bf[sl]), pages=pages,
                                 n_new=n_ds, n_sel=n_sel_s)
    qb_s = q_b_pad[sl].reshape(n_seq, n_ds, H_B, KVH_B * HEAD_DIM).swapaxes(1, 2).reshape(n_seq, H_B * n_ds, -1)
    kv_new_t = tail_t(kv_rows[sl]).astype(BF16)
    o_b_s = _dsa_sample(l, page_table, cache_kv_t, qb_s, bias, kv_new_t, bias_new, pages=pages)
    o_b_s = o_b_s.reshape(n_seq, KVH_B, GROUP_B, n_ds, KVH_B, HEAD_DIM)
    o_b_s = jnp.stack([o_b_s[:, g, :, :, g, :] for g in range(KVH_B)], axis=1)
    o_b_s = o_b_s.reshape(n_seq, H_B, n_ds, HEAD_DIM).swapaxes(1, 2).reshape(n_s, H_B * HEAD_DIM)

    zeros_tail = jnp.zeros((n_all - n_p - n_s, H_A * V_A), BF16)
    o_a = jnp.concatenate([o_a_p.reshape(n_p, H_A * V_A), o_a_s.astype(BF16), zeros_tail], axis=0)
    o_b = jnp.concatenate([o_b_p, o_b_s.astype(BF16), zeros_tail], axis=0)
    assert COL_GATE % (2 * d) == 0
    x_all = _merge(x_all, o_a, o_b, z, wts["w_pa"][l].astype(BF16), wts["w_pb"][l].astype(BF16),
                   wts["w_o"][l].astype(BF16))
    x_all = _moe(x_all, wts["g_ffn_norm"][l], wts["w_router"][l], wts["b_router"][l], wts["w_mlp1"][l],
                 wts["b_mlp1"][l], wts["w_mlp2"][l], wts["b_mlp2"][l])

    new_p = (mla_rows[:n_p].reshape(n_b, t_pad, -1)[:, :t_real],
             kv_rows[:n_p].reshape(n_b, t_pad, 2, KVH_B, HEAD_DIM)[:, :t_real],
             i_k[:n_p].reshape(n_b, t_pad, D_I)[:, :t_real])
    new_s = (mla_rows[sl].reshape(n_seq, n_ds, -1), kv_rows[sl].reshape(n_seq, n_ds, 2, KVH_B, HEAD_DIM),
             i_k[sl].reshape(n_seq, n_ds, D_I))
    return x_all[:n_p].reshape(n_b, t_pad, d), x_all[sl].reshape(n_seq, n_ds, d), new_p, new_s


def kernel(x_prompt, x_sample, cache_mla, cache_dsa_kv, cache_dsa_idx, page_table, meta_tokens, g_attn_norm, w_in,
           g_cq, w_uq, g_qa, g_ckv, w_uk, w_uv, g_ka, g_qb, g_kb, g_ik, w_pa, w_pb, w_o, g_ffn_norm, w_router,
           b_router, w_mlp1, b_mlp1, w_mlp2, b_mlp2):
    n_b, n_s, d = x_prompt.shape
    t_real = n_s + N_META
    t_pad = _round_up(t_real, LANE)
    wts = dict(g_attn_norm=g_attn_norm, w_in=w_in, g_cq=g_cq, w_uq=w_uq, g_qa=g_qa, g_ckv=g_ckv, w_uk=w_uk, w_uv=w_uv,
               g_ka=g_ka, g_qb=g_qb, g_kb=g_kb, g_ik=g_ik, w_pa=w_pa, w_pb=w_pb, w_o=w_o, g_ffn_norm=g_ffn_norm,
               w_router=w_router, b_router=b_router, w_mlp1=w_mlp1, b_mlp1=b_mlp1, w_mlp2=w_mlp2, b_mlp2=b_mlp2)
    meta = jnp.broadcast_to(meta_tokens[None].astype(x_prompt.dtype), (n_b, N_META, d))
    xp = jnp.concatenate([meta, x_prompt, jnp.zeros((n_b, t_pad - t_real, d), x_prompt.dtype)], axis=1)
    xs = x_sample
    outs_p, outs_s = [], []
    for l in range(w_in.shape[0]):
        xp, xs, new_p, new_s = _layer(l, xp, xs, page_table, (cache_mla, cache_dsa_kv, cache_dsa_idx), wts,
                                      t_real=t_real)
        outs_p.append(new_p)
        outs_s.append(new_s)
    stack = lambda outs, i: jnp.stack([o[i] for o in outs], axis=0)
    return (xp[:, N_META:t_real], xs, stack(outs_p, 0), stack(outs_p, 1), stack(outs_p, 2),
            stack(outs_s, 0), stack(outs_s, 1), stack(outs_s, 2))
```

```python
import functools

import jax
import jax.numpy as jnp
import numpy as np
from jax import lax
from jax.experimental import pallas as pl
from jax.experimental.pallas import tpu as pltpu

N_META = 16
HEAD_DIM = 64
H_A = 8
NOPE_A = 64
ROPE_A = 32
DQK_A = NOPE_A + ROPE_A
V_A = 64
Q_LORA = 512
KV_LORA = 256
H_B = 8
KVH_B = 2
GROUP_B = H_B // KVH_B
H_I = 8
D_I = 32
D_I_ROPE = 16
MAX_SEL = 256
N_EXPERTS = 32
TOP_K = 4
SWIGLU_LIMIT = 7.0
SWIGLU_ALPHA = 1.702
ROPE_THETA = 10000.0
NORM_EPS = 1e-6
PAGE_SIZE = 128

LANE = 128
MASK_VALUE = -1e30
INT_MIN = -(2 ** 31)
VMEM_LIMIT = 56 * 1024 * 1024
MAX_PAGES_PER_STEP = 16

COL_CQ = 0
COL_CKV = 512
COL_QB = 768
COL_KB = 1280
COL_VB = 1408
COL_IQ = 1536
COL_MISC = 1792
COL_GATE = 2048
N_PROJ = 4096

F32 = jnp.float32
BF16 = jnp.bfloat16
NT_DIMS = (((1,), (1,)), ((), ()))


def _cdiv(a, b):
    return (a + b - 1) // b


def _round_up(a, b):
    return _cdiv(a, b) * b


def _params(*sem):
    return pltpu.CompilerParams(dimension_semantics=sem, vmem_limit_bytes=VMEM_LIMIT)


def _mm_kernel(x_ref, w_ref, o_ref):
    o_ref[...] = jnp.dot(x_ref[...].astype(BF16), w_ref[...], preferred_element_type=F32).astype(o_ref.dtype)


def _mm_norm_kernel(x_ref, g_ref, w_ref, o_ref, xn_ref):
    @pl.when(pl.program_id(1) == 0)
    def _():
        x = x_ref[...].astype(F32)
        ms = jnp.mean(x * x, axis=-1, keepdims=True)
        xn_ref[...] = (x * lax.rsqrt(ms + NORM_EPS) * g_ref[...]).astype(BF16)

    o_ref[...] = jnp.dot(xn_ref[...], w_ref[...], preferred_element_type=F32).astype(o_ref.dtype)


def _matmul(x, w, *, gain=None, col_block=0, k=None, tm=512, tn=512, out_dtype=F32, name="mm"):
    m = x.shape[0]
    k = x.shape[1] if k is None else k
    n = w.shape[1]
    tn = min(tn, n)
    assert m % tm == 0 and n % tn == 0 and w.shape[0] == k
    grid = (m // tm, n // tn)
    x_spec = pl.BlockSpec((tm, k), lambda i, j: (i, col_block))
    w_spec = pl.BlockSpec((k, tn), lambda i, j: (0, j))
    o_spec = pl.BlockSpec((tm, tn), lambda i, j: (i, j))
    out_shape = jax.ShapeDtypeStruct((m, n), out_dtype)
    if gain is None:
        return pl.pallas_call(_mm_kernel, out_shape=out_shape, grid=grid, in_specs=[x_spec, w_spec],
                              out_specs=o_spec, compiler_params=_params("parallel", "arbitrary"), name=name)(x, w)
    g_spec = pl.BlockSpec((1, k), lambda i, j: (0, 0))
    return pl.pallas_call(_mm_norm_kernel, out_shape=out_shape, grid=grid, in_specs=[x_spec, g_spec, w_spec],
                          out_specs=o_spec, scratch_shapes=[pltpu.VMEM((tm, k), BF16)],
                          compiler_params=_params("parallel", "arbitrary"), name=name)(
                              x, gain.reshape(1, k).astype(F32), w)


def _lane_ids(tm):
    return lax.broadcasted_iota(jnp.int32, (tm, LANE), 1)


def _rotate_pairs(t, cos_t, sin_t, lane, group, split, half):
    first = (lane & (group - 1)) < split
    partner = jnp.where(first, pltpu.roll(t, LANE - half, 1), pltpu.roll(t, half, 1))
    return t * cos_t + partner * sin_t


def _head_tiles(x, fn):
    tiles = [fn(x[:, i * LANE:(i + 1) * LANE]) for i in range(x.shape[1] // LANE)]
    return tiles[0] if len(tiles) == 1 else jnp.concatenate(tiles, axis=1)


def _mla_head_post(t, gain, cos_t, sin_t, lane):
    ms = jnp.sum(t * t, axis=1, keepdims=True) * (1.0 / DQK_A)
    return _rotate_pairs(t * lax.rsqrt(ms + NORM_EPS) * gain, cos_t, sin_t, lane, LANE, NOPE_A + ROPE_A // 2,
                         ROPE_A // 2)


def _q_up_kernel(z_ref, gz_ref, w_ref, gq_ref, cos_ref, sin_ref, o_ref):
    x = z_ref[...]
    xn = (x * lax.rsqrt(jnp.mean(x * x, axis=-1, keepdims=True) + NORM_EPS) * gz_ref[...]).astype(BF16)
    q = jnp.dot(xn, w_ref[...], preferred_element_type=F32)
    lane = _lane_ids(x.shape[0])
    post = lambda t: _mla_head_post(t, gq_ref[...], cos_ref[...], sin_ref[...], lane)
    o_ref[...] = _head_tiles(q, post).astype(o_ref.dtype)


def _kv_up_kernel(z_ref, gz_ref, misc_ref, w_ref, place_ref, gk_ref, cos_ref, sin_ref, k_ref, v_ref):
    x = z_ref[...]
    c = (x * lax.rsqrt(jnp.mean(x * x, axis=-1, keepdims=True) + NORM_EPS) * gz_ref[...]).astype(BF16)
    kv = jnp.dot(c, w_ref[...], preferred_element_type=F32)
    n_k = H_A * LANE
    misc = misc_ref[...]
    hi = misc.astype(BF16)
    lo = (misc - hi.astype(F32)).astype(BF16)
    k = kv[:, :n_k] + (jnp.dot(hi, place_ref[...], preferred_element_type=F32)
                       + jnp.dot(lo, place_ref[...], preferred_element_type=F32))
    lane = _lane_ids(x.shape[0])
    post = lambda t: _mla_head_post(t, gk_ref[...], cos_ref[...], sin_ref[...], lane)
    k_ref[...] = _head_tiles(k, post).astype(k_ref.dtype)
    v_ref[...] = kv[:, n_k:].astype(v_ref.dtype)


def _row_call(kernel_fn, n, tm, ins, outs, name):
    def spec(width, col_block, shape):
        if width is None:
            return pl.BlockSpec(shape, lambda i: tuple(0 for _ in shape))
        return pl.BlockSpec((tm, width), lambda i: (i, col_block))
    in_specs = [spec(w, cb, a.shape) for a, w, cb in ins]
    out_specs = tuple(pl.BlockSpec((tm, w), lambda i: (i, 0)) for w, _ in outs)
    out_shape = tuple(jax.ShapeDtypeStruct((n, w), dt) for w, dt in outs)
    return pl.pallas_call(kernel_fn, out_shape=out_shape, grid=(n // tm,), in_specs=in_specs, out_specs=out_specs,
                          compiler_params=_params("parallel"), name=name)(*[a for a, _, _ in ins])


def _post_z_kernel(z_ref, gckv_ref, gqb_ref, gkb_ref, gik_ref, cb_ref, sb_ref, ci_ref, si_ref, expand_ref,
                   mla_ref, kv_ref, ik_ref, iw_ref, qb_ref, k2_ref, v2_ref, iq_ref, ikb_ref):
    z = z_ref[...]
    tm = z.shape[0]
    lane = _lane_ids(tm)
    cb, sb, ci, si = cb_ref[...], sb_ref[...], ci_ref[...], si_ref[...]
    misc = z[:, COL_MISC:COL_MISC + LANE]

    x = z[:, COL_CKV:COL_CKV + KV_LORA]
    mla_ref[:, 0:KV_LORA] = x * lax.rsqrt(jnp.mean(x * x, axis=-1, keepdims=True) + NORM_EPS) * gckv_ref[...]
    mla_ref[:, KV_LORA:] = misc[:, 0:ROPE_A]

    low = lane < HEAD_DIM

    def head_pair(t, gain):
        t2 = t * t
        ms = jnp.where(low, jnp.sum(jnp.where(low, t2, 0.0), axis=1, keepdims=True),
                       jnp.sum(jnp.where(low, 0.0, t2), axis=1, keepdims=True)) * (1.0 / HEAD_DIM)
        return _rotate_pairs(t * lax.rsqrt(ms + NORM_EPS) * gain, cb, sb, lane, HEAD_DIM, HEAD_DIM // 2, HEAD_DIM // 2)

    qb = _head_tiles(z[:, COL_QB:COL_QB + H_B * HEAD_DIM], lambda t: head_pair(t, gqb_ref[...])).astype(BF16)
    qb_ref[...] = jnp.dot(qb, expand_ref[...], preferred_element_type=F32).astype(qb_ref.dtype)

    kb = head_pair(z[:, COL_KB:COL_KB + LANE], gkb_ref[...])
    vb = z[:, COL_VB:COL_VB + LANE]
    kv_ref[:, 0:LANE] = kb
    kv_ref[:, LANE:] = vb
    k2_ref[...] = kb.astype(k2_ref.dtype)
    v2_ref[...] = vb.astype(v2_ref.dtype)

    rot_i = lambda t: _rotate_pairs(t, ci, si, lane, D_I, D_I - D_I_ROPE // 2, D_I_ROPE // 2)
    iq_ref[...] = _head_tiles(z[:, COL_IQ:COL_IQ + H_I * D_I], rot_i).astype(iq_ref.dtype)

    in_ik = (lane >= ROPE_A) & (lane < ROPE_A + D_I)
    ms = jnp.sum(jnp.where(in_ik, misc * misc, 0.0), axis=1, keepdims=True) * (1.0 / D_I)
    ik = pltpu.roll(rot_i(misc * lax.rsqrt(ms + NORM_EPS) * gik_ref[...]), LANE - ROPE_A, 1)
    ik_ref[...] = ik[:, 0:D_I]
    ikb_ref[...] = ik[:, 0:D_I].astype(ikb_ref.dtype)
    iw_ref[...] = pltpu.roll(misc, LANE - ROPE_A - D_I, 1)[:, 0:H_I] * ((H_I ** -0.5) * (D_I ** -0.5))


def _merge_kernel(x_ref, oa_ref, ob_ref, zg_ref, wpa_ref, wpb_ref, wo_ref, o_ref):
    d = x_ref.shape[1]
    y_a = jnp.dot(oa_ref[...], wpa_ref[...], preferred_element_type=F32)
    y_b = jnp.dot(ob_ref[...], wpb_ref[...], preferred_element_type=F32)
    zg = zg_ref[...]
    merged = jax.nn.sigmoid(zg[:, :d]) * y_a + jax.nn.sigmoid(zg[:, d:]) * y_b
    o_ref[...] = x_ref[...] + jnp.dot(merged.astype(BF16), wo_ref[...], preferred_element_type=F32)


def _merge(x, o_a, o_b, z, w_pa, w_pb, w_o, *, tm=512):
    n, d = x.shape
    ka, kb = o_a.shape[1], o_b.shape[1]
    const = lambda shape: pl.BlockSpec(shape, lambda i: (0, 0))
    return pl.pallas_call(
        _merge_kernel,
        out_shape=jax.ShapeDtypeStruct((n, d), F32),
        grid=(n // tm,),
        in_specs=[pl.BlockSpec((tm, d), lambda i: (i, 0)), pl.BlockSpec((tm, ka), lambda i: (i, 0)),
                  pl.BlockSpec((tm, kb), lambda i: (i, 0)), pl.BlockSpec((tm, 2 * d), lambda i: (i, COL_GATE // (2 * d))),
                  const((ka, d)), const((kb, d)), const((d, d))],
        out_specs=pl.BlockSpec((tm, d), lambda i: (i, 0)),
        compiler_params=_params("parallel"),
        name="merge")(x, o_a, o_b, z, w_pa, w_pb, w_o)


def _sortable_key(x):
    b = lax.bitcast_convert_type(x + 0.0, jnp.int32)
    return jnp.where(b < 0, b ^ jnp.int32(0x7FFFFFFF), b)


KEY_NEG_INF = INT_MIN + 0x7FFFFF


def _has_excess_ties(n_ge, thr, n_sel):
    return jnp.max(jnp.where(n_ge > n_sel, jnp.where(thr > KEY_NEG_INF, 1, 0), 0)) > 0


def _kth_largest_key(count_ge, n_sel, rows):
    def step(s, t):
        bit = lax.shift_left(jnp.int32(1), jnp.int32(31) - s)
        cand = t + bit
        return jnp.where(count_ge(cand) >= n_sel, cand, t)

    return lax.fori_loop(0, 32, step, jnp.full((rows, 1), INT_MIN, jnp.int32))


MLA_HEADS_PER_STEP = 4


def _mla_prompt_kernel(q_ref, k_ref, v_ref, o_ref, *, tq):
    i = pl.program_id(2)
    nh = MLA_HEADS_PER_STEP
    q = q_ref[...]
    causal = lax.broadcasted_iota(jnp.int32, (tq, tq), 1) <= lax.broadcasted_iota(jnp.int32, (tq, tq), 0)

    def chunk(c, carry, diagonal):
        m, l, acc = carry
        start = pl.multiple_of(c * tq, tq)
        k_blk = k_ref[pl.ds(start, tq), :]
        v_blk = v_ref[pl.ds(start, tq), :]
        s = jnp.concatenate([lax.dot_general(q[:, h * LANE:(h + 1) * LANE], k_blk[:, h * LANE:(h + 1) * LANE],
                                             NT_DIMS, preferred_element_type=F32) for h in range(nh)], axis=0)
        if diagonal:
            s = jnp.where(causal[None], s.reshape(nh, tq, tq), MASK_VALUE).reshape(nh * tq, tq)
        m_new = jnp.maximum(m, jnp.max(s, axis=1, keepdims=True))
        alpha = jnp.exp(m - m_new)
        p = jnp.exp(s - m_new)
        l_new = alpha * l + jnp.sum(p, axis=1, keepdims=True)
        p = p.astype(BF16)
        pv = jnp.concatenate([jnp.dot(p[h * tq:(h + 1) * tq], v_blk[:, (h // 2) * LANE:(h // 2 + 1) * LANE],
                                      preferred_element_type=F32) for h in range(nh)], axis=0)
        return m_new, l_new, alpha * acc + pv

    init = (jnp.full((nh * tq, 1), MASK_VALUE, F32), jnp.zeros((nh * tq, 1), F32), jnp.zeros((nh * tq, LANE), F32))
    carry = lax.fori_loop(0, i, functools.partial(chunk, diagonal=False), init)
    _, l, acc = chunk(i, carry, True)
    o = acc / l
    lane = lax.broadcasted_iota(jnp.int32, (tq, LANE), 1)
    pairs = [jnp.where(lane < V_A, o[(2 * g) * tq:(2 * g + 1) * tq], o[(2 * g + 1) * tq:(2 * g + 2) * tq])
             for g in range(nh // 2)]
    o_ref[...] = jnp.concatenate(pairs, axis=1).astype(o_ref.dtype)


def _mla_prompt(q, k, v, *, tq):
    b, t, _ = q.shape
    nh = MLA_HEADS_PER_STEP
    grid = (b, H_A // nh, t // tq)
    return pl.pallas_call(
        functools.partial(_mla_prompt_kernel, tq=tq),
        out_shape=jax.ShapeDtypeStruct((b, t, H_A * V_A), BF16),
        grid=grid,
        in_specs=[pl.BlockSpec((None, tq, nh * LANE), lambda bi, g, i: (bi, i, g)),
                  pl.BlockSpec((None, t, nh * LANE), lambda bi, g, i: (bi, 0, g)),
                  pl.BlockSpec((None, t, nh * V_A), lambda bi, g, i: (bi, 0, g))],
        out_specs=pl.BlockSpec((None, tq, nh * V_A), lambda bi, g, i: (bi, i, g)),
        compiler_params=_params("parallel", "parallel", "arbitrary"),
        name="mla_prompt")(q, k, v)


def _select_bias(key, thr, need, carry, valid, tri_ref):
    w = min(tri_ref.shape[0], key.shape[1])
    tri = tri_ref[0:w, 0:w]
    parts = []
    for t in range(key.shape[1] // w):
        k_t = key[:, t * w:(t + 1) * w]
        eq = k_t == thr
        rank = jnp.dot(jnp.where(eq, 1.0, 0.0).astype(BF16), tri, preferred_element_type=F32) + carry
        carry = jnp.max(rank, axis=1, keepdims=True)
        tie = jnp.where(rank <= need, 0.0, MASK_VALUE)
        parts.append(jnp.where(k_t > thr, 0.0, jnp.where(eq, tie, MASK_VALUE)))
    bias = parts[0] if len(parts) == 1 else jnp.concatenate(parts, axis=1)
    if valid is not None:
        bias = jnp.where(valid, bias, MASK_VALUE)
    return bias, carry


def _dsa_prompt_kernel(iq_ref, w_ref, qt_ref, ik_ref, k_ref, vt_ref, tri_ref, o_ref, key_ref, bias_ref, wb_ref,
                       *, tq, tk, n_sel):
    i = pl.program_id(1)
    n_c = ((i + 1) * tq + tk - 1) // tk
    q_pos = i * tq + lax.broadcasted_iota(jnp.int32, (tq, tk), 0)
    lane_pos = lax.broadcasted_iota(jnp.int32, (tq, tk), 1)

    w = w_ref[...]
    for h in range(H_I):
        wb_ref[h] = jnp.broadcast_to(w[:, h:h + 1], (tq, LANE))
    iq = iq_ref[...].reshape(H_I * tq, D_I)

    def score_chunk(c, _):
        start = pl.multiple_of(c * tk, tk)
        logit = lax.dot_general(iq, ik_ref[pl.ds(start, tk), :], NT_DIMS, preferred_element_type=F32)
        cols = []
        for t in range(tk // LANE):
            acc = jnp.zeros((tq, LANE), F32)
            for h in range(H_I):
                acc = acc + jnp.maximum(logit[h * tq:(h + 1) * tq, t * LANE:(t + 1) * LANE], 0.0) * wb_ref[h]
            cols.append(acc)
        score = jnp.concatenate(cols, axis=1)
        score = jnp.where(start + lane_pos <= q_pos, score, -jnp.inf)
        key_ref[c] = _sortable_key(score)
        return 0

    lax.fori_loop(0, n_c, score_chunk, 0)

    def count_ge(cand):
        cand_b = jnp.broadcast_to(cand, (tq, LANE))

        def body(c, cnt):
            key = key_ref[c]
            for t in range(tk // LANE):
                cnt = cnt + jnp.where(key[:, t * LANE:(t + 1) * LANE] >= cand_b, 1, 0)
            return cnt

        cnt = lax.fori_loop(0, n_c, body, jnp.zeros((tq, LANE), jnp.int32))
        return jnp.sum(cnt, axis=1, keepdims=True)

    thr = _kth_largest_key(count_ge, n_sel, tq)

    def bias_by_threshold():
        thr_eff = jnp.maximum(thr, KEY_NEG_INF + 1)

        def body(c, _):
            bias_ref[c] = jnp.where(key_ref[c] >= thr_eff, 0.0, MASK_VALUE).T
            return 0

        lax.fori_loop(0, n_c, body, 0)

    def bias_with_ties():
        need = (n_sel - count_ge(thr + 1)).astype(F32)

        def body(c, carry):
            start = pl.multiple_of(c * tk, tk)
            bias, carry = _select_bias(key_ref[c], thr, need, carry, start + lane_pos <= q_pos, tri_ref)
            bias_ref[c] = bias.T
            return carry

        lax.fori_loop(0, n_c, body, jnp.zeros((tq, 1), F32))

    lax.cond(_has_excess_ties(count_ge(thr), thr, n_sel), bias_with_ties, bias_by_threshold)

    rows = H_B * tq
    kw = KVH_B * HEAD_DIM
    qt = qt_ref[...]

    def attend(c, carry):
        m, l, acc = carry
        start = pl.multiple_of(c * tk, tk)
        s = jnp.dot(k_ref[pl.ds(start, tk), :], qt, preferred_element_type=F32)
        bias_t = bias_ref[c]
        s = jnp.concatenate([s[:, h * tq:(h + 1) * tq] + bias_t for h in range(H_B)], axis=1)
        m_new = jnp.maximum(m, jnp.max(s, axis=0, keepdims=True))
        alpha = jnp.exp(m - m_new)
        p = jnp.exp(s - m_new)
        l_new = alpha * l + jnp.sum(p, axis=0, keepdims=True)
        pv = jnp.dot(vt_ref[c], p.astype(BF16), preferred_element_type=F32)
        return m_new, l_new, alpha * acc + pv

    m, l, acc = lax.fori_loop(0, n_c, attend, (jnp.full((1, rows), MASK_VALUE, F32),
                                               jnp.zeros((1, rows), F32), jnp.zeros((kw, rows), F32)))
    o_ref[...] = (acc / l).astype(o_ref.dtype)


def _dsa_prompt(iq, iw, qt, ik, k, vt, *, tq, tk, n_sel):
    b, nq = iq.shape[:2]
    t = ik.shape[1]
    n_chunks = t // tk
    kw = KVH_B * HEAD_DIM
    assert tq == LANE
    tri_w = 2 * LANE if tk % (2 * LANE) == 0 else LANE
    tri = jnp.asarray(np.arange(tri_w)[:, None] <= np.arange(tri_w)[None, :], BF16)
    return pl.pallas_call(
        functools.partial(_dsa_prompt_kernel, tq=tq, tk=tk, n_sel=n_sel),
        out_shape=jax.ShapeDtypeStruct((b, nq, kw, H_B * tq), BF16),
        grid=(b, nq),
        in_specs=[pl.BlockSpec((None, None, H_I, tq, D_I), lambda bi, i: (bi, i, 0, 0, 0)),
                  pl.BlockSpec((None, tq, H_I), lambda bi, i: (bi, i, 0)),
                  pl.BlockSpec((None, None, kw, H_B * tq), lambda bi, i: (bi, i, 0, 0)),
                  pl.BlockSpec((None, t, D_I), lambda bi, i: (bi, 0, 0)),
                  pl.BlockSpec((None, t, kw), lambda bi, i: (bi, 0, 0)),
                  pl.BlockSpec((None, n_chunks, kw, tk), lambda bi, i: (bi, 0, 0, 0)),
                  pl.BlockSpec((tri_w, tri_w), lambda bi, i: (0, 0))],
        out_specs=pl.BlockSpec((None, None, kw, H_B * tq), lambda bi, i: (bi, i, 0, 0)),
        scratch_shapes=[pltpu.VMEM((n_chunks, tq, tk), jnp.int32),
                        pltpu.VMEM((n_chunks, tk, tq), F32),
                        pltpu.VMEM((H_I, tq, LANE), F32)],
        compiler_params=_params("parallel", "arbitrary"),
        name="dsa_prompt")(iq, iw, qt, ik, k, vt, tri)


def _seq_spec(shape):
    return pl.BlockSpec((None,) + shape, lambda b, j, pt: (b,) + tuple(0 for _ in shape))


def _const_spec(shape):
    return pl.BlockSpec(shape, lambda b, j, pt: tuple(0 for _ in shape))


HBM_SPEC = pl.BlockSpec(memory_space=pl.ANY)


def _page_copy(cache_hbm, buf, sem, layer, page, slot, p):
    return pltpu.make_async_copy(cache_hbm.at[layer, page], buf.at[slot, :, pl.ds(p * PAGE_SIZE, PAGE_SIZE)],
                                 sem.at[slot])


def _stream_pages(pt_ref, cache_hbm, buf, sem, *, layer, pages):
    b = pl.program_id(0)
    j = pl.program_id(1)
    n_seq = pl.num_programs(0)
    n_steps = pl.num_programs(1)
    t = b * n_steps + j
    slot = lax.rem(t, 2)

    def start(seq, step, dst_slot):
        for p in range(pages):
            _page_copy(cache_hbm, buf, sem, layer, pt_ref[seq, step * pages + p], dst_slot, p).start()

    @pl.when(t == 0)
    def _():
        start(b, j, slot)

    @pl.when(t + 1 < n_seq * n_steps)
    def _():
        wrap = j + 1 == n_steps
        start(jnp.where(wrap, b + 1, b), jnp.where(wrap, 0, j + 1), 1 - slot)

    for p in range(pages):
        _page_copy(cache_hbm, buf, sem, layer, 0, slot, p).wait()
    return slot


def _page_scratch(pages, width):
    return [pltpu.VMEM((2, width, pages * PAGE_SIZE), F32), pltpu.SemaphoreType.DMA((2,))]


def _softmax_step_t(s, m_ref, l_ref, acc_ref, v_t):
    m_prev = m_ref[...]
    m_new = jnp.maximum(m_prev, jnp.max(s, axis=1, keepdims=True))
    alpha = jnp.exp(m_prev - m_new)
    p = jnp.exp(s - m_new)
    l_ref[...] = alpha * l_ref[...] + jnp.sum(p, axis=1, keepdims=True)
    acc_ref[...] = alpha * acc_ref[...] + lax.dot_general(p.astype(BF16), v_t, NT_DIMS, preferred_element_type=F32)
    m_ref[...] = m_new


def _init_softmax(m_ref, l_ref, acc_ref):
    m_ref[...] = jnp.full(m_ref.shape, MASK_VALUE, F32)
    l_ref[...] = jnp.zeros(l_ref.shape, F32)
    acc_ref[...] = jnp.zeros(acc_ref.shape, F32)


def _mla_sample_kernel(pt_ref, cache_hbm, wt_ref, qbd_ref, qr_ref, qs_ref, cg_ref, sg_ref, new_ref, cg_new_ref,
                       sg_new_ref, o_ref, buf, sem, lhs_ref, m_ref, l_ref, acc_ref, *, layer, pages, n_new):
    j = pl.program_id(1)
    rows = o_ref.shape[0]
    n_k = H_A * NOPE_A
    slot = _stream_pages(pt_ref, cache_hbm, buf, sem, layer=layer, pages=pages)

    @pl.when(j == 0)
    def _():
        wt = wt_ref[...]
        lhs_ref[0:n_k, :] = wt
        lhs_ref[n_k:, :] = jnp.dot(qbd_ref[...], wt, preferred_element_type=F32).astype(BF16)
        _init_softmax(m_ref, l_ref, acc_ref)

    def attend(c_t, pe_t, cg, sg, mask):
        n = c_t.shape[1]
        big = jnp.dot(lhs_ref[...], c_t, preferred_element_type=F32)
        ssq = jnp.zeros((H_A, n), F32)
        row_id = lax.broadcasted_iota(jnp.int32, (H_A, n), 0)
        for h in range(H_A):
            kh = big[h * NOPE_A:(h + 1) * NOPE_A]
            ssq = jnp.where(row_id == h, jnp.sum(kh * kh, axis=0, keepdims=True), ssq)
        ssq = ssq + jnp.sum(pe_t * pe_t, axis=0, keepdims=True)
        inv_rms = lax.rsqrt(ssq * (1.0 / DQK_A) + NORM_EPS)
        s_rope = (jnp.dot(qr_ref[...], (pe_t * cg).astype(BF16), preferred_element_type=F32)
                  + jnp.dot(qs_ref[...], (pe_t * sg).astype(BF16), preferred_element_type=F32))
        s = ((big[n_k:] + s_rope).reshape(rows // H_A, H_A, n) * inv_rms[None]).reshape(rows, n)
        if mask is not None:
            s = jnp.where(mask, s, MASK_VALUE)
        _softmax_step_t(s, m_ref, l_ref, acc_ref, c_t)

    attend(buf[slot, 0:KV_LORA, :].astype(BF16), buf[slot, KV_LORA:, :], cg_ref[...], sg_ref[...], None)

    @pl.when(j == pl.num_programs(1) - 1)
    def _():
        new = new_ref[...]
        n = new.shape[1]
        key_i = lax.broadcasted_iota(jnp.int32, (rows, n), 1)
        q_i = lax.broadcasted_iota(jnp.int32, (rows, n), 0) // H_A
        attend(new[:KV_LORA].astype(BF16), new[KV_LORA:], cg_new_ref[...], sg_new_ref[...],
               (key_i <= q_i) & (key_i < n_new))
        o_ref[...] = acc_ref[...] / l_ref[...]


def _mla_sample(layer, page_table, cache_t, wt, qbd, qr, qs, cg_t, sg_t, new_t, cg_new_t, sg_new_t, *, pages, n_new):
    n_seq, n_pages = page_table.shape
    rows = qbd.shape[1]
    n_main = pages * PAGE_SIZE
    n_tail = new_t.shape[2]
    width = KV_LORA + ROPE_A
    in_specs = [HBM_SPEC,
                _const_spec((H_A * NOPE_A, KV_LORA)),
                _seq_spec((rows, H_A * NOPE_A)), _seq_spec((rows, ROPE_A)), _seq_spec((rows, ROPE_A)),
                pl.BlockSpec((ROPE_A, n_main), lambda b, j, pt: (0, j)),
                pl.BlockSpec((ROPE_A, n_main), lambda b, j, pt: (0, j)),
                _seq_spec((width, n_tail)), _const_spec((ROPE_A, n_tail)), _const_spec((ROPE_A, n_tail))]
    return pl.pallas_call(
        functools.partial(_mla_sample_kernel, layer=layer, pages=pages, n_new=n_new),
        out_shape=jax.ShapeDtypeStruct((n_seq, rows, KV_LORA), F32),
        grid_spec=pltpu.PrefetchScalarGridSpec(
            num_scalar_prefetch=1, grid=(n_seq, n_pages // pages), in_specs=in_specs,
            out_specs=pl.BlockSpec((None, rows, KV_LORA), lambda b, j, pt: (b, 0, 0)),
            scratch_shapes=_page_scratch(pages, width) + [
                pltpu.VMEM((H_A * NOPE_A + rows, KV_LORA), BF16),
                pltpu.VMEM((rows, 1), F32), pltpu.VMEM((rows, 1), F32), pltpu.VMEM((rows, KV_LORA), F32)]),
        compiler_params=_params("arbitrary", "arbitrary"),
        name="mla_sample")(page_table, cache_t, wt, qbd, qr, qs, cg_t, sg_t, new_t, cg_new_t, sg_new_t)


def _dsa_select_kernel(pt_ref, cache_hbm, iq_ref, w_ref, ik_new_ref, tri_ref, bias_ref, bias_new_ref, buf, sem,
                       key_ref, key_new_ref, *, layer, pages, n_new, n_sel):
    j = pl.program_id(1)
    n_steps = key_ref.shape[0]
    n_main = key_ref.shape[2]
    n_tail = ik_new_ref.shape[1]
    nq = 8
    slot = _stream_pages(pt_ref, cache_hbm, buf, sem, layer=layer, pages=pages)

    def scores(ik_t):
        logit = jnp.dot(iq_ref[...], ik_t, preferred_element_type=F32)
        weighted = jnp.maximum(logit, 0.0) * w_ref[...]
        return jnp.sum(weighted.reshape(H_I, nq, ik_t.shape[1]), axis=0)

    key_ref[j] = _sortable_key(scores(buf[slot].astype(BF16)))

    @pl.when(j == n_steps - 1)
    def _():
        key_i = lax.broadcasted_iota(jnp.int32, (nq, n_tail), 1)
        q_i = lax.broadcasted_iota(jnp.int32, (nq, n_tail), 0) % n_new
        valid_new = (key_i <= q_i) & (key_i < n_new)
        key_new_ref[...] = _sortable_key(jnp.where(valid_new, scores(ik_new_ref[...]), -jnp.inf))

        def count_ge(cand):
            cand_b = jnp.broadcast_to(cand, (nq, LANE))
            cnts = [jnp.zeros((nq, LANE), jnp.int32) for _ in range(4)]
            tiles = [key_ref[c, :, t * LANE:(t + 1) * LANE] for c in range(n_steps) for t in range(n_main // LANE)]
            tiles += [key_new_ref[:, t * LANE:(t + 1) * LANE] for t in range(n_tail // LANE)]
            for idx, tile in enumerate(tiles):
                cnts[idx % 4] = cnts[idx % 4] + jnp.where(tile >= cand_b, 1, 0)
            return jnp.sum((cnts[0] + cnts[1]) + (cnts[2] + cnts[3]), axis=1, keepdims=True)

        thr = _kth_largest_key(count_ge, n_sel, nq)

        def bias_by_threshold():
            thr_eff = jnp.maximum(thr, KEY_NEG_INF + 1)

            def body(c, _):
                bias_ref[c] = jnp.where(key_ref[c] >= thr_eff, 0.0, MASK_VALUE)
                return 0

            lax.fori_loop(0, n_steps, body, 0)
            bias_new_ref[...] = jnp.where(key_new_ref[...] >= thr_eff, 0.0, MASK_VALUE)

        def bias_with_ties():
            need = (n_sel - count_ge(thr + 1)).astype(F32)

            def body(c, carry):
                bias, carry = _select_bias(key_ref[c], thr, need, carry, None, tri_ref)
                bias_ref[c] = bias
                return carry

            carry = lax.fori_loop(0, n_steps, body, jnp.zeros((nq, 1), F32))
            bias_new, _ = _select_bias(key_new_ref[...], thr, need, carry, valid_new, tri_ref)
            bias_new_ref[...] = bias_new

        lax.cond(_has_excess_ties(count_ge(thr), thr, n_sel), bias_with_ties, bias_by_threshold)


def _dsa_select(layer, page_table, cache_idx_t, iq, iw, ik_new_t, *, pages, n_new, n_sel):
    n_seq, n_pages = page_table.shape
    n_steps = n_pages // pages
    n_main = pages * PAGE_SIZE
    n_tail = ik_new_t.shape[2]
    rows = iq.shape[1]
    tri = jnp.asarray(np.arange(2 * LANE)[:, None] <= np.arange(2 * LANE)[None, :], BF16)
    in_specs = [HBM_SPEC, _seq_spec((rows, D_I)), _seq_spec((rows, 1)), _seq_spec((D_I, n_tail)),
                _const_spec((2 * LANE, 2 * LANE))]
    return pl.pallas_call(
        functools.partial(_dsa_select_kernel, layer=layer, pages=pages, n_new=n_new, n_sel=n_sel),
        out_shape=(jax.ShapeDtypeStruct((n_seq, n_steps, 8, n_main), F32),
                   jax.ShapeDtypeStruct((n_seq, 8, n_tail), F32)),
        grid_spec=pltpu.PrefetchScalarGridSpec(
            num_scalar_prefetch=1, grid=(n_seq, n_steps), in_specs=in_specs,
            out_specs=(pl.BlockSpec((None, n_steps, 8, n_main), lambda b, j, pt: (b, 0, 0, 0)),
                       pl.BlockSpec((None, 8, n_tail), lambda b, j, pt: (b, 0, 0))),
            scratch_shapes=_page_scratch(pages, D_I) + [
                pltpu.VMEM((n_steps, 8, n_main), jnp.int32), pltpu.VMEM((8, n_tail), jnp.int32)]),
        compiler_params=_params("arbitrary", "arbitrary"),
        name="dsa_select")(page_table, cache_idx_t, iq, iw, ik_new_t, tri)


def _dsa_sample_kernel(pt_ref, cache_hbm, q_ref, bias_ref, kv_new_ref, bias_new_ref, o_ref, buf, sem,
                       m_ref, l_ref, acc_ref, *, layer, pages):
    j = pl.program_id(1)
    rows = q_ref.shape[0]
    kw = KVH_B * HEAD_DIM
    slot = _stream_pages(pt_ref, cache_hbm, buf, sem, layer=layer, pages=pages)

    @pl.when(j == 0)
    def _():
        _init_softmax(m_ref, l_ref, acc_ref)

    def attend(k_t, v_t, bias):
        n = k_t.shape[1]
        s = jnp.dot(q_ref[...], k_t, preferred_element_type=F32)
        s = (s.reshape(rows // 8, 8, n) + bias[None]).reshape(rows, n)
        _softmax_step_t(s, m_ref, l_ref, acc_ref, v_t)

    attend(buf[slot, 0:kw, :].astype(BF16), buf[slot, kw:, :].astype(BF16), bias_ref[...])

    @pl.when(j == pl.num_programs(1) - 1)
    def _():
        kv_new = kv_new_ref[...]
        attend(kv_new[:kw], kv_new[kw:], bias_new_ref[...])
        o_ref[...] = acc_ref[...] / l_ref[...]


def _dsa_sample(layer, page_table, cache_kv_t, q, bias, kv_new_t, bias_new, *, pages):
    n_seq, n_pages = page_table.shape
    n_steps = n_pages // pages
    n_main = pages * PAGE_SIZE
    rows = q.shape[1]
    n_tail = kv_new_t.shape[2]
    kw = KVH_B * HEAD_DIM
    in_specs = [HBM_SPEC, _seq_spec((rows, kw)),
                pl.BlockSpec((None, None, 8, n_main), lambda b, j, pt: (b, j, 0, 0)),
                _seq_spec((2 * kw, n_tail)), _seq_spec((8, n_tail))]
    return pl.pallas_call(
        functools.partial(_dsa_sample_kernel, layer=layer, pages=pages),
        out_shape=jax.ShapeDtypeStruct((n_seq, rows, kw), F32),
        grid_spec=pltpu.PrefetchScalarGridSpec(
            num_scalar_prefetch=1, grid=(n_seq, n_steps), in_specs=in_specs,
            out_specs=pl.BlockSpec((None, rows, kw), lambda b, j, pt: (b, 0, 0)),
            scratch_shapes=_page_scratch(pages, 2 * kw) + [
                pltpu.VMEM((rows, 1), F32), pltpu.VMEM((rows, 1), F32), pltpu.VMEM((rows, kw), F32)]),
        compiler_params=_params("arbitrary", "arbitrary"),
        name="dsa_sample")(page_table, cache_kv_t, q, bias, kv_new_t, bias_new)


def _router_kernel(x_ref, g_ref, w_ref, b_ref, tri_ref, idx_ref, gate_ref, rank_ref, count_ref, carry_ref):
    @pl.when(pl.program_id(0) == 0)
    def _():
        carry_ref[...] = jnp.zeros(carry_ref.shape, F32)

    x = x_ref[...]
    ms = jnp.mean(x * x, axis=-1, keepdims=True)
    h = x * lax.rsqrt(ms + NORM_EPS) * g_ref[...]
    logit = jnp.dot(h, w_ref[...], preferred_element_type=F32, precision=lax.Precision.HIGHEST) + b_ref[...]
    tm, n_e = logit.shape
    lane = lax.broadcasted_iota(jnp.int32, (tm, n_e), 1)
    picks, vals = [], []
    for _ in range(TOP_K):
        top = jnp.max(logit, axis=1, keepdims=True)
        first = jnp.min(jnp.where(logit == top, lane, n_e), axis=1, keepdims=True)
        picks.append(first)
        vals.append(top)
        logit = jnp.where(lane == first, -jnp.inf, logit)
    exps = [jnp.exp(v - vals[0]) for v in vals]
    denom = exps[0]
    for e in exps[1:]:
        denom = denom + e
    chosen = jnp.zeros((tm, n_e), F32)
    for first in picks:
        chosen = chosen + jnp.where(lane == first, 1.0, 0.0)
    before = jnp.dot(tri_ref[...], chosen.astype(BF16), preferred_element_type=F32) + carry_ref[...]
    for k in range(TOP_K):
        idx_ref[:, k:k + 1] = picks[k]
        gate_ref[:, k:k + 1] = exps[k] / denom
        rank_ref[:, k:k + 1] = jnp.sum(jnp.where(lane == picks[k], before, 0.0), axis=1, keepdims=True).astype(jnp.int32)
    carry_ref[...] = carry_ref[...] + jnp.sum(chosen, axis=0, keepdims=True)
    count_ref[...] = carry_ref[...].astype(jnp.int32)


def _router(x, gain, w_router, b_router, *, tm=512):
    n, d = x.shape
    e = w_router.shape[1]
    tri = jnp.asarray(np.arange(tm)[:, None] > np.arange(tm)[None, :], BF16)
    pick = lambda dt: jax.ShapeDtypeStruct((n, TOP_K), dt)
    pick_spec = pl.BlockSpec((tm, TOP_K), lambda i: (i, 0))
    return pl.pallas_call(
        _router_kernel,
        out_shape=(pick(jnp.int32), pick(F32), pick(jnp.int32), jax.ShapeDtypeStruct((1, e), jnp.int32)),
        grid=(n // tm,),
        in_specs=[pl.BlockSpec((tm, d), lambda i: (i, 0)), pl.BlockSpec((1, d), lambda i: (0, 0)),
                  pl.BlockSpec((d, e), lambda i: (0, 0)), pl.BlockSpec((1, e), lambda i: (0, 0)),
                  pl.BlockSpec((tm, tm), lambda i: (0, 0))],
        out_specs=(pick_spec, pick_spec, pick_spec, pl.BlockSpec((1, e), lambda i: (0, 0))),
        scratch_shapes=[pltpu.VMEM((1, e), F32)],
        compiler_params=_params("arbitrary"),
        name="router")(x, gain.reshape(1, d), w_router, b_router.reshape(1, e), tri)


def _expert_kernel(be_ref, bv_ref, x_ref, g_ref, w1_ref, b1_ref, w2_ref, b2_ref, o_ref):
    i = pl.program_id(0)

    @pl.when(bv_ref[i] != 0)
    def _():
        d_ff = w2_ref.shape[0]
        x = x_ref[...]
        ms = jnp.mean(x * x, axis=-1, keepdims=True)
        h = (x * lax.rsqrt(ms + NORM_EPS) * g_ref[...]).astype(BF16)
        u = jnp.dot(h, w1_ref[...].astype(BF16), preferred_element_type=F32) + b1_ref[...]
        g = jnp.minimum(u[:, :d_ff], SWIGLU_LIMIT)
        up = jnp.clip(u[:, d_ff:], -SWIGLU_LIMIT, SWIGLU_LIMIT)
        a = g * jax.nn.sigmoid(SWIGLU_ALPHA * g) * (up + 1.0)
        o_ref[...] = jnp.dot(a.astype(BF16), w2_ref[...].astype(BF16), preferred_element_type=F32) + b2_ref[...]

    @pl.when(bv_ref[i] == 0)
    def _():
        o_ref[...] = jnp.zeros(o_ref.shape, o_ref.dtype)


def _experts(block_e, block_valid, xb, gain, w1, b1, w2, b2, *, tm):
    n_rows, d = xb.shape
    e, _, two_ff = w1.shape
    d_ff = w2.shape[1]
    return pl.pallas_call(
        _expert_kernel,
        out_shape=jax.ShapeDtypeStruct((n_rows, d), F32),
        grid_spec=pltpu.PrefetchScalarGridSpec(
            num_scalar_prefetch=2, grid=(n_rows // tm,),
            in_specs=[pl.BlockSpec((tm, d), lambda i, be, bv: (i, 0)),
                      pl.BlockSpec((1, d), lambda i, be, bv: (0, 0)),
                      pl.BlockSpec((None, d, two_ff), lambda i, be, bv: (be[i], 0, 0)),
                      pl.BlockSpec((None, 1, two_ff), lambda i, be, bv: (be[i], 0, 0)),
                      pl.BlockSpec((None, d_ff, d), lambda i, be, bv: (be[i], 0, 0)),
                      pl.BlockSpec((None, 1, d), lambda i, be, bv: (be[i], 0, 0))],
            out_specs=pl.BlockSpec((tm, d), lambda i, be, bv: (i, 0))),
        compiler_params=_params("arbitrary"),
        name="experts")(block_e, block_valid, xb, gain.reshape(1, d), w1, b1.reshape(e, 1, two_ff), w2,
                        b2.reshape(e, 1, d))


def _moe(x, gain, w_router, b_router, w1, b1, w2, b2, *, tm_e=256):
    n, d = x.shape
    top_idx, gate, rank, counts = _router(x, gain, w_router, b_router)
    flat_e = top_idx.reshape(-1)
    nk = flat_e.shape[0]
    counts = counts[0]
    start = jnp.cumsum(counts) - counts
    padded = _cdiv(counts, tm_e) * tm_e
    pad_end = jnp.cumsum(padded)
    pad_start = pad_end - padded
    dest = pad_start[flat_e] + rank.reshape(-1)
    order = jnp.argsort(dest).astype(jnp.int32)
    n_blocks = _cdiv(nk, tm_e) + N_EXPERTS
    block_start = jnp.arange(n_blocks, dtype=jnp.int32) * tm_e
    block_e = jnp.minimum(jnp.sum((block_start[:, None] >= pad_end[None, :]).astype(jnp.int32), axis=1), N_EXPERTS - 1)
    block_valid = (block_start < pad_end[-1]).astype(jnp.int32)
    row = jnp.arange(n_blocks * tm_e, dtype=jnp.int32)
    row_e = jnp.repeat(block_e, tm_e)
    offset = row - pad_start[row_e]
    src = jnp.clip(start[row_e] + offset, 0, nk - 1)
    row_tok = jnp.where(offset < counts[row_e], order[src] // TOP_K, 0)
    yb = _experts(block_e, block_valid, x[row_tok], gain, w1, b1, w2, b2, tm=tm_e)
    dest_k = dest.reshape(n, TOP_K)
    y = x
    for k in range(TOP_K):
        y = y + yb[dest_k[:, k]] * gate[:, k:k + 1]
    return y


def _rope_tables(pos, n_rot):
    half = n_rot // 2
    inv_freq = ROPE_THETA ** (-jnp.arange(half, dtype=F32) / half)
    ang = pos.astype(F32)[:, None] * inv_freq[None, :]
    return jnp.cos(ang), jnp.sin(ang)


def _regroup_w_in(w_in):
    cuts = np.cumsum([0, Q_LORA, KV_LORA, ROPE_A, H_B * HEAD_DIM, KVH_B * HEAD_DIM, KVH_B * HEAD_DIM,
                      H_I * D_I, D_I, H_I, 2 * w_in.shape[0]])
    w_cq, w_ckv, w_kpe, w_qb, w_kb, w_vb, w_iq, w_ik, w_iw, w_g = [w_in[:, a:b] for a, b in zip(cuts[:-1], cuts[1:])]
    misc_pad = jnp.zeros((w_in.shape[0], COL_GATE - COL_MISC - ROPE_A - D_I - H_I), w_in.dtype)
    return jnp.concatenate([w_cq, w_ckv, w_qb, w_kb, w_vb, w_iq, w_kpe, w_ik, w_iw, misc_pad, w_g], axis=1).astype(BF16)


def _layer(l, xp, xs, page_table, caches, wts, *, t_real):
    (cache_mla, cache_dsa_kv, cache_dsa_idx) = caches
    n_b, t_pad, d = xp.shape
    n_seq, n_ds, _ = xs.shape
    past = page_table.shape[1] * PAGE_SIZE
    n_p = n_b * t_pad
    n_s = n_seq * n_ds
    n_sel_p = min(MAX_SEL, (t_real - N_META) // 4)
    n_sel_s = min(MAX_SEL, (past + n_ds) // 4)

    x_all = jnp.concatenate([xp.reshape(n_p, d), xs.reshape(n_s, d)], axis=0)
    n_all = _round_up(n_p + n_s, 1024)
    x_all = jnp.pad(x_all, ((0, n_all - n_p - n_s), (0, 0)))
    pos = jnp.concatenate([jnp.tile(jnp.arange(t_pad), n_b), jnp.tile(past + jnp.arange(n_ds), n_seq),
                           jnp.zeros((n_all - n_p - n_s,), jnp.int32)])
    z = _matmul(x_all, _regroup_w_in(wts["w_in"][l]), gain=wts["g_attn_norm"][l], tm=1024, tn=1024, name="proj_in")

    cos_a, sin_a = _rope_tables(pos, ROPE_A)
    cos_b, sin_b = _rope_tables(pos, HEAD_DIM)
    cos_i, sin_i = _rope_tables(pos, D_I_ROPE)
    ones = lambda w: jnp.ones((n_all, w), F32)
    zeros = lambda w: jnp.zeros((n_all, w), F32)
    cos_a_t = jnp.concatenate([ones(NOPE_A), cos_a, cos_a, zeros(LANE - DQK_A)], axis=1)
    sin_a_t = jnp.concatenate([zeros(NOPE_A), -sin_a, sin_a, zeros(LANE - DQK_A)], axis=1)
    cos_b_t = jnp.tile(jnp.concatenate([cos_b, cos_b], axis=1), (1, LANE // HEAD_DIM))
    sin_b_t = jnp.tile(jnp.concatenate([-sin_b, sin_b], axis=1), (1, LANE // HEAD_DIM))
    cos_i_t = jnp.tile(jnp.concatenate([ones(D_I - D_I_ROPE), cos_i, cos_i], axis=1), (1, LANE // D_I))
    sin_i_t = jnp.tile(jnp.concatenate([zeros(D_I - D_I_ROPE), -sin_i, sin_i], axis=1), (1, LANE // D_I))
    pad_gain = lambda g: jnp.pad(g, (0, LANE - DQK_A)).reshape(1, LANE)
    tm = 512

    w_uq = jnp.pad(wts["w_uq"][l], ((0, 0), (0, 0), (0, LANE - DQK_A))).reshape(Q_LORA, H_A * LANE).astype(BF16)
    (q_a,) = _row_call(_q_up_kernel, n_all, tm,
                       [(z, Q_LORA, COL_CQ // Q_LORA), (wts["g_cq"][l].reshape(1, Q_LORA), None, 0), (w_uq, None, 0),
                        (pad_gain(wts["g_qa"][l] * (DQK_A ** -0.5)), None, 0), (cos_a_t, LANE, 0), (sin_a_t, LANE, 0)],
                       [(H_A * LANE, BF16)], "q_up")
    sl = slice(n_p, n_p + n_s)
    tm_s = min(tm, n_s)
    (q_a_s,) = _row_call(_q_up_kernel, n_s, tm_s,
                         [(z[sl], Q_LORA, COL_CQ // Q_LORA), (wts["g_cq"][l].reshape(1, Q_LORA), None, 0), (w_uq, None, 0),
                          (pad_gain(wts["g_qa"][l] * (DQK_A ** -0.5)), None, 0), (cos_a_t[sl], LANE, 0),
                          (sin_a_t[sl], LANE, 0)],
                         [(H_A * LANE, F32)], "q_up_sample")

    w_uk = jnp.pad(wts["w_uk"][l], ((0, 0), (0, 0), (0, LANE - NOPE_A))).reshape(KV_LORA, H_A * LANE)
    w_kv = jnp.concatenate([w_uk, wts["w_uv"][l].reshape(KV_LORA, H_A * V_A)], axis=1).astype(BF16)
    place = np.zeros((LANE, H_A * LANE), np.float32)
    for h in range(H_A):
        place[np.arange(ROPE_A), h * LANE + NOPE_A + np.arange(ROPE_A)] = 1.0
    k_a, v_a = _row_call(_kv_up_kernel, n_all, tm,
                         [(z, KV_LORA, COL_CKV // KV_LORA), (wts["g_ckv"][l].reshape(1, KV_LORA), None, 0),
                          (z, LANE, COL_MISC // LANE), (w_kv, None, 0), (jnp.asarray(place, BF16), None, 0),
                          (pad_gain(wts["g_ka"][l]), None, 0), (cos_a_t, LANE, 0), (sin_a_t, LANE, 0)],
                         [(H_A * LANE, BF16), (H_A * V_A, BF16)], "kv_up")

    expand = np.zeros((H_B * HEAD_DIM, H_B * KVH_B * HEAD_DIM), np.float32)
    for h in range(H_B):
        cols = h * KVH_B * HEAD_DIM + (h // GROUP_B) * HEAD_DIM + np.arange(HEAD_DIM)
        expand[h * HEAD_DIM + np.arange(HEAD_DIM), cols] = 1.0
    two = lambda g: jnp.tile(g, LANE // HEAD_DIM).reshape(1, LANE)
    g_ik_t = jnp.pad(wts["g_ik"][l], (ROPE_A, LANE - ROPE_A - D_I)).reshape(1, LANE)
    mla_rows, kv_rows, i_k, i_w, q_b_pad, k_b2, v_b2, i_q, i_k_bf = _row_call(
        _post_z_kernel, n_all, tm,
        [(z, COL_GATE, 0), (wts["g_ckv"][l].reshape(1, KV_LORA), None, 0),
         (two(wts["g_qb"][l] * (HEAD_DIM ** -0.5)), None, 0), (two(wts["g_kb"][l]), None, 0), (g_ik_t, None, 0),
         (cos_b_t, LANE, 0), (sin_b_t, LANE, 0), (cos_i_t, LANE, 0), (sin_i_t, LANE, 0),
         (jnp.asarray(expand, BF16), None, 0)],
        [(KV_LORA + ROPE_A, F32), (2 * KVH_B * HEAD_DIM, F32), (D_I, F32), (H_I, F32), (H_B * LANE, BF16),
         (KVH_B * HEAD_DIM, BF16), (KVH_B * HEAD_DIM, BF16), (H_I * D_I, BF16), (D_I, BF16)], "post_z")
    q_b_pad = q_b_pad.reshape(n_all, H_B, KVH_B * HEAD_DIM)
    i_q = i_q.reshape(n_all, H_I, D_I)

    tq = 128
    nq = t_pad // tq
    n_kc = max(dd for dd in range(1, 5) if nq % dd == 0)
    tk = n_kc * tq
    o_a_p = _mla_prompt(q_a[:n_p].reshape(n_b, t_pad, H_A * LANE), k_a[:n_p].reshape(n_b, t_pad, H_A * LANE),
                        v_a[:n_p].reshape(n_b, t_pad, H_A * V_A), tq=tk)

    blocked = lambda a: a.reshape((n_b, nq, tq) + a.shape[1:]).swapaxes(2, 3)
    kw = KVH_B * HEAD_DIM
    qt_p = q_b_pad[:n_p].reshape(n_b, nq, tq, H_B, kw).transpose(0, 1, 4, 3, 2).reshape(n_b, nq, kw, H_B * tq)
    vt_p = v_b2[:n_p].reshape(n_b, t_pad // tk, tk, kw).swapaxes(2, 3)
    o_b_p = _dsa_prompt(blocked(i_q[:n_p]), i_w[:n_p].reshape(n_b, t_pad, H_I), qt_p,
                        i_k_bf[:n_p].reshape(n_b, t_pad, D_I), k_b2[:n_p].reshape(n_b, t_pad, kw), vt_p,
                        tq=tq, tk=tk, n_sel=n_sel_p)
    o_b_p = o_b_p.reshape(n_b, nq, KVH_B, HEAD_DIM, KVH_B, GROUP_B, tq)
    o_b_p = jnp.stack([o_b_p[:, :, g, :, g] for g in range(KVH_B)], axis=2)
    o_b_p = o_b_p.transpose(0, 1, 5, 2, 4, 3).reshape(n_p, H_B * HEAD_DIM)

    pages = max(dd for dd in range(1, MAX_PAGES_PER_STEP + 1) if page_table.shape[1] % dd == 0)
    assert 8 % n_ds == 0
    n_tail = LANE
    tail_t = lambda a: jnp.pad(a.reshape(n_seq, n_ds, a.shape[-1]), ((0, 0), (0, n_tail - n_ds), (0, 0))).swapaxes(1, 2)
    cache_mla_t = jnp.swapaxes(cache_mla, 2, 3)
    cache_idx_t = jnp.swapaxes(cache_dsa_idx, 2, 3)
    cache_kv_t = jnp.moveaxis(cache_dsa_kv, 2, -1).reshape(cache_dsa_kv.shape[:2] + (2 * KVH_B * HEAD_DIM, PAGE_SIZE))
    g_ka = wts["g_ka"][l]
    pos_k = jnp.arange(past + n_tail)
    cos_k, sin_k = _rope_tables(pos_k, ROPE_A)
    g1, g2 = g_ka[NOPE_A:NOPE_A + ROPE_A // 2], g_ka[NOPE_A + ROPE_A // 2:]
    cg_t = jnp.concatenate([cos_k * g1, cos_k * g2], axis=1).T
    sg_t = jnp.concatenate([sin_k * g1, -sin_k * g2], axis=1).T
    qa_s = q_a_s.reshape(n_s, H_A, LANE)
    rows_a = n_ds * H_A
    eye_h = jnp.eye(H_A, dtype=F32)
    qn_s = qa_s[..., :NOPE_A] * g_ka[:NOPE_A]
    qbd = (qn_s[:, :, None, :] * eye_h[None, :, :, None]).reshape(n_seq, rows_a, H_A * NOPE_A)
    q_r = qa_s[:, :, NOPE_A:DQK_A].reshape(n_seq, rows_a, ROPE_A)
    q_r_swap = jnp.concatenate([q_r[..., ROPE_A // 2:], q_r[..., :ROPE_A // 2]], axis=-1)
    wt = wts["w_uk"][l].reshape(KV_LORA, H_A * NOPE_A).T
    o_lat = _mla_sample(l, page_table, cache_mla_t, wt.astype(BF16), qbd.astype(BF16), q_r.astype(BF16),
                        q_r_swap.astype(BF16), cg_t[:, :past], sg_t[:, :past], tail_t(mla_rows[sl]),
                        cg_t[:, past:], sg_t[:, past:], pages=pages, n_new=n_ds)
    o_a_s = jnp.einsum("bqhr,rhv->bqhv", o_lat.reshape(n_seq, n_ds, H_A, KV_LORA), wts["w_uv"][l])
    o_a_s = o_a_s.reshape(n_s, H_A * V_A)

    slot = jnp.arange(8) % n_ds
    iq_s = i_q[sl].reshape(n_seq, n_ds, H_I, D_I)[:, slot].swapaxes(1, 2).reshape(n_seq, H_I * 8, D_I)
    iw_s = i_w[sl].reshape(n_seq, n_ds, H_I)[:, slot].swapaxes(1, 2).reshape(n_seq, H_I * 8, 1)
    bias, bias_new = _dsa_select(l, page_table, cache_idx_t, iq_s, iw_s, tail_t(i_k_bf[sl]), pages=pages,
                                 n_new=n_ds, n_sel=n_sel_s)
    qb_s = q_b_pad[sl].reshape(n_seq, n_ds, H_B, KVH_B * HEAD_DIM).swapaxes(1, 2).reshape(n_seq, H_B * n_ds, -1)
    kv_new_t = tail_t(kv_rows[sl]).astype(BF16)
    o_b_s = _dsa_sample(l, page_table, cache_kv_t, qb_s, bias, kv_new_t, bias_new, pages=pages)
    o_b_s = o_b_s.reshape(n_seq, KVH_B, GROUP_B, n_ds, KVH_B, HEAD_DIM)
    o_b_s = jnp.stack([o_b_s[:, g, :, :, g, :] for g in range(KVH_B)], axis=1)
    o_b_s = o_b_s.reshape(n_seq, H_B, n_ds, HEAD_DIM).swapaxes(1, 2).reshape(n_s, H_B * HEAD_DIM)

    zeros_tail = jnp.zeros((n_all - n_p - n_s, H_A * V_A), BF16)
    o_a = jnp.concatenate([o_a_p.reshape(n_p, H_A * V_A), o_a_s.astype(BF16), zeros_tail], axis=0)
    o_b = jnp.concatenate([o_b_p, o_b_s.astype(BF16), zeros_tail], axis=0)
    assert COL_GATE % (2 * d) == 0
    x_all = _merge(x_all, o_a, o_b, z, wts["w_pa"][l].astype(BF16), wts["w_pb"][l].astype(BF16),
                   wts["w_o"][l].astype(BF16))
    x_all = _moe(x_all, wts["g_ffn_norm"][l], wts["w_router"][l], wts["b_router"][l], wts["w_mlp1"][l],
                 wts["b_mlp1"][l], wts["w_mlp2"][l], wts["b_mlp2"][l])

    new_p = (mla_rows[:n_p].reshape(n_b, t_pad, -1)[:, :t_real],
             kv_rows[:n_p].reshape(n_b, t_pad, 2, KVH_B, HEAD_DIM)[:, :t_real],
             i_k[:n_p].reshape(n_b, t_pad, D_I)[:, :t_real])
    new_s = (mla_rows[sl].reshape(n_seq, n_ds, -1), kv_rows[sl].reshape(n_seq, n_ds, 2, KVH_B, HEAD_DIM),
             i_k[sl].reshape(n_seq, n_ds, D_I))
    return x_all[:n_p].reshape(n_b, t_pad, d), x_all[sl].reshape(n_seq, n_ds, d), new_p, new_s


def kernel(x_prompt, x_sample, cache_mla, cache_dsa_kv, cache_dsa_idx, page_table, meta_tokens, g_attn_norm, w_in,
           g_cq, w_uq, g_qa, g_ckv, w_uk, w_uv, g_ka, g_qb, g_kb, g_ik, w_pa, w_pb, w_o, g_ffn_norm, w_router,
           b_router, w_mlp1, b_mlp1, w_mlp2, b_mlp2):
    n_b, n_s, d = x_prompt.shape
    t_real = n_s + N_META
    t_pad = _round_up(t_real, LANE)
    wts = dict(g_attn_norm=g_attn_norm, w_in=w_in, g_cq=g_cq, w_uq=w_uq, g_qa=g_qa, g_ckv=g_ckv, w_uk=w_uk, w_uv=w_uv,
               g_ka=g_ka, g_qb=g_qb, g_kb=g_kb, g_ik=g_ik, w_pa=w_pa, w_pb=w_pb, w_o=w_o, g_ffn_norm=g_ffn_norm,
               w_router=w_router, b_router=b_router, w_mlp1=w_mlp1, b_mlp1=b_mlp1, w_mlp2=w_mlp2, b_mlp2=b_mlp2)
    meta = jnp.broadcast_to(meta_tokens[None].astype(x_prompt.dtype), (n_b, N_META, d))
    xp = jnp.concatenate([meta, x_prompt, jnp.zeros((n_b, t_pad - t_real, d), x_prompt.dtype)], axis=1)
    xs = x_sample
    outs_p, outs_s = [], []
    for l in range(w_in.shape[0]):
        xp, xs, new_p, new_s = _layer(l, xp, xs, page_table, (cache_mla, cache_dsa_kv, cache_dsa_idx), wts,
                                      t_real=t_real)
        outs_p.append(new_p)
        outs_s.append(new_s)
    stack = lambda outs, i: jnp.stack([o[i] for o in outs], axis=0)
    return (xp[:, N_META:t_real], xs, stack(outs_p, 0), stack(outs_p, 1), stack(outs_p, 2),
            stack(outs_s, 0), stack(outs_s, 1), stack(outs_s, 2))
```

```python
import functools

import jax
import jax.numpy as jnp
import numpy as np
from jax import lax
from jax.experimental import pallas as pl
from jax.experimental.pallas import tpu as pltpu

N_META = 16
HEAD_DIM = 64
H_A = 8
NOPE_A = 64
ROPE_A = 32
DQK_A = NOPE_A + ROPE_A
V_A = 64
Q_LORA = 512
KV_LORA = 256
H_B = 8
KVH_B = 2
GROUP_B = H_B // KVH_B
H_I = 8
D_I = 32
D_I_ROPE = 16
MAX_SEL = 256
N_EXPERTS = 32
TOP_K = 4
SWIGLU_LIMIT = 7.0
SWIGLU_ALPHA = 1.702
ROPE_THETA = 10000.0
NORM_EPS = 1e-6
PAGE_SIZE = 128

LANE = 128
MASK_VALUE = -1e30
INT_MIN = -(2 ** 31)
VMEM_LIMIT = 56 * 1024 * 1024
MAX_PAGES_PER_STEP = 16

COL_CQ = 0
COL_CKV = 512
COL_QB = 768
COL_KB = 1280
COL_VB = 1408
COL_IQ = 1536
COL_MISC = 1792
COL_GATE = 2048
N_PROJ = 4096

F32 = jnp.float32
BF16 = jnp.bfloat16
NT_DIMS = (((1,), (1,)), ((), ()))


def _cdiv(a, b):
    return (a + b - 1) // b


def _round_up(a, b):
    return _cdiv(a, b) * b


def _params(*sem):
    return pltpu.CompilerParams(dimension_semantics=sem, vmem_limit_bytes=VMEM_LIMIT)


def _mm_kernel(x_ref, w_ref, o_ref):
    o_ref[...] = jnp.dot(x_ref[...].astype(BF16), w_ref[...], preferred_element_type=F32).astype(o_ref.dtype)


def _mm_norm_kernel(x_ref, g_ref, w_ref, o_ref, xn_ref):
    @pl.when(pl.program_id(1) == 0)
    def _():
        x = x_ref[...].astype(F32)
        ms = jnp.mean(x * x, axis=-1, keepdims=True)
        xn_ref[...] = (x * lax.rsqrt(ms + NORM_EPS) * g_ref[...]).astype(BF16)

    o_ref[...] = jnp.dot(xn_ref[...], w_ref[...], preferred_element_type=F32).astype(o_ref.dtype)


def _matmul(x, w, *, gain=None, col_block=0, k=None, tm=512, tn=512, out_dtype=F32, name="mm"):
    m = x.shape[0]
    k = x.shape[1] if k is None else k
    n = w.shape[1]
    tn = min(tn, n)
    assert m % tm == 0 and n % tn == 0 and w.shape[0] == k
    grid = (m // tm, n // tn)
    x_spec = pl.BlockSpec((tm, k), lambda i, j: (i, col_block))
    w_spec = pl.BlockSpec((k, tn), lambda i, j: (0, j))
    o_spec = pl.BlockSpec((tm, tn), lambda i, j: (i, j))
    out_shape = jax.ShapeDtypeStruct((m, n), out_dtype)
    if gain is None:
        return pl.pallas_call(_mm_kernel, out_shape=out_shape, grid=grid, in_specs=[x_spec, w_spec],
                              out_specs=o_spec, compiler_params=_params("parallel", "arbitrary"), name=name)(x, w)
    g_spec = pl.BlockSpec((1, k), lambda i, j: (0, 0))
    return pl.pallas_call(_mm_norm_kernel, out_shape=out_shape, grid=grid, in_specs=[x_spec, g_spec, w_spec],
                          out_specs=o_spec, scratch_shapes=[pltpu.VMEM((tm, k), BF16)],
                          compiler_params=_params("parallel", "arbitrary"), name=name)(
                              x, gain.reshape(1, k).astype(F32), w)


def _lane_ids(tm):
    return lax.broadcasted_iota(jnp.int32, (tm, LANE), 1)


def _rotate_pairs(t, cos_t, sin_t, lane, group, split, half):
    first = (lane & (group - 1)) < split
    partner = jnp.where(first, pltpu.roll(t, LANE - half, 1), pltpu.roll(t, half, 1))
    return t * cos_t + partner * sin_t


def _head_tiles(x, fn):
    tiles = [fn(x[:, i * LANE:(i + 1) * LANE]) for i in range(x.shape[1] // LANE)]
    return tiles[0] if len(tiles) == 1 else jnp.concatenate(tiles, axis=1)


def _mla_head_post(t, gain, cos_t, sin_t, lane):
    ms = jnp.sum(t * t, axis=1, keepdims=True) * (1.0 / DQK_A)
    return _rotate_pairs(t * lax.rsqrt(ms + NORM_EPS) * gain, cos_t, sin_t, lane, LANE, NOPE_A + ROPE_A // 2,
                         ROPE_A // 2)


def _q_up_kernel(z_ref, gz_ref, w_ref, gq_ref, cos_ref, sin_ref, o_ref):
    x = z_ref[...]
    xn = (x * lax.rsqrt(jnp.mean(x * x, axis=-1, keepdims=True) + NORM_EPS) * gz_ref[...]).astype(BF16)
    q = jnp.dot(xn, w_ref[...], preferred_element_type=F32)
    lane = _lane_ids(x.shape[0])
    post = lambda t: _mla_head_post(t, gq_ref[...], cos_ref[...], sin_ref[...], lane)
    o_ref[...] = _head_tiles(q, post).astype(o_ref.dtype)


def _kv_up_kernel(z_ref, gz_ref, misc_ref, w_ref, place_ref, gk_ref, cos_ref, sin_ref, k_ref, v_ref):
    x = z_ref[...]
    c = (x * lax.rsqrt(jnp.mean(x * x, axis=-1, keepdims=True) + NORM_EPS) * gz_ref[...]).astype(BF16)
    kv = jnp.dot(c, w_ref[...], preferred_element_type=F32)
    n_k = H_A * LANE
    misc = misc_ref[...]
    hi = misc.astype(BF16)
    lo = (misc - hi.astype(F32)).astype(BF16)
    k = kv[:, :n_k] + (jnp.dot(hi, place_ref[...], preferred_element_type=F32)
                       + jnp.dot(lo, place_ref[...], preferred_element_type=F32))
    lane = _lane_ids(x.shape[0])
    post = lambda t: _mla_head_post(t, gk_ref[...], cos_ref[...], sin_ref[...], lane)
    k_ref[...] = _head_tiles(k, post).astype(k_ref.dtype)
    v_ref[...] = kv[:, n_k:].astype(v_ref.dtype)


def _row_call(kernel_fn, n, tm, ins, outs, name):
    def spec(width, col_block, shape):
        if width is None:
            return pl.BlockSpec(shape, lambda i: tuple(0 for _ in shape))
        return pl.BlockSpec((tm, width), lambda i: (i, col_block))
    in_specs = [spec(w, cb, a.shape) for a, w, cb in ins]
    out_specs = tuple(pl.BlockSpec((tm, w), lambda i: (i, 0)) for w, _ in outs)
    out_shape = tuple(jax.ShapeDtypeStruct((n, w), dt) for w, dt in outs)
    return pl.pallas_call(kernel_fn, out_shape=out_shape, grid=(n // tm,), in_specs=in_specs, out_specs=out_specs,
                          compiler_params=_params("parallel"), name=name)(*[a for a, _, _ in ins])


def _post_z_kernel(z_ref, gckv_ref, gqb_ref, gkb_ref, gik_ref, cb_ref, sb_ref, ci_ref, si_ref, expand_ref,
                   mla_ref, kv_ref, ik_ref, iw_ref, qb_ref, k2_ref, v2_ref, iq_ref, ikb_ref):
    z = z_ref[...]
    tm = z.shape[0]
    lane = _lane_ids(tm)
    cb, sb, ci, si = cb_ref[...], sb_ref[...], ci_ref[...], si_ref[...]
    misc = z[:, COL_MISC:COL_MISC + LANE]

    x = z[:, COL_CKV:COL_CKV + KV_LORA]
    mla_ref[:, 0:KV_LORA] = x * lax.rsqrt(jnp.mean(x * x, axis=-1, keepdims=True) + NORM_EPS) * gckv_ref[...]
    mla_ref[:, KV_LORA:] = misc[:, 0:ROPE_A]

    low = lane < HEAD_DIM

    def head_pair(t, gain):
        t2 = t * t
        ms = jnp.where(low, jnp.sum(jnp.where(low, t2, 0.0), axis=1, keepdims=True),
                       jnp.sum(jnp.where(low, 0.0, t2), axis=1, keepdims=True)) * (1.0 / HEAD_DIM)
        return _rotate_pairs(t * lax.rsqrt(ms + NORM_EPS) * gain, cb, sb, lane, HEAD_DIM, HEAD_DIM // 2, HEAD_DIM // 2)

    qb = _head_tiles(z[:, COL_QB:COL_QB + H_B * HEAD_DIM], lambda t: head_pair(t, gqb_ref[...])).astype(BF16)
    qb_ref[...] = jnp.dot(qb, expand_ref[...], preferred_element_type=F32).astype(qb_ref.dtype)

    kb = head_pair(z[:, COL_KB:COL_KB + LANE], gkb_ref[...])
    vb = z[:, COL_VB:COL_VB + LANE]
    kv_ref[:, 0:LANE] = kb
    kv_ref[:, LANE:] = vb
    k2_ref[...] = kb.astype(k2_ref.dtype)
    v2_ref[...] = vb.astype(v2_ref.dtype)

    rot_i = lambda t: _rotate_pairs(t, ci, si, lane, D_I, D_I - D_I_ROPE // 2, D_I_ROPE // 2)
    iq_ref[...] = _head_tiles(z[:, COL_IQ:COL_IQ + H_I * D_I], rot_i).astype(iq_ref.dtype)

    in_ik = (lane >= ROPE_A) & (lane < ROPE_A + D_I)
    ms = jnp.sum(jnp.where(in_ik, misc * misc, 0.0), axis=1, keepdims=True) * (1.0 / D_I)
    ik = pltpu.roll(rot_i(misc * lax.rsqrt(ms + NORM_EPS) * gik_ref[...]), LANE - ROPE_A, 1)
    ik_ref[...] = ik[:, 0:D_I]
    ikb_ref[...] = ik[:, 0:D_I].astype(ikb_ref.dtype)
    iw_ref[...] = pltpu.roll(misc, LANE - ROPE_A - D_I, 1)[:, 0:H_I] * ((H_I ** -0.5) * (D_I ** -0.5))


def _merge_kernel(x_ref, oa_ref, ob_ref, zg_ref, wpa_ref, wpb_ref, wo_ref, o_ref):
    d = x_ref.shape[1]
    y_a = jnp.dot(oa_ref[...], wpa_ref[...], preferred_element_type=F32)
    y_b = jnp.dot(ob_ref[...], wpb_ref[...], preferred_element_type=F32)
    zg = zg_ref[...]
    merged = jax.nn.sigmoid(zg[:, :d]) * y_a + jax.nn.sigmoid(zg[:, d:]) * y_b
    o_ref[...] = x_ref[...] + jnp.dot(merged.astype(BF16), wo_ref[...], preferred_element_type=F32)


def _merge(x, o_a, o_b, z, w_pa, w_pb, w_o, *, tm=512):
    n, d = x.shape
    ka, kb = o_a.shape[1], o_b.shape[1]
    const = lambda shape: pl.BlockSpec(shape, lambda i: (0, 0))
    return pl.pallas_call(
        _merge_kernel,
        out_shape=jax.ShapeDtypeStruct((n, d), F32),
        grid=(n // tm,),
        in_specs=[pl.BlockSpec((tm, d), lambda i: (i, 0)), pl.BlockSpec((tm, ka), lambda i: (i, 0)),
                  pl.BlockSpec((tm, kb), lambda i: (i, 0)), pl.BlockSpec((tm, 2 * d), lambda i: (i, COL_GATE // (2 * d))),
                  const((ka, d)), const((kb, d)), const((d, d))],
        out_specs=pl.BlockSpec((tm, d), lambda i: (i, 0)),
        compiler_params=_params("parallel"),
        name="merge")(x, o_a, o_b, z, w_pa, w_pb, w_o)


def _softmax_step(s, m_prev, l_prev, acc_prev, v_blk):
    m_new = jnp.maximum(m_prev, jnp.max(s, axis=1, keepdims=True))
    alpha = jnp.exp(m_prev - m_new)
    p = jnp.exp(s - m_new)
    l_new = alpha * l_prev + jnp.sum(p, axis=1, keepdims=True)
    acc_new = alpha * acc_prev + jnp.dot(p.astype(BF16), v_blk, preferred_element_type=F32)
    return m_new, l_new, acc_new


def _sortable_key(x):
    b = lax.bitcast_convert_type(x + 0.0, jnp.int32)
    return jnp.where(b < 0, b ^ jnp.int32(0x7FFFFFFF), b)


KEY_NEG_INF = INT_MIN + 0x7FFFFF


def _has_excess_ties(n_ge, thr, n_sel):
    return jnp.max(jnp.where(n_ge > n_sel, jnp.where(thr > KEY_NEG_INF, 1, 0), 0)) > 0


def _kth_largest_key(count_ge, n_sel, rows):
    def step(s, t):
        bit = lax.shift_left(jnp.int32(1), jnp.int32(31) - s)
        cand = t + bit
        return jnp.where(count_ge(cand) >= n_sel, cand, t)

    return lax.fori_loop(0, 32, step, jnp.full((rows, 1), INT_MIN, jnp.int32))


MLA_HEADS_PER_STEP = 4


def _mla_prompt_kernel(q_ref, k_ref, v_ref, o_ref, *, tq):
    i = pl.program_id(2)
    nh = MLA_HEADS_PER_STEP
    q = q_ref[...]
    causal = lax.broadcasted_iota(jnp.int32, (tq, tq), 1) <= lax.broadcasted_iota(jnp.int32, (tq, tq), 0)

    def chunk(c, carry, diagonal):
        m, l, acc = carry
        start = pl.multiple_of(c * tq, tq)
        k_blk = k_ref[pl.ds(start, tq), :]
        v_blk = v_ref[pl.ds(start, tq), :]
        s = jnp.concatenate([lax.dot_general(q[:, h * LANE:(h + 1) * LANE], k_blk[:, h * LANE:(h + 1) * LANE],
                                             NT_DIMS, preferred_element_type=F32) for h in range(nh)], axis=0)
        if diagonal:
            s = jnp.where(causal[None], s.reshape(nh, tq, tq), MASK_VALUE).reshape(nh * tq, tq)
        m_new = jnp.maximum(m, jnp.max(s, axis=1, keepdims=True))
        alpha = jnp.exp(m - m_new)
        p = jnp.exp(s - m_new)
        l_new = alpha * l + jnp.sum(p, axis=1, keepdims=True)
        p = p.astype(BF16)
        pv = jnp.concatenate([jnp.dot(p[h * tq:(h + 1) * tq], v_blk[:, (h // 2) * LANE:(h // 2 + 1) * LANE],
                                      preferred_element_type=F32) for h in range(nh)], axis=0)
        return m_new, l_new, alpha * acc + pv

    init = (jnp.full((nh * tq, 1), MASK_VALUE, F32), jnp.zeros((nh * tq, 1), F32), jnp.zeros((nh * tq, LANE), F32))
    carry = lax.fori_loop(0, i, functools.partial(chunk, diagonal=False), init)
    _, l, acc = chunk(i, carry, True)
    o = acc / l
    lane = lax.broadcasted_iota(jnp.int32, (tq, LANE), 1)
    pairs = [jnp.where(lane < V_A, o[(2 * g) * tq:(2 * g + 1) * tq], o[(2 * g + 1) * tq:(2 * g + 2) * tq])
             for g in range(nh // 2)]
    o_ref[...] = jnp.concatenate(pairs, axis=1).astype(o_ref.dtype)


def _mla_prompt(q, k, v, *, tq):
    b, t, _ = q.shape
    nh = MLA_HEADS_PER_STEP
    grid = (b, H_A // nh, t // tq)
    return pl.pallas_call(
        functools.partial(_mla_prompt_kernel, tq=tq),
        out_shape=jax.ShapeDtypeStruct((b, t, H_A * V_A), BF16),
        grid=grid,
        in_specs=[pl.BlockSpec((None, tq, nh * LANE), lambda bi, g, i: (bi, i, g)),
                  pl.BlockSpec((None, t, nh * LANE), lambda bi, g, i: (bi, 0, g)),
                  pl.BlockSpec((None, t, nh * V_A), lambda bi, g, i: (bi, 0, g))],
        out_specs=pl.BlockSpec((None, tq, nh * V_A), lambda bi, g, i: (bi, i, g)),
        compiler_params=_params("parallel", "parallel", "arbitrary"),
        name="mla_prompt")(q, k, v)


def _select_bias(key, thr, need, carry, valid, tri_ref):
    w = min(tri_ref.shape[0], key.shape[1])
    tri = tri_ref[0:w, 0:w]
    parts = []
    for t in range(key.shape[1] // w):
        k_t = key[:, t * w:(t + 1) * w]
        eq = k_t == thr
        rank = jnp.dot(jnp.where(eq, 1.0, 0.0).astype(BF16), tri, preferred_element_type=F32) + carry
        carry = jnp.max(rank, axis=1, keepdims=True)
        tie = jnp.where(rank <= need, 0.0, MASK_VALUE)
        parts.append(jnp.where(k_t > thr, 0.0, jnp.where(eq, tie, MASK_VALUE)))
    bias = parts[0] if len(parts) == 1 else jnp.concatenate(parts, axis=1)
    if valid is not None:
        bias = jnp.where(valid, bias, MASK_VALUE)
    return bias, carry


def _dsa_prompt_kernel(iq_ref, w_ref, q_ref, ik_ref, k_ref, v_ref, tri_ref, o_ref, key_ref, bias_ref, wb_ref,
                       *, tq, tk, n_sel):
    i = pl.program_id(1)
    n_c = ((i + 1) * tq + tk - 1) // tk
    q_pos = i * tq + lax.broadcasted_iota(jnp.int32, (tq, tk), 0)
    lane_pos = lax.broadcasted_iota(jnp.int32, (tq, tk), 1)

    w = w_ref[...]
    for h in range(H_I):
        wb_ref[h] = jnp.broadcast_to(w[:, h:h + 1], (tq, LANE))
    iq = iq_ref[...].reshape(H_I * tq, D_I)

    def score_chunk(c, _):
        start = pl.multiple_of(c * tk, tk)
        logit = lax.dot_general(iq, ik_ref[pl.ds(start, tk), :], NT_DIMS, preferred_element_type=F32)
        cols = []
        for t in range(tk // LANE):
            acc = jnp.zeros((tq, LANE), F32)
            for h in range(H_I):
                acc = acc + jnp.maximum(logit[h * tq:(h + 1) * tq, t * LANE:(t + 1) * LANE], 0.0) * wb_ref[h]
            cols.append(acc)
        score = jnp.concatenate(cols, axis=1)
        score = jnp.where(start + lane_pos <= q_pos, score, -jnp.inf)
        key_ref[c] = _sortable_key(score)
        return 0

    lax.fori_loop(0, n_c, score_chunk, 0)

    def count_ge(cand):
        cand_b = jnp.broadcast_to(cand, (tq, LANE))

        def body(c, cnt):
            key = key_ref[c]
            for t in range(tk // LANE):
                cnt = cnt + jnp.where(key[:, t * LANE:(t + 1) * LANE] >= cand_b, 1, 0)
            return cnt

        cnt = lax.fori_loop(0, n_c, body, jnp.zeros((tq, LANE), jnp.int32))
        return jnp.sum(cnt, axis=1, keepdims=True)

    thr = _kth_largest_key(count_ge, n_sel, tq)

    def bias_by_threshold():
        thr_eff = jnp.maximum(thr, KEY_NEG_INF + 1)

        def body(c, _):
            bias_ref[c] = jnp.where(key_ref[c] >= thr_eff, 0.0, MASK_VALUE)
            return 0

        lax.fori_loop(0, n_c, body, 0)

    def bias_with_ties():
        need = (n_sel - count_ge(thr + 1)).astype(F32)

        def body(c, carry):
            start = pl.multiple_of(c * tk, tk)
            bias, carry = _select_bias(key_ref[c], thr, need, carry, start + lane_pos <= q_pos, tri_ref)
            bias_ref[c] = bias
            return carry

        lax.fori_loop(0, n_c, body, jnp.zeros((tq, 1), F32))

    lax.cond(_has_excess_ties(count_ge(thr), thr, n_sel), bias_with_ties, bias_by_threshold)

    rows = H_B * tq
    qa = q_ref[...].reshape(rows, LANE)

    def attend(c, carry):
        start = pl.multiple_of(c * tk, tk)
        s = lax.dot_general(qa, k_ref[pl.ds(start, tk), :], NT_DIMS, preferred_element_type=F32)
        s = (s.reshape(H_B, tq, tk) + bias_ref[c][None]).reshape(rows, tk)
        return _softmax_step(s, *carry, v_ref[pl.ds(start, tk), :])

    m, l, acc = lax.fori_loop(0, n_c, attend, (jnp.full((rows, 1), MASK_VALUE, F32),
                                               jnp.zeros((rows, 1), F32), jnp.zeros((rows, LANE), F32)))
    o_ref[...] = (acc / l).reshape(H_B, tq, LANE).astype(o_ref.dtype)


def _dsa_prompt(iq, iw, q, ik, k, v, *, tq, tk, n_sel):
    b, nq = iq.shape[:2]
    t = ik.shape[1]
    n_chunks = t // tk
    tri_w = 2 * LANE if tk % (2 * LANE) == 0 else LANE
    tri = jnp.asarray(np.arange(tri_w)[:, None] <= np.arange(tri_w)[None, :], BF16)
    return pl.pallas_call(
        functools.partial(_dsa_prompt_kernel, tq=tq, tk=tk, n_sel=n_sel),
        out_shape=jax.ShapeDtypeStruct((b, nq, H_B, tq, LANE), BF16),
        grid=(b, nq),
        in_specs=[pl.BlockSpec((None, None, H_I, tq, D_I), lambda bi, i: (bi, i, 0, 0, 0)),
                  pl.BlockSpec((None, tq, H_I), lambda bi, i: (bi, i, 0)),
                  pl.BlockSpec((None, None, H_B, tq, LANE), lambda bi, i: (bi, i, 0, 0, 0)),
                  pl.BlockSpec((None, t, D_I), lambda bi, i: (bi, 0, 0)),
                  pl.BlockSpec((None, t, LANE), lambda bi, i: (bi, 0, 0)),
                  pl.BlockSpec((None, t, LANE), lambda bi, i: (bi, 0, 0)),
                  pl.BlockSpec((tri_w, tri_w), lambda bi, i: (0, 0))],
        out_specs=pl.BlockSpec((None, None, H_B, tq, LANE), lambda bi, i: (bi, i, 0, 0, 0)),
        scratch_shapes=[pltpu.VMEM((n_chunks, tq, tk), jnp.int32),
                        pltpu.VMEM((n_chunks, tq, tk), F32),
                        pltpu.VMEM((H_I, tq, LANE), F32)],
        compiler_params=_params("parallel", "arbitrary"),
        name="dsa_prompt")(iq, iw, q, ik, k, v, tri)


def _seq_spec(shape):
    return pl.BlockSpec((None,) + shape, lambda b, j, pt: (b,) + tuple(0 for _ in shape))


def _const_spec(shape):
    return pl.BlockSpec(shape, lambda b, j, pt: tuple(0 for _ in shape))


HBM_SPEC = pl.BlockSpec(memory_space=pl.ANY)


def _page_copy(cache_hbm, buf, sem, layer, page, slot, p):
    return pltpu.make_async_copy(cache_hbm.at[layer, page], buf.at[slot, :, pl.ds(p * PAGE_SIZE, PAGE_SIZE)],
                                 sem.at[slot])


def _stream_pages(pt_ref, cache_hbm, buf, sem, *, layer, pages):
    b = pl.program_id(0)
    j = pl.program_id(1)
    n_seq = pl.num_programs(0)
    n_steps = pl.num_programs(1)
    t = b * n_steps + j
    slot = lax.rem(t, 2)

    def start(seq, step, dst_slot):
        for p in range(pages):
            _page_copy(cache_hbm, buf, sem, layer, pt_ref[seq, step * pages + p], dst_slot, p).start(priority=p % 2)

    @pl.when(t == 0)
    def _():
        start(b, j, slot)

    @pl.when(t + 1 < n_seq * n_steps)
    def _():
        wrap = j + 1 == n_steps
        start(jnp.where(wrap, b + 1, b), jnp.where(wrap, 0, j + 1), 1 - slot)

    for p in range(pages):
        _page_copy(cache_hbm, buf, sem, layer, 0, slot, p).wait()
    return slot


def _page_scratch(pages, width):
    return [pltpu.VMEM((2, width, pages * PAGE_SIZE), F32), pltpu.SemaphoreType.DMA((2,))]


def _softmax_step_t(s, m_ref, l_ref, acc_ref, v_t):
    m_prev = m_ref[...]
    m_new = jnp.maximum(m_prev, jnp.max(s, axis=1, keepdims=True))
    alpha = jnp.exp(m_prev - m_new)
    p = jnp.exp(s - m_new)
    l_ref[...] = alpha * l_ref[...] + jnp.sum(p, axis=1, keepdims=True)
    acc_ref[...] = alpha * acc_ref[...] + lax.dot_general(p.astype(BF16), v_t, NT_DIMS, preferred_element_type=F32)
    m_ref[...] = m_new


def _init_softmax(m_ref, l_ref, acc_ref):
    m_ref[...] = jnp.full(m_ref.shape, MASK_VALUE, F32)
    l_ref[...] = jnp.zeros(l_ref.shape, F32)
    acc_ref[...] = jnp.zeros(acc_ref.shape, F32)


def _mla_sample_kernel(pt_ref, cache_hbm, wt_ref, qbd_ref, qr_ref, qs_ref, cg_ref, sg_ref, new_ref, cg_new_ref,
                       sg_new_ref, o_ref, buf, sem, lhs_ref, m_ref, l_ref, acc_ref, *, layer, pages, n_new):
    j = pl.program_id(1)
    rows = o_ref.shape[0]
    n_k = H_A * NOPE_A
    slot = _stream_pages(pt_ref, cache_hbm, buf, sem, layer=layer, pages=pages)

    @pl.when(j == 0)
    def _():
        wt = wt_ref[...]
        lhs_ref[0:n_k, :] = wt
        lhs_ref[n_k:, :] = jnp.dot(qbd_ref[...], wt, preferred_element_type=F32).astype(BF16)
        _init_softmax(m_ref, l_ref, acc_ref)

    def attend(c_t, pe_t, cg, sg, mask):
        n = c_t.shape[1]
        big = jnp.dot(lhs_ref[...], c_t, preferred_element_type=F32)
        ssq = jnp.zeros((H_A, n), F32)
        row_id = lax.broadcasted_iota(jnp.int32, (H_A, n), 0)
        for h in range(H_A):
            kh = big[h * NOPE_A:(h + 1) * NOPE_A]
            ssq = jnp.where(row_id == h, jnp.sum(kh * kh, axis=0, keepdims=True), ssq)
        ssq = ssq + jnp.sum(pe_t * pe_t, axis=0, keepdims=True)
        inv_rms = lax.rsqrt(ssq * (1.0 / DQK_A) + NORM_EPS)
        s_rope = (jnp.dot(qr_ref[...], (pe_t * cg).astype(BF16), preferred_element_type=F32)
                  + jnp.dot(qs_ref[...], (pe_t * sg).astype(BF16), preferred_element_type=F32))
        s = ((big[n_k:] + s_rope).reshape(rows // H_A, H_A, n) * inv_rms[None]).reshape(rows, n)
        if mask is not None:
            s = jnp.where(mask, s, MASK_VALUE)
        _softmax_step_t(s, m_ref, l_ref, acc_ref, c_t)

    attend(buf[slot, 0:KV_LORA, :].astype(BF16), buf[slot, KV_LORA:, :], cg_ref[...], sg_ref[...], None)

    @pl.when(j == pl.num_programs(1) - 1)
    def _():
        new = new_ref[...]
        n = new.shape[1]
        key_i = lax.broadcasted_iota(jnp.int32, (rows, n), 1)
        q_i = lax.broadcasted_iota(jnp.int32, (rows, n), 0) // H_A
        attend(new[:KV_LORA].astype(BF16), new[KV_LORA:], cg_new_ref[...], sg_new_ref[...],
               (key_i <= q_i) & (key_i < n_new))
        o_ref[...] = acc_ref[...] / l_ref[...]


def _mla_sample(layer, page_table, cache_t, wt, qbd, qr, qs, cg_t, sg_t, new_t, cg_new_t, sg_new_t, *, pages, n_new):
    n_seq, n_pages = page_table.shape
    rows = qbd.shape[1]
    n_main = pages * PAGE_SIZE
    n_tail = new_t.shape[2]
    width = KV_LORA + ROPE_A
    in_specs = [HBM_SPEC,
                _const_spec((H_A * NOPE_A, KV_LORA)),
                _seq_spec((rows, H_A * NOPE_A)), _seq_spec((rows, ROPE_A)), _seq_spec((rows, ROPE_A)),
                pl.BlockSpec((ROPE_A, n_main), lambda b, j, pt: (0, j)),
                pl.BlockSpec((ROPE_A, n_main), lambda b, j, pt: (0, j)),
                _seq_spec((width, n_tail)), _const_spec((ROPE_A, n_tail)), _const_spec((ROPE_A, n_tail))]
    return pl.pallas_call(
        functools.partial(_mla_sample_kernel, layer=layer, pages=pages, n_new=n_new),
        out_shape=jax.ShapeDtypeStruct((n_seq, rows, KV_LORA), F32),
        grid_spec=pltpu.PrefetchScalarGridSpec(
            num_scalar_prefetch=1, grid=(n_seq, n_pages // pages), in_specs=in_specs,
            out_specs=pl.BlockSpec((None, rows, KV_LORA), lambda b, j, pt: (b, 0, 0)),
            scratch_shapes=_page_scratch(pages, width) + [
                pltpu.VMEM((H_A * NOPE_A + rows, KV_LORA), BF16),
                pltpu.VMEM((rows, 1), F32), pltpu.VMEM((rows, 1), F32), pltpu.VMEM((rows, KV_LORA), F32)]),
        compiler_params=_params("arbitrary", "arbitrary"),
        name="mla_sample")(page_table, cache_t, wt, qbd, qr, qs, cg_t, sg_t, new_t, cg_new_t, sg_new_t)


def _dsa_select_kernel(pt_ref, cache_hbm, iq_ref, w_ref, ik_new_ref, tri_ref, bias_ref, bias_new_ref, buf, sem,
                       key_ref, key_new_ref, *, layer, pages, n_new, n_sel):
    j = pl.program_id(1)
    n_steps = key_ref.shape[0]
    n_main = key_ref.shape[2]
    n_tail = ik_new_ref.shape[1]
    nq = 8
    slot = _stream_pages(pt_ref, cache_hbm, buf, sem, layer=layer, pages=pages)

    def scores(ik_t):
        logit = jnp.dot(iq_ref[...], ik_t, preferred_element_type=F32)
        weighted = jnp.maximum(logit, 0.0) * w_ref[...]
        return jnp.sum(weighted.reshape(H_I, nq, ik_t.shape[1]), axis=0)

    key_ref[j] = _sortable_key(scores(buf[slot].astype(BF16)))

    @pl.when(j == n_steps - 1)
    def _():
        key_i = lax.broadcasted_iota(jnp.int32, (nq, n_tail), 1)
        q_i = lax.broadcasted_iota(jnp.int32, (nq, n_tail), 0) % n_new
        valid_new = (key_i <= q_i) & (key_i < n_new)
        key_new_ref[...] = _sortable_key(jnp.where(valid_new, scores(ik_new_ref[...]), -jnp.inf))

        def count_ge(cand):
            cand_b = jnp.broadcast_to(cand, (nq, LANE))
            cnts = [jnp.zeros((nq, LANE), jnp.int32) for _ in range(4)]
            tiles = [key_ref[c, :, t * LANE:(t + 1) * LANE] for c in range(n_steps) for t in range(n_main // LANE)]
            tiles += [key_new_ref[:, t * LANE:(t + 1) * LANE] for t in range(n_tail // LANE)]
            for idx, tile in enumerate(tiles):
                cnts[idx % 4] = cnts[idx % 4] + jnp.where(tile >= cand_b, 1, 0)
            return jnp.sum((cnts[0] + cnts[1]) + (cnts[2] + cnts[3]), axis=1, keepdims=True)

        thr = _kth_largest_key(count_ge, n_sel, nq)

        def bias_by_threshold():
            thr_eff = jnp.maximum(thr, KEY_NEG_INF + 1)

            def body(c, _):
                bias_ref[c] = jnp.where(key_ref[c] >= thr_eff, 0.0, MASK_VALUE)
                return 0

            lax.fori_loop(0, n_steps, body, 0)
            bias_new_ref[...] = jnp.where(key_new_ref[...] >= thr_eff, 0.0, MASK_VALUE)

        def bias_with_ties():
            need = (n_sel - count_ge(thr + 1)).astype(F32)

            def body(c, carry):
                bias, carry = _select_bias(key_ref[c], thr, need, carry, None, tri_ref)
                bias_ref[c] = bias
                return carry

            carry = lax.fori_loop(0, n_steps, body, jnp.zeros((nq, 1), F32))
            bias_new, _ = _select_bias(key_new_ref[...], thr, need, carry, valid_new, tri_ref)
            bias_new_ref[...] = bias_new

        lax.cond(_has_excess_ties(count_ge(thr), thr, n_sel), bias_with_ties, bias_by_threshold)


def _dsa_select(layer, page_table, cache_idx_t, iq, iw, ik_new_t, *, pages, n_new, n_sel):
    n_seq, n_pages = page_table.shape
    n_steps = n_pages // pages
    n_main = pages * PAGE_SIZE
    n_tail = ik_new_t.shape[2]
    rows = iq.shape[1]
    tri = jnp.asarray(np.arange(2 * LANE)[:, None] <= np.arange(2 * LANE)[None, :], BF16)
    in_specs = [HBM_SPEC, _seq_spec((rows, D_I)), _seq_spec((rows, 1)), _seq_spec((D_I, n_tail)),
                _const_spec((2 * LANE, 2 * LANE))]
    return pl.pallas_call(
        functools.partial(_dsa_select_kernel, layer=layer, pages=pages, n_new=n_new, n_sel=n_sel),
        out_shape=(jax.ShapeDtypeStruct((n_seq, n_steps, 8, n_main), F32),
                   jax.ShapeDtypeStruct((n_seq, 8, n_tail), F32)),
        grid_spec=pltpu.PrefetchScalarGridSpec(
            num_scalar_prefetch=1, grid=(n_seq, n_steps), in_specs=in_specs,
            out_specs=(pl.BlockSpec((None, n_steps, 8, n_main), lambda b, j, pt: (b, 0, 0, 0)),
                       pl.BlockSpec((None, 8, n_tail), lambda b, j, pt: (b, 0, 0))),
            scratch_shapes=_page_scratch(pages, D_I) + [
                pltpu.VMEM((n_steps, 8, n_main), jnp.int32), pltpu.VMEM((8, n_tail), jnp.int32)]),
        compiler_params=_params("arbitrary", "arbitrary"),
        name="dsa_select")(page_table, cache_idx_t, iq, iw, ik_new_t, tri)


def _dsa_sample_kernel(pt_ref, cache_hbm, q_ref, bias_ref, kv_new_ref, bias_new_ref, o_ref, buf, sem,
                       m_ref, l_ref, acc_ref, *, layer, pages):
    j = pl.program_id(1)
    rows = q_ref.shape[0]
    kw = KVH_B * HEAD_DIM
    slot = _stream_pages(pt_ref, cache_hbm, buf, sem, layer=layer, pages=pages)

    @pl.when(j == 0)
    def _():
        _init_softmax(m_ref, l_ref, acc_ref)

    def attend(k_t, v_t, bias):
        n = k_t.shape[1]
        s = jnp.dot(q_ref[...], k_t, preferred_element_type=F32)
        s = (s.reshape(rows // 8, 8, n) + bias[None]).reshape(rows, n)
        _softmax_step_t(s, m_ref, l_ref, acc_ref, v_t)

    attend(buf[slot, 0:kw, :].astype(BF16), buf[slot, kw:, :].astype(BF16), bias_ref[...])

    @pl.when(j == pl.num_programs(1) - 1)
    def _():
        kv_new = kv_new_ref[...]
        attend(kv_new[:kw], kv_new[kw:], bias_new_ref[...])
        o_ref[...] = acc_ref[...] / l_ref[...]


def _dsa_sample(layer, page_table, cache_kv_t, q, bias, kv_new_t, bias_new, *, pages):
    n_seq, n_pages = page_table.shape
    n_steps = n_pages // pages
    n_main = pages * PAGE_SIZE
    rows = q.shape[1]
    n_tail = kv_new_t.shape[2]
    kw = KVH_B * HEAD_DIM
    in_specs = [HBM_SPEC, _seq_spec((rows, kw)),
                pl.BlockSpec((None, None, 8, n_main), lambda b, j, pt: (b, j, 0, 0)),
                _seq_spec((2 * kw, n_tail)), _seq_spec((8, n_tail))]
    return pl.pallas_call(
        functools.partial(_dsa_sample_kernel, layer=layer, pages=pages),
        out_shape=jax.ShapeDtypeStruct((n_seq, rows, kw), F32),
        grid_spec=pltpu.PrefetchScalarGridSpec(
            num_scalar_prefetch=1, grid=(n_seq, n_steps), in_specs=in_specs,
            out_specs=pl.BlockSpec((None, rows, kw), lambda b, j, pt: (b, 0, 0)),
            scratch_shapes=_page_scratch(pages, 2 * kw) + [
                pltpu.VMEM((rows, 1), F32), pltpu.VMEM((rows, 1), F32), pltpu.VMEM((rows, kw), F32)]),
        compiler_params=_params("arbitrary", "arbitrary"),
        name="dsa_sample")(page_table, cache_kv_t, q, bias, kv_new_t, bias_new)


def _router_kernel(x_ref, g_ref, w_ref, b_ref, tri_ref, idx_ref, gate_ref, rank_ref, count_ref, carry_ref):
    @pl.when(pl.program_id(0) == 0)
    def _():
        carry_ref[...] = jnp.zeros(carry_ref.shape, F32)

    x = x_ref[...]
    ms = jnp.mean(x * x, axis=-1, keepdims=True)
    h = x * lax.rsqrt(ms + NORM_EPS) * g_ref[...]
    logit = jnp.dot(h, w_ref[...], preferred_element_type=F32, precision=lax.Precision.HIGHEST) + b_ref[...]
    tm, n_e = logit.shape
    lane = lax.broadcasted_iota(jnp.int32, (tm, n_e), 1)
    picks, vals = [], []
    for _ in range(TOP_K):
        top = jnp.max(logit, axis=1, keepdims=True)
        first = jnp.min(jnp.where(logit == top, lane, n_e), axis=1, keepdims=True)
        picks.append(first)
        vals.append(top)
        logit = jnp.where(lane == first, -jnp.inf, logit)
    exps = [jnp.exp(v - vals[0]) for v in vals]
    denom = exps[0]
    for e in exps[1:]:
        denom = denom + e
    chosen = jnp.zeros((tm, n_e), F32)
    for first in picks:
        chosen = chosen + jnp.where(lane == first, 1.0, 0.0)
    before = jnp.dot(tri_ref[...], chosen.astype(BF16), preferred_element_type=F32) + carry_ref[...]
    for k in range(TOP_K):
        idx_ref[:, k:k + 1] = picks[k]
        gate_ref[:, k:k + 1] = exps[k] / denom
        rank_ref[:, k:k + 1] = jnp.sum(jnp.where(lane == picks[k], before, 0.0), axis=1, keepdims=True).astype(jnp.int32)
    carry_ref[...] = carry_ref[...] + jnp.sum(chosen, axis=0, keepdims=True)
    count_ref[...] = carry_ref[...].astype(jnp.int32)


def _router(x, gain, w_router, b_router, *, tm=512):
    n, d = x.shape
    e = w_router.shape[1]
    tri = jnp.asarray(np.arange(tm)[:, None] > np.arange(tm)[None, :], BF16)
    pick = lambda dt: jax.ShapeDtypeStruct((n, TOP_K), dt)
    pick_spec = pl.BlockSpec((tm, TOP_K), lambda i: (i, 0))
    return pl.pallas_call(
        _router_kernel,
        out_shape=(pick(jnp.int32), pick(F32), pick(jnp.int32), jax.ShapeDtypeStruct((1, e), jnp.int32)),
        grid=(n // tm,),
        in_specs=[pl.BlockSpec((tm, d), lambda i: (i, 0)), pl.BlockSpec((1, d), lambda i: (0, 0)),
                  pl.BlockSpec((d, e), lambda i: (0, 0)), pl.BlockSpec((1, e), lambda i: (0, 0)),
                  pl.BlockSpec((tm, tm), lambda i: (0, 0))],
        out_specs=(pick_spec, pick_spec, pick_spec, pl.BlockSpec((1, e), lambda i: (0, 0))),
        scratch_shapes=[pltpu.VMEM((1, e), F32)],
        compiler_params=_params("arbitrary"),
        name="router")(x, gain.reshape(1, d), w_router, b_router.reshape(1, e), tri)


def _expert_kernel(be_ref, bv_ref, x_ref, g_ref, w1_ref, b1_ref, w2_ref, b2_ref, o_ref):
    i = pl.program_id(0)

    @pl.when(bv_ref[i] != 0)
    def _():
        d_ff = w2_ref.shape[0]
        x = x_ref[...]
        ms = jnp.mean(x * x, axis=-1, keepdims=True)
        h = (x * lax.rsqrt(ms + NORM_EPS) * g_ref[...]).astype(BF16)
        u = jnp.dot(h, w1_ref[...].astype(BF16), preferred_element_type=F32) + b1_ref[...]
        g = jnp.minimum(u[:, :d_ff], SWIGLU_LIMIT)
        up = jnp.clip(u[:, d_ff:], -SWIGLU_LIMIT, SWIGLU_LIMIT)
        a = g * jax.nn.sigmoid(SWIGLU_ALPHA * g) * (up + 1.0)
        o_ref[...] = jnp.dot(a.astype(BF16), w2_ref[...].astype(BF16), preferred_element_type=F32) + b2_ref[...]

    @pl.when(bv_ref[i] == 0)
    def _():
        o_ref[...] = jnp.zeros(o_ref.shape, o_ref.dtype)


def _experts(block_e, block_valid, xb, gain, w1, b1, w2, b2, *, tm):
    n_rows, d = xb.shape
    e, _, two_ff = w1.shape
    d_ff = w2.shape[1]
    return pl.pallas_call(
        _expert_kernel,
        out_shape=jax.ShapeDtypeStruct((n_rows, d), F32),
        grid_spec=pltpu.PrefetchScalarGridSpec(
            num_scalar_prefetch=2, grid=(n_rows // tm,),
            in_specs=[pl.BlockSpec((tm, d), lambda i, be, bv: (i, 0)),
                      pl.BlockSpec((1, d), lambda i, be, bv: (0, 0)),
                      pl.BlockSpec((None, d, two_ff), lambda i, be, bv: (be[i], 0, 0)),
                      pl.BlockSpec((None, 1, two_ff), lambda i, be, bv: (be[i], 0, 0)),
                      pl.BlockSpec((None, d_ff, d), lambda i, be, bv: (be[i], 0, 0)),
                      pl.BlockSpec((None, 1, d), lambda i, be, bv: (be[i], 0, 0))],
            out_specs=pl.BlockSpec((tm, d), lambda i, be, bv: (i, 0))),
        compiler_params=_params("arbitrary"),
        name="experts")(block_e, block_valid, xb, gain.reshape(1, d), w1, b1.reshape(e, 1, two_ff), w2,
                        b2.reshape(e, 1, d))


def _moe(x, gain, w_router, b_router, w1, b1, w2, b2, *, tm_e=256):
    n, d = x.shape
    top_idx, gate, rank, counts = _router(x, gain, w_router, b_router)
    flat_e = top_idx.reshape(-1)
    nk = flat_e.shape[0]
    counts = counts[0]
    start = jnp.cumsum(counts) - counts
    padded = _cdiv(counts, tm_e) * tm_e
    pad_end = jnp.cumsum(padded)
    pad_start = pad_end - padded
    dest = pad_start[flat_e] + rank.reshape(-1)
    order = jnp.argsort(dest).astype(jnp.int32)
    n_blocks = _cdiv(nk, tm_e) + N_EXPERTS
    block_start = jnp.arange(n_blocks, dtype=jnp.int32) * tm_e
    block_e = jnp.minimum(jnp.sum((block_start[:, None] >= pad_end[None, :]).astype(jnp.int32), axis=1), N_EXPERTS - 1)
    block_valid = (block_start < pad_end[-1]).astype(jnp.int32)
    row = jnp.arange(n_blocks * tm_e, dtype=jnp.int32)
    row_e = jnp.repeat(block_e, tm_e)
    offset = row - pad_start[row_e]
    src = jnp.clip(start[row_e] + offset, 0, nk - 1)
    row_tok = jnp.where(offset < counts[row_e], order[src] // TOP_K, 0)
    yb = _experts(block_e, block_valid, x[row_tok], gain, w1, b1, w2, b2, tm=tm_e)
    dest_k = dest.reshape(n, TOP_K)
    y = x
    for k in range(TOP_K):
        y = y + yb[dest_k[:, k]] * gate[:, k:k + 1]
    return y


def _rms(x, g, n=None):
    n = x.shape[-1] if n is None else n
    y = x * lax.rsqrt(jnp.sum(x * x, axis=-1, keepdims=True) / n + NORM_EPS)
    return y * g


def _rope_tables(pos, n_rot):
    half = n_rot // 2
    inv_freq = ROPE_THETA ** (-jnp.arange(half, dtype=F32) / half)
    ang = pos.astype(F32)[:, None] * inv_freq[None, :]
    return jnp.cos(ang), jnp.sin(ang)


def _rope(x, cos, sin, n_rot):
    d = x.shape[-1]
    half = n_rot // 2
    cos = cos[:, None, :]
    sin = sin[:, None, :]
    x1 = x[..., d - n_rot:d - half]
    x2 = x[..., d - half:]
    return jnp.concatenate([x[..., :d - n_rot], x1 * cos - x2 * sin, x2 * cos + x1 * sin], axis=-1)


def _regroup_w_in(w_in):
    cuts = np.cumsum([0, Q_LORA, KV_LORA, ROPE_A, H_B * HEAD_DIM, KVH_B * HEAD_DIM, KVH_B * HEAD_DIM,
                      H_I * D_I, D_I, H_I, 2 * w_in.shape[0]])
    w_cq, w_ckv, w_kpe, w_qb, w_kb, w_vb, w_iq, w_ik, w_iw, w_g = [w_in[:, a:b] for a, b in zip(cuts[:-1], cuts[1:])]
    misc_pad = jnp.zeros((w_in.shape[0], COL_GATE - COL_MISC - ROPE_A - D_I - H_I), w_in.dtype)
    return jnp.concatenate([w_cq, w_ckv, w_qb, w_kb, w_vb, w_iq, w_kpe, w_ik, w_iw, misc_pad, w_g], axis=1).astype(BF16)


def _layer(l, xp, xs, page_table, caches, wts, *, t_real):
    (cache_mla, cache_dsa_kv, cache_dsa_idx) = caches
    n_b, t_pad, d = xp.shape
    n_seq, n_ds, _ = xs.shape
    past = page_table.shape[1] * PAGE_SIZE
    n_p = n_b * t_pad
    n_s = n_seq * n_ds
    n_sel_p = min(MAX_SEL, (t_real - N_META) // 4)
    n_sel_s = min(MAX_SEL, (past + n_ds) // 4)

    x_all = jnp.concatenate([xp.reshape(n_p, d), xs.reshape(n_s, d)], axis=0)
    n_all = _round_up(n_p + n_s, 1024)
    x_all = jnp.pad(x_all, ((0, n_all - n_p - n_s), (0, 0)))
    pos = jnp.concatenate([jnp.tile(jnp.arange(t_pad), n_b), jnp.tile(past + jnp.arange(n_ds), n_seq),
                           jnp.zeros((n_all - n_p - n_s,), jnp.int32)])
    z = _matmul(x_all, _regroup_w_in(wts["w_in"][l]), gain=wts["g_attn_norm"][l], tm=1024, tn=1024, name="proj_in")

    cos_a, sin_a = _rope_tables(pos, ROPE_A)
    cos_b, sin_b = _rope_tables(pos, HEAD_DIM)
    cos_i, sin_i = _rope_tables(pos, D_I_ROPE)
    ones = lambda w: jnp.ones((n_all, w), F32)
    zeros = lambda w: jnp.zeros((n_all, w), F32)
    cos_a_t = jnp.concatenate([ones(NOPE_A), cos_a, cos_a, zeros(LANE - DQK_A)], axis=1)
    sin_a_t = jnp.concatenate([zeros(NOPE_A), -sin_a, sin_a, zeros(LANE - DQK_A)], axis=1)
    cos_b_t = jnp.tile(jnp.concatenate([cos_b, cos_b], axis=1), (1, LANE // HEAD_DIM))
    sin_b_t = jnp.tile(jnp.concatenate([-sin_b, sin_b], axis=1), (1, LANE // HEAD_DIM))
    cos_i_t = jnp.tile(jnp.concatenate([ones(D_I - D_I_ROPE), cos_i, cos_i], axis=1), (1, LANE // D_I))
    sin_i_t = jnp.tile(jnp.concatenate([zeros(D_I - D_I_ROPE), -sin_i, sin_i], axis=1), (1, LANE // D_I))
    pad_gain = lambda g: jnp.pad(g, (0, LANE - DQK_A)).reshape(1, LANE)
    tm = 512

    w_uq = jnp.pad(wts["w_uq"][l], ((0, 0), (0, 0), (0, LANE - DQK_A))).reshape(Q_LORA, H_A * LANE).astype(BF16)
    (q_a,) = _row_call(_q_up_kernel, n_all, tm,
                       [(z, Q_LORA, COL_CQ // Q_LORA), (wts["g_cq"][l].reshape(1, Q_LORA), None, 0), (w_uq, None, 0),
                        (pad_gain(wts["g_qa"][l] * (DQK_A ** -0.5)), None, 0), (cos_a_t, LANE, 0), (sin_a_t, LANE, 0)],
                       [(H_A * LANE, BF16)], "q_up")
    sl = slice(n_p, n_p + n_s)
    tm_s = min(tm, n_s)
    (q_a_s,) = _row_call(_q_up_kernel, n_s, tm_s,
                         [(z[sl], Q_LORA, COL_CQ // Q_LORA), (wts["g_cq"][l].reshape(1, Q_LORA), None, 0), (w_uq, None, 0),
                          (pad_gain(wts["g_qa"][l] * (DQK_A ** -0.5)), None, 0), (cos_a_t[sl], LANE, 0),
                          (sin_a_t[sl], LANE, 0)],
                         [(H_A * LANE, F32)], "q_up_sample")

    w_uk = jnp.pad(wts["w_uk"][l], ((0, 0), (0, 0), (0, LANE - NOPE_A))).reshape(KV_LORA, H_A * LANE)
    w_kv = jnp.concatenate([w_uk, wts["w_uv"][l].reshape(KV_LORA, H_A * V_A)], axis=1).astype(BF16)
    place = np.zeros((LANE, H_A * LANE), np.float32)
    for h in range(H_A):
        place[np.arange(ROPE_A), h * LANE + NOPE_A + np.arange(ROPE_A)] = 1.0
    k_a, v_a = _row_call(_kv_up_kernel, n_all, tm,
                         [(z, KV_LORA, COL_CKV // KV_LORA), (wts["g_ckv"][l].reshape(1, KV_LORA), None, 0),
                          (z, LANE, COL_MISC // LANE), (w_kv, None, 0), (jnp.asarray(place, BF16), None, 0),
                          (pad_gain(wts["g_ka"][l]), None, 0), (cos_a_t, LANE, 0), (sin_a_t, LANE, 0)],
                         [(H_A * LANE, BF16), (H_A * V_A, BF16)], "kv_up")

    expand = np.zeros((H_B * HEAD_DIM, H_B * KVH_B * HEAD_DIM), np.float32)
    for h in range(H_B):
        cols = h * KVH_B * HEAD_DIM + (h // GROUP_B) * HEAD_DIM + np.arange(HEAD_DIM)
        expand[h * HEAD_DIM + np.arange(HEAD_DIM), cols] = 1.0
    two = lambda g: jnp.tile(g, LANE // HEAD_DIM).reshape(1, LANE)
    g_ik_t = jnp.pad(wts["g_ik"][l], (ROPE_A, LANE - ROPE_A - D_I)).reshape(1, LANE)
    mla_rows, kv_rows, i_k, i_w, q_b_pad, k_b2, v_b2, i_q, i_k_bf = _row_call(
        _post_z_kernel, n_all, tm,
        [(z, COL_GATE, 0), (wts["g_ckv"][l].reshape(1, KV_LORA), None, 0),
         (two(wts["g_qb"][l] * (HEAD_DIM ** -0.5)), None, 0), (two(wts["g_kb"][l]), None, 0), (g_ik_t, None, 0),
         (cos_b_t, LANE, 0), (sin_b_t, LANE, 0), (cos_i_t, LANE, 0), (sin_i_t, LANE, 0),
         (jnp.asarray(expand, BF16), None, 0)],
        [(KV_LORA + ROPE_A, F32), (2 * KVH_B * HEAD_DIM, F32), (D_I, F32), (H_I, F32), (H_B * LANE, BF16),
         (KVH_B * HEAD_DIM, BF16), (KVH_B * HEAD_DIM, BF16), (H_I * D_I, BF16), (D_I, BF16)], "post_z")
    q_b_pad = q_b_pad.reshape(n_all, H_B, KVH_B * HEAD_DIM)
    i_q = i_q.reshape(n_all, H_I, D_I)

    tq = 128
    nq = t_pad // tq
    n_kc = max(dd for dd in range(1, 5) if nq % dd == 0)
    tk = n_kc * tq
    o_a_p = _mla_prompt(q_a[:n_p].reshape(n_b, t_pad, H_A * LANE), k_a[:n_p].reshape(n_b, t_pad, H_A * LANE),
                        v_a[:n_p].reshape(n_b, t_pad, H_A * V_A), tq=tk)

    blocked = lambda a: a.reshape((n_b, nq, tq) + a.shape[1:]).swapaxes(2, 3)
    o_b_p = _dsa_prompt(blocked(i_q[:n_p]), i_w[:n_p].reshape(n_b, t_pad, H_I), blocked(q_b_pad[:n_p]),
                        i_k_bf[:n_p].reshape(n_b, t_pad, D_I), k_b2[:n_p].reshape(n_b, t_pad, KVH_B * HEAD_DIM),
                        v_b2[:n_p].reshape(n_b, t_pad, KVH_B * HEAD_DIM), tq=tq, tk=tk, n_sel=n_sel_p)
    o_b_p = o_b_p.reshape(n_b, nq, KVH_B, GROUP_B, tq, KVH_B, HEAD_DIM)
    o_b_p = jnp.stack([o_b_p[:, :, g, :, :, g, :] for g in range(KVH_B)], axis=2)
    o_b_p = o_b_p.reshape(n_b, nq, H_B, tq, HEAD_DIM).swapaxes(2, 3).reshape(n_p, H_B * HEAD_DIM)

    pages = max(dd for dd in range(1, MAX_PAGES_PER_STEP + 1) if page_table.shape[1] % dd == 0)
    assert 8 % n_ds == 0
    n_tail = LANE
    tail_t = lambda a: jnp.pad(a.reshape(n_seq, n_ds, a.shape[-1]), ((0, 0), (0, n_tail - n_ds), (0, 0))).swapaxes(1, 2)
    cache_mla_t = jnp.swapaxes(cache_mla, 2, 3)
    cache_idx_t = jnp.swapaxes(cache_dsa_idx, 2, 3)
    cache_kv_t = jnp.moveaxis(cache_dsa_kv, 2, -1).reshape(cache_dsa_kv.shape[:2] + (2 * KVH_B * HEAD_DIM, PAGE_SIZE))
    g_ka = wts["g_ka"][l]
    pos_k = jnp.arange(past + n_tail)
    cos_k, sin_k = _rope_tables(pos_k, ROPE_A)
    g1, g2 = g_ka[NOPE_A:NOPE_A + ROPE_A // 2], g_ka[NOPE_A + ROPE_A // 2:]
    cg_t = jnp.concatenate([cos_k * g1, cos_k * g2], axis=1).T
    sg_t = jnp.concatenate([sin_k * g1, -sin_k * g2], axis=1).T
    qa_s = q_a_s.reshape(n_s, H_A, LANE)
    rows_a = n_ds * H_A
    eye_h = jnp.eye(H_A, dtype=F32)
    qn_s = qa_s[..., :NOPE_A] * g_ka[:NOPE_A]
    qbd = (qn_s[:, :, None, :] * eye_h[None, :, :, None]).reshape(n_seq, rows_a, H_A * NOPE_A)
    q_r = qa_s[:, :, NOPE_A:DQK_A].reshape(n_seq, rows_a, ROPE_A)
    q_r_swap = jnp.concatenate([q_r[..., ROPE_A // 2:], q_r[..., :ROPE_A // 2]], axis=-1)
    wt = wts["w_uk"][l].reshape(KV_LORA, H_A * NOPE_A).T
    o_lat = _mla_sample(l, page_table, cache_mla_t, wt.astype(BF16), qbd.astype(BF16), q_r.astype(BF16),
                        q_r_swap.astype(BF16), cg_t[:, :past], sg_t[:, :past], tail_t(mla_rows[sl]),
                        cg_t[:, past:], sg_t[:, past:], pages=pages, n_new=n_ds)
    o_a_s = jnp.einsum("bqhr,rhv->bqhv", o_lat.reshape(n_seq, n_ds, H_A, KV_LORA), wts["w_uv"][l])
    o_a_s = o_a_s.reshape(n_s, H_A * V_A)

    slot = jnp.arange(8) % n_ds
    iq_s = i_q[sl].reshape(n_seq, n_ds, H_I, D_I)[:, slot].swapaxes(1, 2).reshape(n_seq, H_I * 8, D_I)
    iw_s = i_w[sl].reshape(n_seq, n_ds, H_I)[:, slot].swapaxes(1, 2).reshape(n_seq, H_I * 8, 1)
    bias, bias_new = _dsa_select(l, page_table, cache_idx_t, iq_s, iw_s, tail_t(i_k_bf[sl]), pages=pages,
                                 n_new=n_ds, n_sel=n_sel_s)
    qb_s = q_b_pad[sl].reshape(n_seq, n_ds, H_B, KVH_B * HEAD_DIM).swapaxes(1, 2).reshape(n_seq, H_B * n_ds, -1)
    kv_new_t = tail_t(kv_rows[sl]).astype(BF16)
    o_b_s = _dsa_sample(l, page_table, cache_kv_t, qb_s, bias, kv_new_t, bias_new, pages=pages)
    o_b_s = o_b_s.reshape(n_seq, KVH_B, GROUP_B, n_ds, KVH_B, HEAD_DIM)
    o_b_s = jnp.stack([o_b_s[:, g, :, :, g, :] for g in range(KVH_B)], axis=1)
    o_b_s = o_b_s.reshape(n_seq, H_B, n_ds, HEAD_DIM).swapaxes(1, 2).reshape(n_s, H_B * HEAD_DIM)

    zeros_tail = jnp.zeros((n_all - n_p - n_s, H_A * V_A), BF16)
    o_a = jnp.concatenate([o_a_p.reshape(n_p, H_A * V_A), o_a_s.astype(BF16), zeros_tail], axis=0)
    o_b = jnp.concatenate([o_b_p, o_b_s.astype(BF16), zeros_tail], axis=0)
    assert COL_GATE % (2 * d) == 0
    x_all = _merge(x_all, o_a, o_b, z, wts["w_pa"][l].astype(BF16), wts["w_pb"][l].astype(BF16),
                   wts["w_o"][l].astype(BF16))
    x_all = _moe(x_all, wts["g_ffn_norm"][l], wts["w_router"][l], wts["b_router"][l], wts["w_mlp1"][l],
                 wts["b_mlp1"][l], wts["w_mlp2"][l], wts["b_mlp2"][l])

    new_p = (mla_rows[:n_p].reshape(n_b, t_pad, -1)[:, :t_real],
             kv_rows[:n_p].reshape(n_b, t_pad, 2, KVH_B, HEAD_DIM)[:, :t_real],
             i_k[:n_p].reshape(n_b, t_pad, D_I)[:, :t_real])
    new_s = (mla_rows[sl].reshape(n_seq, n_ds, -1), kv_rows[sl].reshape(n_seq, n_ds, 2, KVH_B, HEAD_DIM),
             i_k[sl].reshape(n_seq, n_ds, D_I))
    return x_all[:n_p].reshape(n_b, t_pad, d), x_all[sl].reshape(n_seq, n_ds, d), new_p, new_s


def kernel(x_prompt, x_sample, cache_mla, cache_dsa_kv, cache_dsa_idx, page_table, meta_tokens, g_attn_norm, w_in,
           g_cq, w_uq, g_qa, g_ckv, w_uk, w_uv, g_ka, g_qb, g_kb, g_ik, w_pa, w_pb, w_o, g_ffn_norm, w_router,
           b_router, w_mlp1, b_mlp1, w_mlp2, b_mlp2):
    n_b, n_s, d = x_prompt.shape
    t_real = n_s + N_META
    t_pad = _round_up(t_real, LANE)
    wts = dict(g_attn_norm=g_attn_norm, w_in=w_in, g_cq=g_cq, w_uq=w_uq, g_qa=g_qa, g_ckv=g_ckv, w_uk=w_uk, w_uv=w_uv,
               g_ka=g_ka, g_qb=g_qb, g_kb=g_kb, g_ik=g_ik, w_pa=w_pa, w_pb=w_pb, w_o=w_o, g_ffn_norm=g_ffn_norm,
               w_router=w_router, b_router=b_router, w_mlp1=w_mlp1, b_mlp1=b_mlp1, w_mlp2=w_mlp2, b_mlp2=b_mlp2)
    meta = jnp.broadcast_to(meta_tokens[None].astype(x_prompt.dtype), (n_b, N_META, d))
    xp = jnp.concatenate([meta, x_prompt, jnp.zeros((n_b, t_pad - t_real, d), x_prompt.dtype)], axis=1)
    xs = x_sample
    outs_p, outs_s = [], []
    for l in range(w_in.shape[0]):
        xp, xs, new_p, new_s = _layer(l, xp, xs, page_table, (cache_mla, cache_dsa_kv, cache_dsa_idx), wts,
                                      t_real=t_real)
        outs_p.append(new_p)
        outs_s.append(new_s)
    stack = lambda outs, i: jnp.stack([o[i] for o in outs], axis=0)
    return (xp[:, N_META:t_real], xs, stack(outs_p, 0), stack(outs_p, 1), stack(outs_p, 2),
            stack(outs_s, 0), stack(outs_s, 1), stack(outs_s, 2))
```
